```python
import math
import jax
import jax.numpy as jnp
from jax import lax
import numpy as np

D_MODEL = 1024
BATCH = 8
SEQ = 2048
DEPTH = 4

GRID_W = 64
CTX_LEN = 256
HEAD_DIM = 64
D_MIX = D_MODEL
GDN_WIDTH = D_MIX // 4
GDN_HEADS = GDN_WIDTH // HEAD_DIM
GDN_CHUNK = 64
SHORT_CONV = 4
LRU_WIDTH = D_MIX // 4
LRU_BLOCKS = 4
LRU_BLOCK_W = LRU_WIDTH // LRU_BLOCKS
LRU_C = 8.0
ATT_WIDTH = D_MIX - GDN_WIDTH - LRU_WIDTH
ATT_Q_HEADS = ATT_WIDTH // HEAD_DIM
ATT_KV_HEADS = 2
ATT_GROUP = ATT_Q_HEADS // ATT_KV_HEADS
ATT_KV_WIDTH = ATT_KV_HEADS * HEAD_DIM
Q_BLOCK = 128
ROPE_THETA = 10000.0
ROPE_AXIS_DIM = HEAD_DIM // 2
GDN_PROJ = 4 * GDN_WIDTH + 4 * GDN_HEADS
LRU_PROJ = 2 * LRU_WIDTH
ATT_PROJ = ATT_WIDTH + 2 * ATT_KV_WIDTH
D_IN_PROJ = GDN_PROJ + LRU_PROJ + ATT_PROJ
MOE_GROUPS = 8
MOE_EXPERTS_PER_GROUP = 8
MOE_EXPERTS = MOE_GROUPS * MOE_EXPERTS_PER_GROUP
MOE_TOP_K = 2
MOE_HIDDEN = 512
MOE_BLOCK = 128
EPS = 1e-6

kernel_name = 'hybrid_prefix_dit_gdn_rglru_gqa_hmoe'


def rms_norm(x, g):
    xf = x.astype(jnp.float32)
    y = xf * lax.rsqrt(jnp.mean(xf * xf, axis=-1, keepdims=True) + EPS)
    return (y * g.astype(jnp.float32)).astype(x.dtype)


def l2_norm(x):
    return x * lax.rsqrt(jnp.sum(x * x, axis=-1, keepdims=True) + EPS)


def modulate(x, g, shift, scale):
    return rms_norm(x, g) * (1 + scale) + shift


def depthwise_conv(x, w, b=None):
    k = w.shape[0]
    lo = (k - 1) // 2
    y = lax.conv_general_dilated(x, w[:, None, :].astype(x.dtype), (1,), [(lo, k - 1 - lo)],
                                 dimension_numbers=('NWC', 'WIO', 'NWC'),
                                 feature_group_count=x.shape[-1])
    if b is not None:
        y = y + b
    return y


def axial_rope_tables(rows):
    row = jnp.repeat(jnp.arange(rows, dtype=jnp.float32), GRID_W)
    col = jnp.tile(jnp.arange(GRID_W, dtype=jnp.float32), rows)
    inv_freq = ROPE_THETA ** (-jnp.arange(0, ROPE_AXIS_DIM, 2, dtype=jnp.float32) / ROPE_AXIS_DIM)
    ang = jnp.concatenate([row[:, None] * inv_freq, col[:, None] * inv_freq], axis=-1)
    return jnp.cos(ang), jnp.sin(ang)


def rotate_pairs(x, cos, sin):
    x1, x2 = jnp.split(x, 2, axis=-1)
    return jnp.concatenate([x1 * cos - x2 * sin, x1 * sin + x2 * cos], axis=-1)


def apply_axial_rope(x, cos, sin):
    n = ROPE_AXIS_DIM // 2
    cos = cos[:, None, :].astype(x.dtype)
    sin = sin[:, None, :].astype(x.dtype)
    xr = rotate_pairs(x[..., :ROPE_AXIS_DIM], cos[..., :n], sin[..., :n])
    xc = rotate_pairs(x[..., ROPE_AXIS_DIM:], cos[..., n:], sin[..., n:])
    return jnp.concatenate([xr, xc], axis=-1)


def gated_delta_chunked(q, k, v, beta, g, state):
    b, t, h, dk = q.shape
    dv = v.shape[-1]
    n = t // GDN_CHUNK

    def chunks(a):
        a = a.reshape((b, n, GDN_CHUNK, h) + a.shape[3:])
        return jnp.moveaxis(a, (1, 3), (0, 2))

    qc, kc, vc, bc, gc = (chunks(a) for a in (q, k, v, beta, g))
    gcum = jnp.cumsum(gc, axis=-1)
    lower = jnp.tril(jnp.ones((GDN_CHUNK, GDN_CHUNK), dtype=bool))
    strict = jnp.tril(jnp.ones((GDN_CHUNK, GDN_CHUNK), dtype=bool), k=-1)
    decay = jnp.exp(jnp.where(lower, gcum[..., :, None] - gcum[..., None, :], -jnp.inf))
    kb = kc * bc[..., None]
    a_mat = jnp.where(strict, jnp.einsum('nbhid,nbhjd->nbhij', kb, kc) * decay, 0.0)
    eye = jnp.eye(GDN_CHUNK, dtype=q.dtype)
    rhs = jnp.concatenate([vc * bc[..., None], kb * jnp.exp(gcum)[..., None]], axis=-1)
    sol = lax.linalg.triangular_solve(eye + a_mat, rhs, left_side=True, lower=True)
    u, w = sol[..., :dv], sol[..., dv:]
    qk = jnp.where(lower, jnp.einsum('nbhid,nbhjd->nbhij', qc, kc) * decay, 0.0)

    def step(s, inp):
        q_i, k_i, u_i, w_i, g_i, qk_i = inp
        v_new = u_i - jnp.einsum('bhck,bhkv->bhcv', w_i, s)
        o = (jnp.einsum('bhck,bhkv->bhcv', q_i * jnp.exp(g_i)[..., None], s)
             + jnp.einsum('bhij,bhjv->bhiv', qk_i, v_new))
        g_last = g_i[..., -1:]
        s = (s * jnp.exp(g_last)[..., None]
             + jnp.einsum('bhck,bhcv->bhkv', k_i * jnp.exp(g_last - g_i)[..., None], v_new))
        return s, o

    state, o = lax.scan(step, state, (qc, kc, u, w, gcum, qk))
    o = jnp.moveaxis(o, (0, 2), (1, 3)).reshape(b, t, h, dv)
    return o, state


def gdn_mixer(p_lat, p_ctx, conv_w, a_log, dt_bias, norm_g, need_ctx):
    f32 = jnp.float32

    def prep(p):
        bsz, t, _ = p.shape
        qkv = jax.nn.silu(depthwise_conv(p[..., :3 * GDN_WIDTH], conv_w)).astype(f32)
        qkv = qkv.reshape(bsz, t, 3, GDN_HEADS, HEAD_DIM)
        q = l2_norm(qkv[:, :, 0]) * HEAD_DIM ** -0.5
        k = l2_norm(qkv[:, :, 1])
        v = qkv[:, :, 2]
        gate = p[..., 3 * GDN_WIDTH:4 * GDN_WIDTH]
        ab = p[..., 4 * GDN_WIDTH:].astype(f32).reshape(bsz, t, 2, 2, GDN_HEADS)
        g = -jnp.exp(a_log) * jax.nn.softplus(ab[:, :, 0] + dt_bias)
        beta = jax.nn.sigmoid(ab[:, :, 1])
        return q, k, v, beta, g, gate

    flip = lambda a: jnp.flip(a, axis=1)

    def bidir(q, k, v, beta, g, s0f, s0b):
        of, sf = gated_delta_chunked(q, k, v, beta[:, :, 0], g[:, :, 0], s0f)
        ob, sb = gated_delta_chunked(flip(q), flip(k), flip(v), flip(beta[:, :, 1]), flip(g[:, :, 1]), s0b)
        return of + flip(ob), sf, sb

    def readout(o, gate, dtype):
        bsz, t = o.shape[:2]
        y = rms_norm(o, norm_g).reshape(bsz, t, GDN_WIDTH) * jax.nn.silu(gate.astype(f32))
        return y.astype(dtype)

    qc, kc, vc, bc, gcx, gate_c = prep(p_ctx)
    s0 = jnp.zeros((qc.shape[0], GDN_HEADS, HEAD_DIM, HEAD_DIM), f32)
    oc, sf, sb = bidir(qc, kc, vc, bc, gcx, s0, s0)
    ql, kl, vl, bl, gl, gate_l = prep(p_lat)
    ol, _, _ = bidir(ql, kl, vl, bl, gl, sf, sb)
    out_l = readout(ol, gate_l, p_lat.dtype)
    out_c = readout(oc, gate_c, p_ctx.dtype) if need_ctx else None
    return out_l, out_c


def lru_coefficients(xr, w_r, b_r, w_i, b_i, lam):
    bsz, t, _ = xr.shape
    xb = xr.reshape(bsz, t, LRU_BLOCKS, LRU_BLOCK_W)
    r = jax.nn.sigmoid(jnp.einsum('btnd,nde->btne', xb, w_r).reshape(bsz, t, LRU_WIDTH) + b_r)
    i = jax.nn.sigmoid(jnp.einsum('btnd,nde->btne', xb, w_i).reshape(bsz, t, LRU_WIDTH) + b_i)
    log_a = -LRU_C * r * jax.nn.softplus(-lam)
    return jnp.exp(log_a), jnp.sqrt(-jnp.expm1(2.0 * log_a)) * (i * xr)


def linear_scan(a, bx, h0):
    def combine(e1, e2):
        return e1[0] * e2[0], e2[0] * e1[1] + e2[1]
    a_cum, b_cum = lax.associative_scan(combine, (a, bx), axis=1)
    h = a_cum * h0[:, None, :] + b_cum
    return h, h[:, -1]


def lru_mixer(p_lat, p_ctx, conv_w, conv_b, w_r, b_r, w_i, b_i, lam, need_ctx):
    f32 = jnp.float32

    def branches(p):
        xr = depthwise_conv(p[..., :LRU_WIDTH], conv_w, conv_b).astype(f32)
        y = jax.nn.gelu(p[..., LRU_WIDTH:].astype(f32), approximate=True)
        return xr, y

    def direction(xr, d, h0):
        a, bx = lru_coefficients(xr, w_r[d], b_r[d], w_i[d], b_i[d], lam[d])
        return linear_scan(a, bx, h0)

    flip = lambda a: jnp.flip(a, axis=1)
    xr_c, y_c = branches(p_ctx)
    xr_l, y_l = branches(p_lat)
    h0 = jnp.zeros((xr_c.shape[0], LRU_WIDTH), f32)
    hc_f, s_f = direction(xr_c, 0, h0)
    hc_b, s_b = direction(flip(xr_c), 1, h0)
    hl_f, _ = direction(xr_l, 0, s_f)
    hl_b, _ = direction(flip(xr_l), 1, s_b)
    out_l = ((hl_f + flip(hl_b)) * y_l).astype(p_lat.dtype)
    out_c = ((hc_f + flip(hc_b)) * y_c).astype(p_ctx.dtype) if need_ctx else None
    return out_l, out_c


def blocked_attention(q, k, v):
    bsz, t = q.shape[:2]
    nb = t // Q_BLOCK
    qb = jnp.moveaxis(q.reshape(bsz, nb, Q_BLOCK, ATT_KV_HEADS, ATT_GROUP, HEAD_DIM), 1, 0)
    scale = HEAD_DIM ** -0.5

    def one_block(q_blk):
        s = jnp.einsum('bqkgd,bskd->bkgqs', q_blk, k, preferred_element_type=jnp.float32) * scale
        p = jax.nn.softmax(s, axis=-1)
        return jnp.einsum('bkgqs,bskd->bqkgd', p.astype(v.dtype), v)

    o = lax.map(one_block, qb)
    return jnp.moveaxis(o, 0, 1).reshape(bsz, t, ATT_WIDTH)


def attn_mixer(p_lat, p_ctx, q_norm_g, k_norm_g, cos, sin, need_ctx):
    def heads(p):
        bsz, t, _ = p.shape
        q = p[..., :ATT_WIDTH].reshape(bsz, t, ATT_Q_HEADS, HEAD_DIM)
        k = p[..., ATT_WIDTH:ATT_WIDTH + ATT_KV_WIDTH].reshape(bsz, t, ATT_KV_HEADS, HEAD_DIM)
        v = p[..., ATT_WIDTH + ATT_KV_WIDTH:].reshape(bsz, t, ATT_KV_HEADS, HEAD_DIM)
        return rms_norm(q, q_norm_g), rms_norm(k, k_norm_g), v

    qc, kc, vc = heads(p_ctx)
    ql, kl, vl = heads(p_lat)
    ql = apply_axial_rope(ql, cos, sin)
    kl = apply_axial_rope(kl, cos, sin)
    out_l = blocked_attention(ql, jnp.concatenate([kl, kc], axis=1), jnp.concatenate([vl, vc], axis=1))
    out_c = blocked_attention(qc, kc, vc) if need_ctx else None
    return out_l.astype(p_lat.dtype), (out_c.astype(p_ctx.dtype) if need_ctx else None)


def expert_mlp(xb, w1, w3, w2):
    return (jax.nn.silu(xb @ w1) * (xb @ w3)) @ w2


def hier_moe(tok, w_group, b_group, w_expert, b_expert, w1, w3, w2):
    n, d = tok.shape
    p_group = jax.nn.softmax((tok @ w_group).astype(jnp.float32) + b_group, axis=-1)
    pg_top, g_idx = lax.top_k(p_group, 1)
    le = ((tok @ w_expert).astype(jnp.float32) + b_expert).reshape(n, MOE_GROUPS, MOE_EXPERTS_PER_GROUP)
    le = le[jnp.arange(n), g_idx[:, 0]]
    pe_top, e_local = lax.top_k(jax.nn.softmax(le, axis=-1), MOE_TOP_K)
    gate = pg_top * pe_top / jnp.sum(pe_top, axis=-1, keepdims=True)
    expert = g_idx * MOE_EXPERTS_PER_GROUP + e_local
    nk = n * MOE_TOP_K
    flat_e = expert.reshape(-1)
    order = jnp.argsort(flat_e)
    sorted_e = flat_e[order]
    token_of = order // MOE_TOP_K
    counts = jnp.bincount(flat_e, length=MOE_EXPERTS)
    padded = (counts + MOE_BLOCK - 1) // MOE_BLOCK * MOE_BLOCK
    pad_end = jnp.cumsum(padded)
    start = jnp.cumsum(counts) - counts
    dest = (pad_end - padded)[sorted_e] + jnp.arange(nk) - start[sorted_e]
    n_blocks = -(-nk // MOE_BLOCK) + MOE_EXPERTS
    slot_tok = jnp.full((n_blocks * MOE_BLOCK,), n, dtype=token_of.dtype).at[dest].set(token_of)
    block_e = jnp.minimum(jnp.searchsorted(pad_end, jnp.arange(n_blocks) * MOE_BLOCK, side='right'),
                          MOE_EXPERTS - 1)
    tok_pad = jnp.concatenate([tok, jnp.zeros((1, d), tok.dtype)], axis=0)
    xb = tok_pad[slot_tok].reshape(n_blocks, MOE_BLOCK, d)
    yb = lax.map(lambda a: expert_mlp(a[0], w1[a[1]], w3[a[1]], w2[a[1]]), (xb, block_e))
    y = yb.reshape(-1, d)[dest] * gate.reshape(-1)[order][:, None].astype(tok.dtype)
    return jnp.zeros_like(tok).at[token_of].add(y)


def trunk_layer(x, xc, mod, mod_c, lp, cos, sin, need_ctx):
    sh1, sc1, g1, sh2, sc2, g2 = jnp.split(mod[:, None, :].astype(x.dtype), 6, axis=-1)
    csh1, csc1, cg1, csh2, csc2, cg2 = jnp.split(mod_c.astype(x.dtype), 6, axis=-1)
    proj = modulate(x, lp['norm1_g'], sh1, sc1) @ lp['w_in']
    proj_c = modulate(xc, lp['norm1_g'], csh1, csc1) @ lp['w_in']
    o1, o2 = GDN_PROJ, GDN_PROJ + LRU_PROJ
    a_l, a_c = gdn_mixer(proj[..., :o1], proj_c[..., :o1], lp['gdn_conv_w'], lp['gdn_a_log'],
                         lp['gdn_dt_bias'], lp['gdn_norm_g'], need_ctx)
    b_l, b_c = lru_mixer(proj[..., o1:o2], proj_c[..., o1:o2], lp['lru_conv_w'], lp['lru_conv_b'],
                         lp['lru_w_r'], lp['lru_b_r'], lp['lru_w_i'], lp['lru_b_i'], lp['lru_lambda'], need_ctx)
    c_l, c_c = attn_mixer(proj[..., o2:], proj_c[..., o2:], lp['attn_q_norm_g'], lp['attn_k_norm_g'],
                          cos, sin, need_ctx)
    x = x + g1 * (jnp.concatenate([a_l, b_l, c_l], axis=-1) @ lp['w_out'])
    moe = lambda t: hier_moe(t, lp['moe_w_group'], lp['moe_b_group'], lp['moe_w_expert'],
                             lp['moe_b_expert'], lp['moe_w1'], lp['moe_w3'], lp['moe_w2'])
    h = modulate(x, lp['norm2_g'], sh2, sc2)
    if need_ctx:
        xc = xc + cg1 * (jnp.concatenate([a_c, b_c, c_c], axis=-1) @ lp['w_out'])
        hc = modulate(xc, lp['norm2_g'], csh2, csc2)
        n_lat = h.shape[0] * h.shape[1]
        y = moe(jnp.concatenate([h.reshape(n_lat, -1), hc.reshape(-1, hc.shape[-1])], axis=0))
        x = x + g2 * y[:n_lat].reshape(x.shape)
        xc = xc + cg2 * y[n_lat:].reshape(xc.shape)
    else:
        x = x + g2 * moe(h.reshape(-1, h.shape[-1])).reshape(x.shape)
    return x, xc


def setup_inputs(seed: int = 0) -> dict:
    key = jax.random.key(seed)
    ks = iter(jax.random.split(key, 40))
    f32 = jnp.float32
    L, D = DEPTH, D_MODEL

    def nrm(shape, scale):
        return jax.random.normal(next(ks), shape, f32) * scale

    def gain(shape):
        return 1.0 + nrm(shape, 0.05)

    x = nrm((BATCH, SEQ, D), 1.0)
    c = nrm((BATCH, D), 1.0)
    ctx = nrm((BATCH, CTX_LEN, D), 1.0)
    c_ctx = nrm((D,), 1.0)
    w_ada = nrm((L, D, 6 * D), 0.3 * D ** -0.5)
    b_ada = nrm((L, 6 * D), 0.02)
    norm1_g = gain((L, D))
    norm2_g = gain((L, D))
    w_in = nrm((L, D, D_IN_PROJ), D ** -0.5)
    w_out = nrm((L, D_MIX, D), D_MIX ** -0.5)
    gdn_conv_w = nrm((L, SHORT_CONV, 3 * GDN_WIDTH), SHORT_CONV ** -0.5)
    gdn_a_log = jnp.log(jax.random.uniform(next(ks), (L, 2, GDN_HEADS), f32, minval=1.0, maxval=16.0))
    dt = jnp.exp(jax.random.uniform(next(ks), (L, 2, GDN_HEADS), f32,
                                    minval=math.log(1e-3), maxval=math.log(1e-1)))
    gdn_dt_bias = dt + jnp.log(-jnp.expm1(-dt))
    gdn_norm_g = gain((L, HEAD_DIM))
    lru_conv_w = nrm((L, SHORT_CONV, LRU_WIDTH), SHORT_CONV ** -0.5)
    lru_conv_b = nrm((L, LRU_WIDTH), 0.01)
    lru_w_r = nrm((L, 2, LRU_BLOCKS, LRU_BLOCK_W, LRU_BLOCK_W), LRU_BLOCK_W ** -0.5)
    lru_b_r = nrm((L, 2, LRU_WIDTH), 0.01)
    lru_w_i = nrm((L, 2, LRU_BLOCKS, LRU_BLOCK_W, LRU_BLOCK_W), LRU_BLOCK_W ** -0.5)
    lru_b_i = nrm((L, 2, LRU_WIDTH), 0.01)
    a0 = jax.random.uniform(next(ks), (L, 2, LRU_WIDTH), f32, minval=0.9, maxval=0.999)
    s = a0 ** (1.0 / LRU_C)
    lru_lambda = jnp.log(s) - jnp.log1p(-s)
    attn_q_norm_g = gain((L, HEAD_DIM))
    attn_k_norm_g = gain((L, HEAD_DIM))
    moe_w_group = nrm((L, D, MOE_GROUPS), D ** -0.5)
    moe_b_group = nrm((L, MOE_GROUPS), 0.01)
    moe_w_expert = nrm((L, D, MOE_EXPERTS), D ** -0.5)
    moe_b_expert = nrm((L, MOE_EXPERTS), 0.01)
    moe_w1 = nrm((L, MOE_EXPERTS, D, MOE_HIDDEN), D ** -0.5)
    moe_w3 = nrm((L, MOE_EXPERTS, D, MOE_HIDDEN), D ** -0.5)
    moe_w2 = nrm((L, MOE_EXPERTS, MOE_HIDDEN, D), MOE_HIDDEN ** -0.5)
    final_norm_g = gain((D,))
    return {'x': x, 'c': c, 'ctx': ctx, 'c_ctx': c_ctx, 'w_ada': w_ada, 'b_ada': b_ada,
            'norm1_g': norm1_g, 'norm2_g': norm2_g, 'w_in': w_in, 'w_out': w_out,
            'gdn_conv_w': gdn_conv_w, 'gdn_a_log': gdn_a_log, 'gdn_dt_bias': gdn_dt_bias,
            'gdn_norm_g': gdn_norm_g, 'lru_conv_w': lru_conv_w, 'lru_conv_b': lru_conv_b,
            'lru_w_r': lru_w_r, 'lru_b_r': lru_b_r, 'lru_w_i': lru_w_i, 'lru_b_i': lru_b_i,
            'lru_lambda': lru_lambda, 'attn_q_norm_g': attn_q_norm_g, 'attn_k_norm_g': attn_k_norm_g,
            'moe_w_group': moe_w_group, 'moe_b_group': moe_b_group, 'moe_w_expert': moe_w_expert,
            'moe_b_expert': moe_b_expert, 'moe_w1': moe_w1, 'moe_w3': moe_w3, 'moe_w2': moe_w2,
            'final_norm_g': final_norm_g}


def reference(x, c, ctx, c_ctx, w_ada, b_ada, norm1_g, norm2_g, w_in, w_out,
              gdn_conv_w, gdn_a_log, gdn_dt_bias, gdn_norm_g,
              lru_conv_w, lru_conv_b, lru_w_r, lru_b_r, lru_w_i, lru_b_i, lru_lambda,
              attn_q_norm_g, attn_k_norm_g,
              moe_w_group, moe_b_group, moe_w_expert, moe_b_expert, moe_w1, moe_w3, moe_w2,
              final_norm_g):
    rows = x.shape[1] // GRID_W
    cos, sin = axial_rope_tables(rows)
    silu_c = jax.nn.silu(c)
    silu_cc = jax.nn.silu(c_ctx)
    xc = ctx
    for l in range(DEPTH):
        mod = silu_c @ w_ada[l] + b_ada[l]
        mod_c = silu_cc @ w_ada[l] + b_ada[l]
        lp = {'norm1_g': norm1_g[l], 'norm2_g': norm2_g[l], 'w_in': w_in[l], 'w_out': w_out[l],
              'gdn_conv_w': gdn_conv_w[l], 'gdn_a_log': gdn_a_log[l], 'gdn_dt_bias': gdn_dt_bias[l],
              'gdn_norm_g': gdn_norm_g[l], 'lru_conv_w': lru_conv_w[l], 'lru_conv_b': lru_conv_b[l],
              'lru_w_r': lru_w_r[l], 'lru_b_r': lru_b_r[l], 'lru_w_i': lru_w_i[l], 'lru_b_i': lru_b_i[l],
              'lru_lambda': lru_lambda[l], 'attn_q_norm_g': attn_q_norm_g[l],
              'attn_k_norm_g': attn_k_norm_g[l], 'moe_w_group': moe_w_group[l],
              'moe_b_group': moe_b_group[l], 'moe_w_expert': moe_w_expert[l],
              'moe_b_expert': moe_b_expert[l], 'moe_w1': moe_w1[l], 'moe_w3': moe_w3[l],
              'moe_w2': moe_w2[l]}
        x, xc = trunk_layer(x, xc, mod, mod_c, lp, cos, sin, need_ctx=l < DEPTH - 1)
    return rms_norm(x, final_norm_g)
```

```python
import functools

import jax
import jax.numpy as jnp
from jax import lax
from jax.experimental import pallas as pl
from jax.experimental.pallas import tpu as pltpu

F32 = jnp.float32
BF16 = jnp.bfloat16
HIGHEST = lax.Precision.HIGHEST

HEAD = 64
GDN_W = 256
GDN_H = GDN_W // HEAD
CHUNK = 64
LRU_W = 256
LRU_BLOCKS = 4
LRU_C = 8.0
ATT_W = 512
ATT_KV_W = 128
ATT_GROUP = 4
CONV_K = 4
N_GROUPS = 8
N_EXPERTS = 64
EXPERTS_PER_GROUP = 8
MOE_HIDDEN = 512
MOE_BLOCK = 128
GRID_W = 64
ROPE_THETA = 10000.0
EPS = 1e-6
LANES = 128
SUBLANES = 8
PAD_ROWS = 8
VMEM_LIMIT = 56 * 1024 * 1024

P_QKV = 3 * GDN_W
P_GATE = GDN_W
P_AB = LANES
P_LRU = 2 * LRU_W
P_ATT = ATT_W + 2 * ATT_KV_W
P_ALL = P_QKV + P_GATE + P_AB + P_LRU + P_ATT


def _cparams(sem):
    return pltpu.CompilerParams(dimension_semantics=sem, vmem_limit_bytes=VMEM_LIMIT)


def _sigmoid(x):
    return 1.0 / (1.0 + jnp.exp(-x))


def _silu(x):
    return x * _sigmoid(x)


def _softplus(x):
    return jnp.maximum(x, 0.0) + jnp.log(1.0 + jnp.exp(-jnp.abs(x)))


def _mm(a, b):
    return jnp.dot(a.astype(BF16), b.astype(BF16), preferred_element_type=F32)


def _mm_nt(a, b):
    return lax.dot_general(a.astype(BF16), b.astype(BF16), (((1,), (1,)), ((), ())),
                           preferred_element_type=F32)


def _mm_tn(a, b):
    return lax.dot_general(a.astype(BF16), b.astype(BF16), (((0,), (0,)), ((), ())),
                           preferred_element_type=F32)


def _mm_exact(a, b):
    return jnp.dot(a, b, precision=HIGHEST, preferred_element_type=F32)


def _group_mean_square(x):
    lane = lax.broadcasted_iota(jnp.int32, x.shape, 1)
    lo = lane < HEAD
    x2 = x * x
    s_lo = jnp.sum(jnp.where(lo, x2, 0.0), axis=-1, keepdims=True)
    s_hi = jnp.sum(jnp.where(lo, 0.0, x2), axis=-1, keepdims=True)
    return jnp.where(lo, s_lo, s_hi) * (1.0 / HEAD)


def _per_head(fn, x):
    n = x.shape[1] // LANES
    return jnp.concatenate([fn(x[:, i * LANES:(i + 1) * LANES], i) for i in range(n)], axis=1)


def _adaln_kernel(cv_ref, w_ref, b_ref, o_ref):
    o_ref[0] = _mm_exact(_silu(cv_ref[...]), w_ref[0]) + b_ref[0]


def _adaln(cv, w_ada, b_ada):
    n_layers, d, _ = w_ada.shape
    rows = cv.shape[0]
    return pl.pallas_call(
        _adaln_kernel,
        grid=(n_layers, 6),
        in_specs=[pl.BlockSpec((rows, d), lambda l, j: (0, 0)),
                  pl.BlockSpec((1, d, d), lambda l, j: (l, 0, j)),
                  pl.BlockSpec((1, 1, d), lambda l, j: (l * 6 + j, 0, 0))],
        out_specs=pl.BlockSpec((1, rows, d), lambda l, j: (l * 6 + j, 0, 0)),
        out_shape=jax.ShapeDtypeStruct((n_layers * 6, rows, d), F32),
        compiler_params=_cparams(("arbitrary", "arbitrary")),
        name="adaln",
    )(cv, w_ada, b_ada.reshape(n_layers * 6, 1, d))


def _inproj_kernel(x_ref, sh_ref, sc_ref, g_ref, w_ref, wabt_ref, gq_ref, gk_ref, cos_ref, sin_ref,
                   qkv_ref, gate_ref, ab_ref, abt_ref, lru_ref, q_ref, k_ref, v_ref):
    x = x_ref[...]
    ms = jnp.mean(x * x, axis=-1, keepdims=True)
    h = (x * lax.rsqrt(ms + EPS) * g_ref[0]) * (1.0 + sc_ref[0]) + sh_ref[0]
    hb = h.astype(BF16)
    p = jnp.dot(hb, w_ref[0], preferred_element_type=F32)
    o = 0
    qkv_ref[...] = p[:, o:o + P_QKV]
    o += P_QKV
    gate_ref[...] = p[:, o:o + P_GATE]
    o += P_GATE
    ab_ref[...] = p[:, o:o + P_AB]
    o += P_AB
    lru_ref[...] = p[:, o:o + P_LRU]
    o += P_LRU
    att = p[:, o:o + P_ATT]
    abt_ref[...] = lax.dot_general(wabt_ref[0], hb, (((1,), (1,)), ((), ())), preferred_element_type=F32)

    cos = cos_ref[...]
    sin = sin_ref[...]
    lane = lax.broadcasted_iota(jnp.int32, cos.shape, 1)
    first_half = (lane & 16) == 0

    def norm_rope(gain):
        def fn(s, _):
            y = s * lax.rsqrt(_group_mean_square(s) + EPS) * gain
            swapped = jnp.where(first_half, pltpu.roll(y, LANES - 16, 1), pltpu.roll(y, 16, 1))
            return y * cos + swapped * sin
        return fn

    q = _per_head(norm_rope(gq_ref[0]), att[:, :ATT_W])
    k = _per_head(norm_rope(gk_ref[0]), att[:, ATT_W:ATT_W + ATT_KV_W])
    q_ref[...] = (q * (HEAD ** -0.5)).astype(BF16)
    k_ref[...] = k.astype(BF16)
    v_ref[...] = att[:, ATT_W + ATT_KV_W:].astype(BF16)


def _inproj(layer, x, mod, norm1_g, w_pack, wabt, gq, gk, cos_t, sin_t, dims):
    b, c, s, d, tm, nt = dims
    n = x.shape[0]
    rows = mod.shape[0] // (6 * norm1_g.shape[0])

    def mod_idx(chunk):
        def f(i):
            r = jnp.where(i % nt == 0, b, i // nt)
            return ((layer * 6 + chunk) * rows + r, 0, 0)
        return f

    row_spec = lambda w: pl.BlockSpec((tm, w), lambda i: (i, 0))
    return pl.pallas_call(
        _inproj_kernel,
        grid=(n // tm,),
        in_specs=[row_spec(d),
                  pl.BlockSpec((1, 1, d), mod_idx(0)),
                  pl.BlockSpec((1, 1, d), mod_idx(1)),
                  pl.BlockSpec((1, 1, d), lambda i: (layer, 0, 0)),
                  pl.BlockSpec((1, d, P_ALL), lambda i: (layer, 0, 0)),
                  pl.BlockSpec((1, 16, d), lambda i: (layer, 0, 0)),
                  pl.BlockSpec((1, 1, LANES), lambda i: (layer, 0, 0)),
                  pl.BlockSpec((1, 1, LANES), lambda i: (layer, 0, 0)),
                  pl.BlockSpec((tm, LANES), lambda i: (i % nt, 0)),
                  pl.BlockSpec((tm, LANES), lambda i: (i % nt, 0))],
        out_specs=[row_spec(P_QKV), row_spec(P_GATE), row_spec(P_AB),
                   pl.BlockSpec((16, tm), lambda i: (0, i)),
                   row_spec(P_LRU), row_spec(ATT_W), row_spec(ATT_KV_W), row_spec(ATT_KV_W)],
        out_shape=[jax.ShapeDtypeStruct((n, P_QKV), F32), jax.ShapeDtypeStruct((n, P_GATE), F32),
                   jax.ShapeDtypeStruct((n, P_AB), F32), jax.ShapeDtypeStruct((16, n), F32),
                   jax.ShapeDtypeStruct((n, P_LRU), F32), jax.ShapeDtypeStruct((n, ATT_W), BF16),
                   jax.ShapeDtypeStruct((n, ATT_KV_W), BF16), jax.ShapeDtypeStruct((n, ATT_KV_W), BF16)],
        compiler_params=_cparams(("arbitrary",)),
        name="inproj",
    )(x, mod, mod, norm1_g, w_pack, wabt, gq, gk, cos_t, sin_t)


def _conv_rows(xp_ref, w, r0, rt, c):
    base = r0 + PAD_ROWS
    xm1 = xp_ref[base - 1:base - 1 + rt, :]
    x0 = xp_ref[base:base + rt, :]
    xp1 = xp_ref[base + 1:base + 1 + rt, :]
    xp2 = xp_ref[base + 2:base + 2 + rt, :]
    row = r0 + lax.broadcasted_iota(jnp.int32, (rt, 1), 0)
    xm1 = jnp.where(row == c, 0.0, xm1)
    xp1 = jnp.where(row == c - 1, 0.0, xp1)
    xp2 = jnp.where((row == c - 1) | (row == c - 2), 0.0, xp2)
    return w[0:1, :] * xm1 + w[1:2, :] * x0 + w[2:3, :] * xp1 + w[3:4, :] * xp2


def _fill_padded(xp_ref, x, t):
    zeros = jnp.zeros((PAD_ROWS, xp_ref.shape[1]), F32)
    xp_ref[0:PAD_ROWS, :] = zeros
    xp_ref[PAD_ROWS + t:2 * PAD_ROWS + t, :] = zeros
    xp_ref[PAD_ROWS:PAD_ROWS + t, :] = x


def _gdn_kernel(c, t, rt, qkv_ref, gate_ref, ab_ref, abt_ref, cw_ref, alog_r_ref, dt_r_ref, alog_c_ref,
                dt_c_ref, ng_ref, out_ref,
                xp_ref, q_s, k_s, v_s, g_s, gcf_s, gcb_s, gr_s, st_s, o_s):
    nch = t // CHUNK
    nc = c // CHUNK
    nl = nch - nc

    _fill_padded(xp_ref, qkv_ref[...], t)
    cw = cw_ref[0]

    def l2n(s, _):
        return s * lax.rsqrt(_group_mean_square(s) * HEAD + EPS)

    for r0 in range(0, t, rt):
        y = _silu(_conv_rows(xp_ref, cw, r0, rt, c))
        q_s[r0:r0 + rt, :] = _per_head(l2n, y[:, :GDN_W]) * (HEAD ** -0.5)
        k_s[r0:r0 + rt, :] = _per_head(l2n, y[:, GDN_W:2 * GDN_W])
        v_s[r0:r0 + rt, :] = y[:, 2 * GDN_W:]

    ab = ab_ref[...]
    lane = lax.broadcasted_iota(jnp.int32, ab.shape, 1)
    gval = -jnp.exp(alog_r_ref[0]) * _softplus(ab + dt_r_ref[0])
    g_s[...] = jnp.where(lane < 2 * GDN_H, gval, _sigmoid(ab))
    abt = abt_ref[...]
    g_t = -jnp.exp(alog_c_ref[0][:, 0:1]) * _softplus(abt + dt_c_ref[0][:, 0:1])

    ri = lax.broadcasted_iota(jnp.int32, (CHUNK, CHUNK), 0)
    ci = lax.broadcasted_iota(jnp.int32, (CHUNK, CHUNK), 1)
    lower = ri >= ci
    upper = ri <= ci
    lower_f = lower.astype(F32)
    upper_f = upper.astype(F32)
    row8 = lax.broadcasted_iota(jnp.int32, (2 * GDN_H, CHUNK), 0)
    for ch in range(nch):
        r0 = ch * CHUNK
        gch = g_s[r0:r0 + CHUNK, :]
        gcf_s[r0:r0 + CHUNK, :] = _mm_exact(lower_f, gch)
        gcb_s[r0:r0 + CHUNK, :] = _mm_exact(upper_f, gch)
        gtc = g_t[0:2 * GDN_H, r0:r0 + CHUNK]
        gr_s[ch] = jnp.where(row8 < GDN_H, _mm_exact(gtc, upper_f), _mm_exact(gtc, lower_f))

    st_s[...] = jnp.zeros(st_s.shape, F32)
    o_s[...] = jnp.zeros(o_s.shape, F32)

    def chunk_update(direction, ch):
        r0 = pl.multiple_of(ch * CHUNK, CHUNK)
        incl = lower if direction == 0 else upper
        strict = (ri > ci) if direction == 0 else (ri < ci)
        gcm = (gcf_s if direction == 0 else gcb_s)[pl.ds(r0, CHUNK), :]
        gall = g_s[pl.ds(r0, CHUNK), :]
        grow = gr_s[ch]
        qa = q_s[pl.ds(r0, CHUNK), :]
        ka = k_s[pl.ds(r0, CHUNK), :]
        va = v_s[pl.ds(r0, CHUNK), :]
        outs = []
        for hd in range(GDN_H):
            j = direction * GDN_H + hd
            gc_col = gcm[:, j:j + 1]
            beta = gall[:, 2 * GDN_H + j:2 * GDN_H + j + 1]
            gc_row = grow[j:j + 1, :]
            gtot = grow[j:j + 1, CHUNK - 1:CHUNK] if direction == 0 else grow[j:j + 1, 0:1]
            q = qa[:, hd * HEAD:(hd + 1) * HEAD]
            k = ka[:, hd * HEAD:(hd + 1) * HEAD]
            v = va[:, hd * HEAD:(hd + 1) * HEAD]
            state = st_s[j]
            decay = jnp.where(incl, jnp.exp(jnp.minimum(gc_col - gc_row, 0.0)), 0.0)
            eg = jnp.exp(gc_col)
            kb = k * beta
            a_mat = jnp.where(strict, _mm_nt(kb, k) * decay, 0.0)
            powers = [a_mat]
            for _ in range(5):
                powers.append(_mm(powers[-1], powers[-1]))
            sol = jnp.concatenate([v * beta, kb * eg], axis=1)
            for pw in reversed(powers[1:]):
                sol = sol + _mm(pw, sol)
            sol = sol - _mm(a_mat, sol)
            u = sol[:, :HEAD]
            w = sol[:, HEAD:]
            qk = _mm_nt(q, k) * decay
            v_new = u - _mm(w, state)
            outs.append(_mm(q * eg, state) + _mm(qk, v_new))
            st_s[j] = state * jnp.exp(gtot) + _mm_tn(k * jnp.exp(gtot - gc_col), v_new)
        o_s[pl.ds(r0, CHUNK), :] = o_s[pl.ds(r0, CHUNK), :] + jnp.concatenate(outs, axis=1)

    def ctx_step(s, carry):
        chunk_update(0, s)
        chunk_update(1, nc - 1 - s)
        return carry

    def lat_step(s, carry):
        chunk_update(0, nc + s)
        chunk_update(1, nch - 1 - s)
        return carry

    lax.fori_loop(0, nc, ctx_step, 0)
    lax.fori_loop(0, nl, lat_step, 0)

    ng = ng_ref[0]

    def rms(s, i):
        return s * lax.rsqrt(_group_mean_square(s) + EPS) * ng[:, i * LANES:(i + 1) * LANES]

    for r0 in range(0, t, rt):
        y = _per_head(rms, o_s[r0:r0 + rt, :]) * _silu(gate_ref[r0:r0 + rt, :])
        out_ref[r0:r0 + rt, :] = y.astype(BF16)


def _gdn(layer, qkv, gate, ab, abt, conv_w, alog_r, dt_r, alog_c, dt_c, ng, dims):
    b, c, s, d, tm, nt = dims
    t = c + s
    n = qkv.shape[0]
    lay = lambda shape: pl.BlockSpec(shape, lambda i: (layer,) + (0,) * (len(shape) - 1))
    return pl.pallas_call(
        functools.partial(_gdn_kernel, c, t, tm),
        grid=(b,),
        in_specs=[pl.BlockSpec((t, P_QKV), lambda i: (i, 0)),
                  pl.BlockSpec((t, P_GATE), lambda i: (i, 0)),
                  pl.BlockSpec((t, P_AB), lambda i: (i, 0)),
                  pl.BlockSpec((16, t), lambda i: (0, i)),
                  lay((1, CONV_K, P_QKV)), lay((1, 1, LANES)), lay((1, 1, LANES)),
                  lay((1, 16, LANES)), lay((1, 16, LANES)), lay((1, 1, GDN_W))],
        out_specs=pl.BlockSpec((t, GDN_W), lambda i: (i, 0)),
        out_shape=jax.ShapeDtypeStruct((n, GDN_W), BF16),
        scratch_shapes=[pltpu.VMEM((t + 2 * PAD_ROWS, P_QKV), F32),
                        pltpu.VMEM((t, GDN_W), F32), pltpu.VMEM((t, GDN_W), F32), pltpu.VMEM((t, GDN_W), F32),
                        pltpu.VMEM((t, LANES), F32), pltpu.VMEM((t, LANES), F32), pltpu.VMEM((t, LANES), F32),
                        pltpu.VMEM((t // CHUNK, 2 * GDN_H, CHUNK), F32),
                        pltpu.VMEM((2 * GDN_H, HEAD, HEAD), F32),
                        pltpu.VMEM((t, GDN_W), F32)],
        compiler_params=_cparams(("arbitrary",)),
        name="gdn",
    )(qkv, gate, ab, abt, conv_w, alog_r, dt_r, alog_c, dt_c, ng)


def _lru_kernel(c, t, rt, p_ref, cw_ref, cb_ref, wr_ref, br_ref, wi_ref, bi_ref, lam_ref, out_ref,
                xp_ref, a_s, b_s, h_s):
    _fill_padded(xp_ref, p_ref[:, :LRU_W], t)
    cw = cw_ref[0]
    cb = cb_ref[0]
    for r0 in range(0, t, rt):
        xr = _conv_rows(xp_ref, cw, r0, rt, c) + cb
        for dirn in range(2):
            r = _sigmoid(_mm(xr, wr_ref[0, dirn]) + br_ref[0, dirn:dirn + 1, :])
            i = _sigmoid(_mm(xr, wi_ref[0, dirn]) + bi_ref[0, dirn:dirn + 1, :])
            log_a = (-LRU_C) * r * _softplus(-lam_ref[0, dirn:dirn + 1, :])
            a = jnp.exp(log_a)
            a_s[dirn, r0:r0 + rt, :] = a
            b_s[dirn, r0:r0 + rt, :] = jnp.sqrt(1.0 - a * a) * (i * xr)

    row = lax.broadcasted_iota(jnp.int32, (SUBLANES, LRU_W), 0)

    def tile_scan(dirn, i, h_prev):
        r0 = pl.multiple_of(i * SUBLANES, SUBLANES)
        a = a_s[dirn, pl.ds(r0, SUBLANES), :]
        bx = b_s[dirn, pl.ds(r0, SUBLANES), :]
        for sh in (1, 2, 4):
            if dirn == 0:
                a_sh, b_sh, m = pltpu.roll(a, sh, 0), pltpu.roll(bx, sh, 0), row >= sh
            else:
                a_sh, b_sh, m = (pltpu.roll(a, SUBLANES - sh, 0), pltpu.roll(bx, SUBLANES - sh, 0),
                                 row < SUBLANES - sh)
            bx = jnp.where(m, a * b_sh + bx, bx)
            a = jnp.where(m, a * a_sh, a)
        h = a * h_prev + bx
        h_s[dirn, pl.ds(r0, SUBLANES), :] = h
        return h[SUBLANES - 1:SUBLANES, :] if dirn == 0 else h[0:1, :]

    n_t = t // SUBLANES
    n_c = c // SUBLANES
    zero = jnp.zeros((1, LRU_W), F32)

    def fwd_step(i, carry):
        hf, hb = carry
        hf = tile_scan(0, i, hf)
        hb = tile_scan(1, jnp.where(i < n_c, n_c - 1 - i, n_t - 1 - (i - n_c)), hb)
        return hf, hb

    lax.fori_loop(0, n_t, fwd_step, (zero, zero))

    for r0 in range(0, t, rt):
        yb = p_ref[r0:r0 + rt, LRU_W:]
        gelu = 0.5 * yb * (1.0 + jnp.tanh(0.7978845608028654 * (yb + 0.044715 * (yb * yb * yb))))
        out_ref[r0:r0 + rt, :] = ((h_s[0, r0:r0 + rt, :] + h_s[1, r0:r0 + rt, :]) * gelu).astype(BF16)


def _lru(layer, p, conv_w, conv_b, wr, br, wi, bi, lam, dims):
    b, c, s, d, tm, nt = dims
    t = c + s
    n = p.shape[0]
    lay = lambda shape: pl.BlockSpec(shape, lambda i: (layer,) + (0,) * (len(shape) - 1))
    return pl.pallas_call(
        functools.partial(_lru_kernel, c, t, tm),
        grid=(b,),
        in_specs=[pl.BlockSpec((t, P_LRU), lambda i: (i, 0)),
                  lay((1, CONV_K, LRU_W)), lay((1, 1, LRU_W)),
                  lay((1, 2, LRU_W, LRU_W)), lay((1, 2, LRU_W)),
                  lay((1, 2, LRU_W, LRU_W)), lay((1, 2, LRU_W)), lay((1, 2, LRU_W))],
        out_specs=pl.BlockSpec((t, LRU_W), lambda i: (i, 0)),
        out_shape=jax.ShapeDtypeStruct((n, LRU_W), BF16),
        scratch_shapes=[pltpu.VMEM((t + 2 * PAD_ROWS, LRU_W), F32),
                        pltpu.VMEM((2, t, LRU_W), F32), pltpu.VMEM((2, t, LRU_W), F32),
                        pltpu.VMEM((2, t, LRU_W), F32)],
        compiler_params=_cparams(("arbitrary",)),
        name="lru",
    )(p, conv_w, conv_b, wr, br, wi, bi, lam)


def _attn_rows(q, k, v):
    outs = []
    for hq in range(ATT_W // HEAD):
        kv = hq // ATT_GROUP
        qh = q[:, hq * HEAD:(hq + 1) * HEAD]
        kh = k[:, kv * HEAD:(kv + 1) * HEAD]
        vh = v[:, kv * HEAD:(kv + 1) * HEAD]
        s = lax.dot_general(qh, kh, (((1,), (1,)), ((), ())), preferred_element_type=F32)
        m = jnp.max(s, axis=-1, keepdims=True)
        p = jnp.exp(s - m)
        l = jnp.sum(p, axis=-1, keepdims=True)
        o = jnp.dot(p.astype(BF16), vh, preferred_element_type=F32)
        outs.append(o * (1.0 / l))
    return jnp.concatenate(outs, axis=1)


def _attn_kernel(c, with_ctx, q_ref, k_ref, v_ref, o_ref):
    def latent():
        o_ref[...] = _attn_rows(q_ref[...], k_ref[...], v_ref[...]).astype(BF16)

    if not with_ctx:
        latent()
        return
    j = pl.program_id(1)

    @pl.when(j == 0)
    def _():
        o_ref[...] = _attn_rows(q_ref[...], k_ref[0:c, :], v_ref[0:c, :]).astype(BF16)

    pl.when(j > 0)(latent)


def _attn(q, k, v, with_ctx, dims):
    b, c, s, d, tm, nt = dims
    t = c + s
    n = q.shape[0]
    off = 0 if with_ctx else 1
    return pl.pallas_call(
        functools.partial(_attn_kernel, c, with_ctx),
        grid=(b, nt - off),
        in_specs=[pl.BlockSpec((tm, ATT_W), lambda i, j: (i * nt + j + off, 0)),
                  pl.BlockSpec((t, ATT_KV_W), lambda i, j: (i, 0)),
                  pl.BlockSpec((t, ATT_KV_W), lambda i, j: (i, 0))],
        out_specs=pl.BlockSpec((tm, ATT_W), lambda i, j: (i * (nt - off) + j, 0)),
        out_shape=jax.ShapeDtypeStruct((b * (nt - off) * tm, ATT_W), BF16),
        compiler_params=_cparams(("arbitrary", "arbitrary")),
        name="attn",
    )(q, k, v)


R_E1, R_E2, R_G1, R_G2, R_RANK1, R_RANK2 = range(6)


def _outproj_kernel(tm, a_ref, b_ref, c_ref, x_ref, g1_ref, sh_ref, sc_ref, ng_ref, wo_ref, wr_ref, br_ref,
                    xo_ref, h_ref, route_ref, cnt_ref, carry_s):
    i = pl.program_id(0)

    @pl.when(i == 0)
    def _():
        carry_s[...] = jnp.zeros(carry_s.shape, F32)

    mix = (jnp.dot(a_ref[...], wo_ref[0, 0:GDN_W, :], preferred_element_type=F32)
           + jnp.dot(b_ref[...], wo_ref[0, GDN_W:GDN_W + LRU_W, :], preferred_element_type=F32)
           + jnp.dot(c_ref[...], wo_ref[0, GDN_W + LRU_W:, :], preferred_element_type=F32))
    x = x_ref[...] + g1_ref[0] * mix
    xo_ref[...] = x
    ms = jnp.mean(x * x, axis=-1, keepdims=True)
    h = (x * lax.rsqrt(ms + EPS) * ng_ref[0]) * (1.0 + sc_ref[0]) + sh_ref[0]
    h_ref[...] = h

    logits = _mm_exact(h, wr_ref[0]) + br_ref[0]
    lane_i = lax.broadcasted_iota(jnp.int32, logits.shape, 1)
    lane = lane_i.astype(F32)
    lane_group = jnp.right_shift(lane_i, 3).astype(F32)
    big = jnp.float32(4 * LANES)
    neg = jnp.float32(-jnp.inf)
    is_group = (lane_i >= N_EXPERTS) & (lane_i < N_EXPERTS + N_GROUPS)
    gl = jnp.where(is_group, logits, neg)
    gm = jnp.max(gl, axis=-1, keepdims=True)
    pg_top = 1.0 / jnp.sum(jnp.where(is_group, jnp.exp(gl - gm), 0.0), axis=-1, keepdims=True)
    g_idx = jnp.min(jnp.where(gl == gm, lane, big), axis=-1, keepdims=True) - N_EXPERTS
    in_group = (lane_i < N_EXPERTS) & (lane_group == g_idx)
    le = jnp.where(in_group, logits, neg)
    m1 = jnp.max(le, axis=-1, keepdims=True)
    e1 = jnp.min(jnp.where(le == m1, lane, big), axis=-1, keepdims=True)
    le2 = jnp.where(lane == e1, neg, le)
    m2 = jnp.max(le2, axis=-1, keepdims=True)
    e2 = jnp.min(jnp.where(le2 == m2, lane, big), axis=-1, keepdims=True)
    z = jnp.sum(jnp.where(in_group, jnp.exp(le - m1), 0.0), axis=-1, keepdims=True)
    pe1 = 1.0 / z
    pe2 = jnp.exp(m2 - m1) / z
    gate1 = pg_top * pe1 / (pe1 + pe2)
    gate2 = pg_top * pe2 / (pe1 + pe2)

    sel1 = lane == e1
    sel2 = lane == e2
    onehot = jnp.where(sel1 | sel2, 1.0, 0.0)
    ri = lax.broadcasted_iota(jnp.int32, (tm, tm), 0)
    ci = lax.broadcasted_iota(jnp.int32, (tm, tm), 1)
    before = _mm(jnp.where(ri > ci, 1.0, 0.0), onehot) + carry_s[0:1, :]
    rank1 = jnp.sum(jnp.where(sel1, before, 0.0), axis=-1, keepdims=True)
    rank2 = jnp.sum(jnp.where(sel2, before, 0.0), axis=-1, keepdims=True)
    carry = carry_s[0:1, :] + jnp.sum(onehot, axis=0, keepdims=True)
    carry_s[...] = jnp.broadcast_to(carry, carry_s.shape)
    cnt_ref[...] = jnp.broadcast_to(carry, cnt_ref.shape)

    route = jnp.zeros(logits.shape, F32)
    for col, val in ((R_E1, e1), (R_E2, e2), (R_G1, gate1), (R_G2, gate2),
                     (R_RANK1, rank1), (R_RANK2, rank2)):
        route = jnp.where(lane_i == col, val, route)
    route_ref[...] = route


def _tile_map(with_ctx, nt):
    if with_ctx:
        return lambda i: i
    return lambda i: (i // (nt - 1)) * nt + 1 + i % (nt - 1)


def _outproj(layer, a, bm, cm, x, mod, norm2_g, w_out, w_route, b_route, with_ctx, dims):
    b, c, s, d, tm, nt = dims
    n = x.shape[0]
    n_layers = norm2_g.shape[0]
    rows = mod.shape[0] // (6 * n_layers)
    tile = _tile_map(with_ctx, nt)
    n_tiles = b * (nt if with_ctx else nt - 1)

    def mod_idx(chunk):
        def f(i):
            ti = tile(i)
            r = jnp.where(ti % nt == 0, b, ti // nt)
            return ((layer * 6 + chunk) * rows + r, 0, 0)
        return f

    row_spec = lambda w: pl.BlockSpec((tm, w), lambda i: (tile(i), 0))
    compact = lambda w: pl.BlockSpec((tm, w), lambda i: (i, 0))
    lay = lambda shape: pl.BlockSpec(shape, lambda i: (layer,) + (0,) * (len(shape) - 1))
    n_moe = n_tiles * tm
    return pl.pallas_call(
        functools.partial(_outproj_kernel, tm),
        grid=(n_tiles,),
        in_specs=[row_spec(GDN_W), row_spec(LRU_W), compact(ATT_W), row_spec(d),
                  pl.BlockSpec((1, 1, d), mod_idx(2)), pl.BlockSpec((1, 1, d), mod_idx(3)),
                  pl.BlockSpec((1, 1, d), mod_idx(4)),
                  lay((1, 1, d)), lay((1, d, d)), lay((1, d, LANES)), lay((1, 1, LANES))],
        out_specs=[row_spec(d), compact(d), compact(LANES),
                   pl.BlockSpec((SUBLANES, LANES), lambda i: (0, 0))],
        out_shape=[jax.ShapeDtypeStruct((n, d), F32), jax.ShapeDtypeStruct((n_moe, d), F32),
                   jax.ShapeDtypeStruct((n_moe, LANES), F32), jax.ShapeDtypeStruct((SUBLANES, LANES), F32)],
        scratch_shapes=[pltpu.VMEM((SUBLANES, LANES), F32)],
        input_output_aliases={3: 0},
        compiler_params=_cparams(("arbitrary",)),
        name="outproj",
    )(a, bm, cm, x, mod, mod, mod, norm2_g, w_out, w_route, b_route)


def _dest_kernel(route_ref, cnt_ref, dest_ref):
    route = route_ref[...]
    lane = lax.broadcasted_iota(jnp.int32, route.shape, 1)
    counts = cnt_ref[...]
    padded = jnp.floor((counts + (MOE_BLOCK - 1)) * (1.0 / MOE_BLOCK)) * MOE_BLOCK
    li = lax.broadcasted_iota(jnp.int32, (LANES, LANES), 0)
    lj = lax.broadcasted_iota(jnp.int32, (LANES, LANES), 1)
    start = _mm_exact(padded, jnp.where((li < lj) & (li < N_EXPERTS), 1.0, 0.0))[0:1, :]

    def col(j):
        return jnp.sum(jnp.where(lane == j, route, 0.0), axis=-1, keepdims=True)

    def slot(e, rank):
        return jnp.sum(jnp.where(lane == e.astype(jnp.int32), start, 0.0), axis=-1, keepdims=True) + rank

    d1 = slot(col(R_E1), col(R_RANK1))
    d2 = slot(col(R_E2), col(R_RANK2))
    dest_ref[...] = jnp.where(lane == 0, d1, jnp.where(lane == 1, d2, 0.0)).astype(jnp.int32)


def _dest(route, counts, dims):
    tm = dims[4]
    n = route.shape[0]
    return pl.pallas_call(
        _dest_kernel,
        grid=(n // tm,),
        in_specs=[pl.BlockSpec((tm, LANES), lambda i: (i, 0)),
                  pl.BlockSpec((SUBLANES, LANES), lambda i: (0, 0))],
        out_specs=pl.BlockSpec((tm, LANES), lambda i: (i, 0)),
        out_shape=jax.ShapeDtypeStruct((n, LANES), jnp.int32),
        compiler_params=_cparams(("arbitrary",)),
        name="dest",
    )(route, counts)


def _dispatch_kernel(tm, d1_ref, d2_ref, h_ref, xb_in_ref, xb_ref, sem):
    del xb_in_ref
    base = pl.program_id(0) * tm

    def issue(r, carry):
        src = h_ref.at[pl.ds(r, 1)]
        pltpu.make_async_copy(src, xb_ref.at[pl.ds(d1_ref[base + r], 1)], sem).start()
        pltpu.make_async_copy(src, xb_ref.at[pl.ds(d2_ref[base + r], 1)], sem).start()
        return carry

    lax.fori_loop(0, tm, issue, 0)
    for _ in range(2):
        pltpu.make_async_copy(h_ref, xb_ref.at[pl.ds(0, tm)], sem).wait()


def _dispatch(d1, d2, h, n_slots, dims):
    tm = dims[4]
    d = h.shape[1]
    return pl.pallas_call(
        functools.partial(_dispatch_kernel, tm),
        grid_spec=pltpu.PrefetchScalarGridSpec(
            num_scalar_prefetch=2,
            grid=(h.shape[0] // tm,),
            in_specs=[pl.BlockSpec((tm, d), lambda i, d1, d2: (i, 0)),
                      pl.BlockSpec(memory_space=pl.ANY)],
            out_specs=pl.BlockSpec(memory_space=pl.ANY),
            scratch_shapes=[pltpu.SemaphoreType.DMA(())]),
        out_shape=jax.ShapeDtypeStruct((n_slots, d), F32),
        input_output_aliases={3: 0},
        compiler_params=_cparams(("arbitrary",)),
        name="dispatch",
    )(d1, d2, h, jnp.zeros((n_slots, d), F32))


def _expert_kernel(be_ref, nact_ref, xb_ref, w1_ref, w3_ref, w2_ref, yb_ref, w1_s, w3_s, w2_s):
    i = pl.program_id(0)
    changed = (i == 0) | (be_ref[i] != be_ref[jnp.maximum(i - 1, 0)])

    @pl.when(changed)
    def _():
        w1_s[...] = w1_ref[0, 0].astype(BF16)
        w3_s[...] = w3_ref[0, 0].astype(BF16)
        w2_s[...] = w2_ref[0, 0].astype(BF16)

    @pl.when(i < nact_ref[0])
    def _():
        x = xb_ref[...].astype(BF16)
        h1 = jnp.dot(x, w1_s[...], preferred_element_type=F32)
        h3 = jnp.dot(x, w3_s[...], preferred_element_type=F32)
        act = (_silu(h1) * h3).astype(BF16)
        yb_ref[...] = jnp.dot(act, w2_s[...], preferred_element_type=F32)

    @pl.when(i >= nact_ref[0])
    def _():
        yb_ref[...] = jnp.zeros(yb_ref.shape, F32)


def _experts(layer, block_e, nact, xb, w1, w3, w2):
    n_slots, d = xb.shape
    hid = w1.shape[-1]
    return pl.pallas_call(
        _expert_kernel,
        grid_spec=pltpu.PrefetchScalarGridSpec(
            num_scalar_prefetch=2,
            grid=(n_slots // MOE_BLOCK,),
            in_specs=[pl.BlockSpec((MOE_BLOCK, d), lambda i, be, na: (i, 0)),
                      pl.BlockSpec((1, 1, d, hid), lambda i, be, na: (layer, be[i], 0, 0)),
                      pl.BlockSpec((1, 1, d, hid), lambda i, be, na: (layer, be[i], 0, 0)),
                      pl.BlockSpec((1, 1, hid, d), lambda i, be, na: (layer, be[i], 0, 0))],
            out_specs=pl.BlockSpec((MOE_BLOCK, d), lambda i, be, na: (i, 0)),
            scratch_shapes=[pltpu.VMEM((d, hid), BF16), pltpu.VMEM((d, hid), BF16),
                            pltpu.VMEM((hid, d), BF16)]),
        out_shape=jax.ShapeDtypeStruct((n_slots, d), F32),
        compiler_params=_cparams(("arbitrary",)),
        name="experts",
    )(block_e, nact, xb, w1, w3, w2)


def _combine_kernel(tm, final, d1_ref, d2_ref, x_ref, route_ref, g2_ref, fg_ref, yb_ref, o_ref,
                    y1_s, y2_s, sem):
    base = pl.program_id(0) * tm

    def issue(r, carry):
        pltpu.make_async_copy(yb_ref.at[pl.ds(d1_ref[base + r], 1)], y1_s.at[pl.ds(r, 1)], sem).start()
        pltpu.make_async_copy(yb_ref.at[pl.ds(d2_ref[base + r], 1)], y2_s.at[pl.ds(r, 1)], sem).start()
        return carry

    lax.fori_loop(0, tm, issue, 0)
    pltpu.make_async_copy(yb_ref.at[pl.ds(0, tm)], y1_s, sem).wait()
    pltpu.make_async_copy(yb_ref.at[pl.ds(0, tm)], y2_s, sem).wait()

    route = route_ref[...]
    lane = lax.broadcasted_iota(jnp.int32, route.shape, 1)
    gate1 = jnp.sum(jnp.where(lane == R_G1, route, 0.0), axis=-1, keepdims=True)
    gate2 = jnp.sum(jnp.where(lane == R_G2, route, 0.0), axis=-1, keepdims=True)
    x = x_ref[...] + g2_ref[0] * (y1_s[...] * gate1 + y2_s[...] * gate2)
    if final:
        ms = jnp.mean(x * x, axis=-1, keepdims=True)
        x = x * lax.rsqrt(ms + EPS) * fg_ref[...]
    o_ref[...] = x


def _combine(layer, d1, d2, x, route, mod, final_g, yb, with_ctx, final, n_layers, dims):
    b, c, s, d, tm, nt = dims
    n = x.shape[0]
    rows = mod.shape[0] // (6 * n_layers)
    tile = _tile_map(with_ctx, nt)
    n_tiles = b * (nt if with_ctx else nt - 1)

    def mod_idx(i, d1, d2):
        ti = tile(i)
        r = jnp.where(ti % nt == 0, b, ti // nt)
        return ((layer * 6 + 5) * rows + r, 0, 0)

    if final:
        out_spec = pl.BlockSpec((tm, d), lambda i, d1, d2: (i, 0))
        out_shape = jax.ShapeDtypeStruct((n_tiles * tm, d), F32)
        aliases = {}
    else:
        out_spec = pl.BlockSpec((tm, d), lambda i, d1, d2: (tile(i), 0))
        out_shape = jax.ShapeDtypeStruct((n, d), F32)
        aliases = {2: 0}
    return pl.pallas_call(
        functools.partial(_combine_kernel, tm, final),
        grid_spec=pltpu.PrefetchScalarGridSpec(
            num_scalar_prefetch=2,
            grid=(n_tiles,),
            in_specs=[pl.BlockSpec((tm, d), lambda i, d1, d2: (tile(i), 0)),
                      pl.BlockSpec((tm, LANES), lambda i, d1, d2: (i, 0)),
                      pl.BlockSpec((1, 1, d), mod_idx),
                      pl.BlockSpec((1, d), lambda i, d1, d2: (0, 0)),
                      pl.BlockSpec(memory_space=pl.ANY)],
            out_specs=out_spec,
            scratch_shapes=[pltpu.VMEM((tm, d), F32), pltpu.VMEM((tm, d), F32),
                            pltpu.SemaphoreType.DMA(())]),
        out_shape=out_shape,
        input_output_aliases=aliases,
        compiler_params=_cparams(("arbitrary",)),
        name="combine",
    )(d1, d2, x, route, mod, final_g, yb)


def _rope_tables(s, tm):
    rows = s // GRID_W
    row = jnp.repeat(jnp.arange(rows, dtype=F32), GRID_W)
    col = jnp.tile(jnp.arange(GRID_W, dtype=F32), rows)
    axis_dim = HEAD // 2
    inv_freq = ROPE_THETA ** (-jnp.arange(0, axis_dim, 2, dtype=F32) / axis_dim)
    ar = row[:, None] * inv_freq
    ac = col[:, None] * inv_freq
    cos = jnp.concatenate([jnp.cos(ar), jnp.cos(ar), jnp.cos(ac), jnp.cos(ac)], axis=1)
    sin = jnp.concatenate([-jnp.sin(ar), jnp.sin(ar), -jnp.sin(ac), jnp.sin(ac)], axis=1)
    cos = jnp.concatenate([jnp.ones((tm, HEAD), F32), cos], axis=0)
    sin = jnp.concatenate([jnp.zeros((tm, HEAD), F32), sin], axis=0)
    return jnp.tile(cos, (1, 2)), jnp.tile(sin, (1, 2))


def _block_diag(w):
    n_layers = w.shape[0]
    eye = jnp.eye(LRU_BLOCKS, dtype=w.dtype)
    full = jnp.einsum('ldnij,nm->ldnimj', w, eye)
    return full.reshape(n_layers, 2, LRU_W, LRU_W)


def _pad_lanes(a, width=LANES):
    return jnp.pad(a, [(0, 0)] * (a.ndim - 1) + [(0, width - a.shape[-1])])


def kernel(x, c, ctx, c_ctx, w_ada, b_ada, norm1_g, norm2_g, w_in, w_out, gdn_conv_w, gdn_a_log, gdn_dt_bias, gdn_norm_g, lru_conv_w, lru_conv_b, lru_w_r, lru_b_r, lru_w_i, lru_b_i, lru_lambda, attn_q_norm_g, attn_k_norm_g, moe_w_group, moe_b_group, moe_w_expert, moe_b_expert, moe_w1, moe_w3, moe_w2, final_norm_g):
    bsz, s, d = x.shape
    cl = ctx.shape[1]
    n_layers = w_in.shape[0]
    tm = cl
    assert s % tm == 0 and tm % CHUNK == 0 and s % GRID_W == 0
    nt = (cl + s) // tm
    dims = (bsz, cl, s, d, tm, nt)

    o1 = 4 * GDN_W + 4 * GDN_H
    o2 = o1 + 2 * LRU_W
    w_ab = w_in[:, :, 4 * GDN_W:o1]
    w_pack = jnp.concatenate([w_in[:, :, :4 * GDN_W], _pad_lanes(w_ab), w_in[:, :, o1:o2], w_in[:, :, o2:]],
                             axis=-1).astype(BF16)
    wabt = jnp.swapaxes(w_ab, 1, 2).astype(BF16)
    gq = jnp.tile(attn_q_norm_g, (1, 2))[:, None, :]
    gk = jnp.tile(attn_k_norm_g, (1, 2))[:, None, :]
    cos_t, sin_t = _rope_tables(s, tm)
    alog = gdn_a_log.reshape(n_layers, 2 * GDN_H)
    dtb = gdn_dt_bias.reshape(n_layers, 2 * GDN_H)
    alog_r = _pad_lanes(alog)[:, None, :]
    dt_r = _pad_lanes(dtb)[:, None, :]
    alog_c = jnp.broadcast_to(_pad_lanes(alog, 16)[:, :, None], (n_layers, 16, LANES))
    dt_c = jnp.broadcast_to(_pad_lanes(dtb, 16)[:, :, None], (n_layers, 16, LANES))
    gdn_ng = jnp.tile(gdn_norm_g, (1, GDN_H))[:, None, :]
    wr_bd = _block_diag(lru_w_r).astype(BF16)
    wi_bd = _block_diag(lru_w_i).astype(BF16)
    w_out_b = w_out.astype(BF16)
    w_route = _pad_lanes(jnp.concatenate([moe_w_expert, moe_w_group], axis=-1))
    b_route = _pad_lanes(jnp.concatenate([moe_b_expert, moe_b_group], axis=-1))[:, None, :]

    cv = jnp.concatenate([c, c_ctx[None, :]], axis=0)
    rows = -(-cv.shape[0] // SUBLANES) * SUBLANES
    cv = jnp.pad(cv, ((0, rows - cv.shape[0]), (0, 0)))
    mod = _adaln(cv, w_ada, b_ada).reshape(n_layers * 6 * rows, 1, d)

    xf = jnp.concatenate([ctx, x], axis=1).reshape(bsz * (cl + s), d)
    out = None
    for layer in range(n_layers):
        with_ctx = layer < n_layers - 1
        qkv, gate, ab, abt, plru, q, k, v = _inproj(
            layer, xf, mod, norm1_g[:, None, :], w_pack, wabt, gq, gk, cos_t, sin_t, dims)
        mix_a = _gdn(layer, qkv, gate, ab, abt, gdn_conv_w, alog_r, dt_r, alog_c, dt_c, gdn_ng, dims)
        mix_b = _lru(layer, plru, lru_conv_w, lru_conv_b[:, None, :], wr_bd, lru_b_r, wi_bd, lru_b_i,
                     lru_lambda, dims)
        mix_c = _attn(q, k, v, with_ctx, dims)
        xf, h, route, counts = _outproj(layer, mix_a, mix_b, mix_c, xf, mod, norm2_g[:, None, :], w_out_b,
                                        w_route, b_route, with_ctx, dims)
        dest = _dest(route, counts, dims)
        d1 = dest[:, 0]
        d2 = dest[:, 1]
        n_tok = bsz * ((cl + s) if with_ctx else s)
        n_blocks = -(-(2 * n_tok) // MOE_BLOCK) + N_EXPERTS
        cnt = counts[0, :N_EXPERTS].astype(jnp.int32)
        pad_end = jnp.cumsum((cnt + MOE_BLOCK - 1) // MOE_BLOCK * MOE_BLOCK)
        block_row = jnp.arange(n_blocks, dtype=jnp.int32)[:, None] * MOE_BLOCK
        block_e = jnp.minimum(jnp.sum((pad_end[None, :] <= block_row).astype(jnp.int32), axis=1),
                              N_EXPERTS - 1)
        nact = (pad_end[-1:] // MOE_BLOCK).astype(jnp.int32)
        xb = _dispatch(d1, d2, h, n_blocks * MOE_BLOCK, dims)
        yb = _experts(layer, block_e, nact, xb, moe_w1, moe_w3, moe_w2)
        res = _combine(layer, d1, d2, xf, route, mod, final_norm_g[None, :], yb, with_ctx,
                       not with_ctx, n_layers, dims)
        if with_ctx:
            xf = res
        else:
            out = res
    return out.reshape(bsz, s, d)
```

```python
import functools

import jax
import jax.numpy as jnp
from jax import lax
from jax.experimental import pallas as pl
from jax.experimental.pallas import tpu as pltpu

F32 = jnp.float32
BF16 = jnp.bfloat16
HIGHEST = lax.Precision.HIGHEST

HEAD = 64
GDN_W = 256
GDN_H = GDN_W // HEAD
CHUNK = 64
LRU_W = 256
LRU_BLOCKS = 4
LRU_C = 8.0
ATT_W = 512
ATT_KV_W = 128
ATT_GROUP = 4
CONV_K = 4
N_GROUPS = 8
N_EXPERTS = 64
EXPERTS_PER_GROUP = 8
MOE_HIDDEN = 512
MOE_BLOCK = 128
GRID_W = 64
ROPE_THETA = 10000.0
EPS = 1e-6
LANES = 128
SUBLANES = 8
PAD_ROWS = 8
VMEM_LIMIT = 56 * 1024 * 1024

P_QKV = 3 * GDN_W
P_GATE = GDN_W
P_AB = LANES
P_LRU = 2 * LRU_W
P_ATT = ATT_W + 2 * ATT_KV_W
P_ALL = P_QKV + P_GATE + P_AB + P_LRU + P_ATT


def _cparams(sem):
    return pltpu.CompilerParams(dimension_semantics=sem, vmem_limit_bytes=VMEM_LIMIT)


def _sigmoid(x):
    return 1.0 / (1.0 + jnp.exp(-x))


def _silu(x):
    return x * _sigmoid(x)


def _softplus(x):
    return jnp.maximum(x, 0.0) + jnp.log(1.0 + jnp.exp(-jnp.abs(x)))


def _mm(a, b):
    return jnp.dot(a.astype(BF16), b.astype(BF16), preferred_element_type=F32)


def _mm_nt(a, b):
    return lax.dot_general(a.astype(BF16), b.astype(BF16), (((1,), (1,)), ((), ())),
                           preferred_element_type=F32)


def _mm_tn(a, b):
    return lax.dot_general(a.astype(BF16), b.astype(BF16), (((0,), (0,)), ((), ())),
                           preferred_element_type=F32)


def _mm_exact(a, b):
    return jnp.dot(a, b, precision=HIGHEST, preferred_element_type=F32)


def _group_mean_square(x):
    lane = lax.broadcasted_iota(jnp.int32, x.shape, 1)
    lo = lane < HEAD
    x2 = x * x
    s_lo = jnp.sum(jnp.where(lo, x2, 0.0), axis=-1, keepdims=True)
    s_hi = jnp.sum(jnp.where(lo, 0.0, x2), axis=-1, keepdims=True)
    return jnp.where(lo, s_lo, s_hi) * (1.0 / HEAD)


def _per_head(fn, x):
    n = x.shape[1] // LANES
    return jnp.concatenate([fn(x[:, i * LANES:(i + 1) * LANES], i) for i in range(n)], axis=1)


def _adaln_kernel(cv_ref, w_ref, b_ref, o_ref):
    o_ref[0] = _mm_exact(_silu(cv_ref[...]), w_ref[0]) + b_ref[0]


def _adaln(cv, w_ada, b_ada):
    n_layers, d, _ = w_ada.shape
    rows = cv.shape[0]
    return pl.pallas_call(
        _adaln_kernel,
        grid=(n_layers, 6),
        in_specs=[pl.BlockSpec((rows, d), lambda l, j: (0, 0)),
                  pl.BlockSpec((1, d, d), lambda l, j: (l, 0, j)),
                  pl.BlockSpec((1, 1, d), lambda l, j: (l * 6 + j, 0, 0))],
        out_specs=pl.BlockSpec((1, rows, d), lambda l, j: (l * 6 + j, 0, 0)),
        out_shape=jax.ShapeDtypeStruct((n_layers * 6, rows, d), F32),
        compiler_params=_cparams(("arbitrary", "arbitrary")),
        name="adaln",
    )(cv, w_ada, b_ada.reshape(n_layers * 6, 1, d))


def _inproj_kernel(x_ref, sh_ref, sc_ref, g_ref, w_ref, wabt_ref, gq_ref, gk_ref, cos_ref, sin_ref,
                   qkv_ref, gate_ref, ab_ref, abt_ref, lru_ref, q_ref, k_ref, v_ref):
    x = x_ref[...]
    ms = jnp.mean(x * x, axis=-1, keepdims=True)
    h = (x * lax.rsqrt(ms + EPS) * g_ref[0]) * (1.0 + sc_ref[0]) + sh_ref[0]
    hb = h.astype(BF16)
    p = jnp.dot(hb, w_ref[0], preferred_element_type=F32)
    o = 0
    qkv_ref[...] = p[:, o:o + P_QKV]
    o += P_QKV
    gate_ref[...] = p[:, o:o + P_GATE]
    o += P_GATE
    ab_ref[...] = p[:, o:o + P_AB]
    o += P_AB
    lru_ref[...] = p[:, o:o + P_LRU]
    o += P_LRU
    att = p[:, o:o + P_ATT]
    abt_ref[...] = lax.dot_general(wabt_ref[0], hb, (((1,), (1,)), ((), ())), preferred_element_type=F32)

    cos = cos_ref[...]
    sin = sin_ref[...]
    lane = lax.broadcasted_iota(jnp.int32, cos.shape, 1)
    first_half = (lane & 16) == 0

    def norm_rope(gain):
        def fn(s, _):
            y = s * lax.rsqrt(_group_mean_square(s) + EPS) * gain
            swapped = jnp.where(first_half, pltpu.roll(y, LANES - 16, 1), pltpu.roll(y, 16, 1))
            return y * cos + swapped * sin
        return fn

    q = _per_head(norm_rope(gq_ref[0]), att[:, :ATT_W])
    k = _per_head(norm_rope(gk_ref[0]), att[:, ATT_W:ATT_W + ATT_KV_W])
    q_ref[...] = (q * (HEAD ** -0.5)).astype(BF16)
    k_ref[...] = k.astype(BF16)
    v_ref[...] = att[:, ATT_W + ATT_KV_W:].astype(BF16)


def _inproj(layer, x, mod, norm1_g, w_pack, wabt, gq, gk, cos_t, sin_t, dims):
    b, c, s, d, tm, nt = dims
    n = x.shape[0]
    rows = mod.shape[0] // (6 * norm1_g.shape[0])

    def mod_idx(chunk):
        def f(i):
            r = jnp.where(i % nt == 0, b, i // nt)
            return ((layer * 6 + chunk) * rows + r, 0, 0)
        return f

    row_spec = lambda w: pl.BlockSpec((tm, w), lambda i: (i, 0))
    return pl.pallas_call(
        _inproj_kernel,
        grid=(n // tm,),
        in_specs=[row_spec(d),
                  pl.BlockSpec((1, 1, d), mod_idx(0)),
                  pl.BlockSpec((1, 1, d), mod_idx(1)),
                  pl.BlockSpec((1, 1, d), lambda i: (layer, 0, 0)),
                  pl.BlockSpec((1, d, P_ALL), lambda i: (layer, 0, 0)),
                  pl.BlockSpec((1, 16, d), lambda i: (layer, 0, 0)),
                  pl.BlockSpec((1, 1, LANES), lambda i: (layer, 0, 0)),
                  pl.BlockSpec((1, 1, LANES), lambda i: (layer, 0, 0)),
                  pl.BlockSpec((tm, LANES), lambda i: (i % nt, 0)),
                  pl.BlockSpec((tm, LANES), lambda i: (i % nt, 0))],
        out_specs=[row_spec(P_QKV), row_spec(P_GATE), row_spec(P_AB),
                   pl.BlockSpec((16, tm), lambda i: (0, i)),
                   row_spec(P_LRU), row_spec(ATT_W), row_spec(ATT_KV_W), row_spec(ATT_KV_W)],
        out_shape=[jax.ShapeDtypeStruct((n, P_QKV), F32), jax.ShapeDtypeStruct((n, P_GATE), F32),
                   jax.ShapeDtypeStruct((n, P_AB), F32), jax.ShapeDtypeStruct((16, n), F32),
                   jax.ShapeDtypeStruct((n, P_LRU), F32), jax.ShapeDtypeStruct((n, ATT_W), BF16),
                   jax.ShapeDtypeStruct((n, ATT_KV_W), BF16), jax.ShapeDtypeStruct((n, ATT_KV_W), BF16)],
        compiler_params=_cparams(("arbitrary",)),
        name="inproj",
    )(x, mod, mod, norm1_g, w_pack, wabt, gq, gk, cos_t, sin_t)


def _conv_rows(xp_ref, w, r0, rt, c):
    base = r0 + PAD_ROWS
    xm1 = xp_ref[base - 1:base - 1 + rt, :]
    x0 = xp_ref[base:base + rt, :]
    xp1 = xp_ref[base + 1:base + 1 + rt, :]
    xp2 = xp_ref[base + 2:base + 2 + rt, :]
    row = r0 + lax.broadcasted_iota(jnp.int32, (rt, 1), 0)
    xm1 = jnp.where(row == c, 0.0, xm1)
    xp1 = jnp.where(row == c - 1, 0.0, xp1)
    xp2 = jnp.where((row == c - 1) | (row == c - 2), 0.0, xp2)
    return w[0:1, :] * xm1 + w[1:2, :] * x0 + w[2:3, :] * xp1 + w[3:4, :] * xp2


def _fill_padded(xp_ref, x, t):
    zeros = jnp.zeros((PAD_ROWS, xp_ref.shape[1]), F32)
    xp_ref[0:PAD_ROWS, :] = zeros
    xp_ref[PAD_ROWS + t:2 * PAD_ROWS + t, :] = zeros
    xp_ref[PAD_ROWS:PAD_ROWS + t, :] = x


def _gdn_kernel(c, t, rt, qkv_ref, gate_ref, ab_ref, abt_ref, cw_ref, alog_r_ref, dt_r_ref, alog_c_ref,
                dt_c_ref, ng_ref, out_ref,
                xp_ref, q_s, k_s, v_s, g_s, gcf_s, gcb_s, grp_s, u_s, w_s, qk_s, qg_s, kt_s, eg_s, st_s, o_s):
    nch = t // CHUNK
    nc = c // CHUNK
    nl = nch - nc

    _fill_padded(xp_ref, qkv_ref[...], t)
    cw = cw_ref[0]

    def l2n(s, _):
        return s * lax.rsqrt(_group_mean_square(s) * HEAD + EPS)

    for r0 in range(0, t, rt):
        y = _silu(_conv_rows(xp_ref, cw, r0, rt, c))
        q_s[r0:r0 + rt, :] = _per_head(l2n, y[:, :GDN_W]) * (HEAD ** -0.5)
        k_s[r0:r0 + rt, :] = _per_head(l2n, y[:, GDN_W:2 * GDN_W])
        v_s[r0:r0 + rt, :] = y[:, 2 * GDN_W:]

    ab = ab_ref[...]
    lane = lax.broadcasted_iota(jnp.int32, ab.shape, 1)
    gval = -jnp.exp(alog_r_ref[0]) * _softplus(ab + dt_r_ref[0])
    g_s[...] = jnp.where(lane < 2 * GDN_H, gval, _sigmoid(ab))
    abt = abt_ref[...]
    g_t = -jnp.exp(alog_c_ref[0][:, 0:1]) * _softplus(abt + dt_c_ref[0][:, 0:1])

    ri = lax.broadcasted_iota(jnp.int32, (CHUNK, CHUNK), 0)
    ci = lax.broadcasted_iota(jnp.int32, (CHUNK, CHUNK), 1)
    lower_f = (ri >= ci).astype(F32)
    upper_f = (ri <= ci).astype(F32)
    row8 = lax.broadcasted_iota(jnp.int32, (2 * GDN_H, CHUNK), 0)
    for ch in range(nch):
        r0 = ch * CHUNK
        gch = g_s[r0:r0 + CHUNK, :]
        gcf_s[r0:r0 + CHUNK, :] = _mm_exact(lower_f, gch)
        gcb_s[r0:r0 + CHUNK, :] = _mm_exact(upper_f, gch)
        gtc = g_t[0:2 * GDN_H, r0:r0 + CHUNK]
        gr = jnp.where(row8 < GDN_H, _mm_exact(gtc, upper_f), _mm_exact(gtc, lower_f))
        for direction in range(2):
            rows = [gr[direction * GDN_H + hd:direction * GDN_H + hd + 1, :] for hd in range(GDN_H)]
            edge = CHUNK - 1 if direction == 0 else 0
            tots = [jnp.broadcast_to(r[:, edge:edge + 1], (1, CHUNK)) for r in rows]
            grp_s[ch, direction:direction + 1, :] = jnp.concatenate(rows, axis=1)
            grp_s[ch, 2 + direction:3 + direction, :] = jnp.concatenate(tots, axis=1)

    st_s[...] = jnp.zeros(st_s.shape, F32)
    o_s[...] = jnp.zeros(o_s.shape, F32)

    ri4 = lax.broadcasted_iota(jnp.int32, (CHUNK, GDN_W), 0)
    li4 = lax.broadcasted_iota(jnp.int32, (CHUNK, GDN_W), 1)
    pos4 = li4 & (CHUNK - 1)
    head_masks = [jnp.where((li4 >= hd * HEAD) & (li4 < (hd + 1) * HEAD), 1.0, 0.0).astype(BF16)
                  for hd in range(GDN_H)]
    lane_lo = lax.broadcasted_iota(jnp.int32, (CHUNK, LANES), 1) < HEAD
    bi = jnp.right_shift(lax.broadcasted_iota(jnp.int32, (GDN_W, GDN_W), 0), 6)
    bj = jnp.right_shift(lax.broadcasted_iota(jnp.int32, (GDN_W, GDN_W), 1), 6)
    same_head = bi == bj

    def block_diag(xb):
        return jnp.concatenate([xb * m for m in head_masks], axis=0)

    def expand4(x, base):
        cols = [jnp.broadcast_to(x[:, base + hd:base + hd + 1], (CHUNK, LANES)) for hd in range(GDN_H)]
        return jnp.concatenate([jnp.where(lane_lo, cols[0], cols[1]), jnp.where(lane_lo, cols[2], cols[3])],
                               axis=1)

    def solve_units(units):
        work = []
        for direction, ch in units:
            r0 = pl.multiple_of(ch * CHUNK, CHUNK)
            unit = direction * nch + ch
            q = q_s[pl.ds(r0, CHUNK), :]
            k = k_s[pl.ds(r0, CHUNK), :]
            v = v_s[pl.ds(r0, CHUNK), :]
            gcx = expand4((gcf_s if direction == 0 else gcb_s)[pl.ds(r0, CHUNK), :], direction * GDN_H)
            betax = expand4(g_s[pl.ds(r0, CHUNK), :], 2 * GDN_H + direction * GDN_H)
            grow = grp_s[ch, direction:direction + 1, :]
            gtot = grp_s[ch, 2 + direction:3 + direction, :]
            incl = (ri4 >= pos4) if direction == 0 else (ri4 <= pos4)
            strict = (ri4 > pos4) if direction == 0 else (ri4 < pos4)
            decay = jnp.exp(jnp.minimum(gcx - grow, 0.0))
            eg = jnp.exp(gcx)
            kb = k * betax
            prod = lax.dot_general(jnp.concatenate([q, kb], axis=0).astype(BF16), block_diag(k.astype(BF16)),
                                   (((1,), (1,)), ((), ())), preferred_element_type=F32)
            qk_s[unit] = jnp.where(incl, prod[:CHUNK] * decay, 0.0).astype(BF16)
            qg_s[unit] = (q * eg).astype(BF16)
            kt_s[unit] = (k * jnp.exp(gtot - gcx)).astype(BF16)
            eg_s[unit] = jnp.broadcast_to(jnp.exp(gtot), (SUBLANES, GDN_W))
            a_mat = jnp.where(strict, prod[CHUNK:] * decay, 0.0)
            work.append([unit, a_mat, v * betax, kb * eg])
        for stage in range(6):
            for item in work:
                _, p, ru, rw = item
                pb = p.astype(BF16)
                parts = [block_diag(ru.astype(BF16)), block_diag(rw.astype(BF16))]
                if stage < 5:
                    parts = [block_diag(pb)] + parts
                res = jnp.dot(pb, jnp.concatenate(parts, axis=1), preferred_element_type=F32)
                off = GDN_W if stage < 5 else 0
                du = res[:, off:off + GDN_W]
                dw = res[:, off + GDN_W:]
                item[1] = res[:, :GDN_W] if stage < 5 else None
                item[2] = ru - du if stage == 0 else ru + du
                item[3] = rw - dw if stage == 0 else rw + dw
        for unit, _, ru, rw in work:
            u_s[unit] = ru
            w_s[unit] = rw.astype(BF16)

    def solve_step(i, carry):
        solve_units([(0, 2 * i), (1, 2 * i), (0, 2 * i + 1), (1, 2 * i + 1)])
        return carry

    lax.fori_loop(0, nch // 2, solve_step, 0)

    def advance(chunks):
        first = []
        for direction, ch in enumerate(chunks):
            unit = direction * nch + ch
            state = st_s[direction]
            res = jnp.dot(jnp.concatenate([w_s[unit], qg_s[unit]], axis=0), state.astype(BF16),
                          preferred_element_type=F32)
            v_new = (u_s[unit] - res[:CHUNK]).astype(BF16)
            first.append((unit, state, res[CHUNK:], v_new))
        for direction, (unit, state, o_state, v_new) in enumerate(first):
            r0 = pl.multiple_of(chunks[direction] * CHUNK, CHUNK)
            o = o_state + jnp.dot(qk_s[unit], block_diag(v_new), preferred_element_type=F32)
            upd = lax.dot_general(kt_s[unit], v_new, (((0,), (0,)), ((), ())), preferred_element_type=F32)
            st_s[direction] = state * eg_s[unit, 0:1, :] + jnp.where(same_head, upd, 0.0)
            o_s[pl.ds(r0, CHUNK), :] = o_s[pl.ds(r0, CHUNK), :] + o

    def ctx_step(s, carry):
        advance((s, nc - 1 - s))
        return carry

    def lat_step(s, carry):
        advance((nc + s, nch - 1 - s))
        return carry

    lax.fori_loop(0, nc, ctx_step, 0)
    lax.fori_loop(0, nl, lat_step, 0)

    ng = ng_ref[0]

    def rms(s, i):
        return s * lax.rsqrt(_group_mean_square(s) + EPS) * ng[:, i * LANES:(i + 1) * LANES]

    for r0 in range(0, t, rt):
        y = _per_head(rms, o_s[r0:r0 + rt, :]) * _silu(gate_ref[r0:r0 + rt, :])
        out_ref[r0:r0 + rt, :] = y.astype(BF16)


def _gdn(layer, qkv, gate, ab, abt, conv_w, alog_r, dt_r, alog_c, dt_c, ng, dims):
    b, c, s, d, tm, nt = dims
    t = c + s
    n = qkv.shape[0]
    lay = lambda shape: pl.BlockSpec(shape, lambda i: (layer,) + (0,) * (len(shape) - 1))
    nch = t // CHUNK
    assert nch % 2 == 0
    once = pl.Buffered(1)
    return pl.pallas_call(
        functools.partial(_gdn_kernel, c, t, tm),
        grid=(b,),
        in_specs=[pl.BlockSpec((t, P_QKV), lambda i: (i, 0), pipeline_mode=once),
                  pl.BlockSpec((t, P_GATE), lambda i: (i, 0), pipeline_mode=once),
                  pl.BlockSpec((t, P_AB), lambda i: (i, 0)),
                  pl.BlockSpec((16, t), lambda i: (0, i)),
                  lay((1, CONV_K, P_QKV)), lay((1, 1, LANES)), lay((1, 1, LANES)),
                  lay((1, 16, LANES)), lay((1, 16, LANES)), lay((1, 1, GDN_W))],
        out_specs=pl.BlockSpec((t, GDN_W), lambda i: (i, 0)),
        out_shape=jax.ShapeDtypeStruct((n, GDN_W), BF16),
        scratch_shapes=[pltpu.VMEM((t + 2 * PAD_ROWS, P_QKV), F32),
                        pltpu.VMEM((t, GDN_W), F32), pltpu.VMEM((t, GDN_W), F32), pltpu.VMEM((t, GDN_W), F32),
                        pltpu.VMEM((t, LANES), F32), pltpu.VMEM((t, LANES), F32), pltpu.VMEM((t, LANES), F32),
                        pltpu.VMEM((nch, SUBLANES, GDN_W), F32),
                        pltpu.VMEM((2 * nch, CHUNK, GDN_W), F32),
                        pltpu.VMEM((2 * nch, CHUNK, GDN_W), BF16), pltpu.VMEM((2 * nch, CHUNK, GDN_W), BF16),
                        pltpu.VMEM((2 * nch, CHUNK, GDN_W), BF16), pltpu.VMEM((2 * nch, CHUNK, GDN_W), BF16),
                        pltpu.VMEM((2 * nch, SUBLANES, GDN_W), F32),
                        pltpu.VMEM((2, GDN_W, GDN_W), F32),
                        pltpu.VMEM((t, GDN_W), F32)],
        compiler_params=_cparams(("arbitrary",)),
        name="gdn",
    )(qkv, gate, ab, abt, conv_w, alog_r, dt_r, alog_c, dt_c, ng)


def _lru_kernel(c, t, rt, p_ref, cw_ref, cb_ref, wr_ref, br_ref, wi_ref, bi_ref, lam_ref, out_ref,
                xp_ref, a_s, b_s, h_s):
    _fill_padded(xp_ref, p_ref[:, :LRU_W], t)
    cw = cw_ref[0]
    cb = cb_ref[0]
    for r0 in range(0, t, rt):
        xr = _conv_rows(xp_ref, cw, r0, rt, c) + cb
        for dirn in range(2):
            r = _sigmoid(_mm(xr, wr_ref[0, dirn]) + br_ref[0, dirn:dirn + 1, :])
            i = _sigmoid(_mm(xr, wi_ref[0, dirn]) + bi_ref[0, dirn:dirn + 1, :])
            log_a = (-LRU_C) * r * _softplus(-lam_ref[0, dirn:dirn + 1, :])
            a = jnp.exp(log_a)
            a_s[dirn, r0:r0 + rt, :] = a
            b_s[dirn, r0:r0 + rt, :] = jnp.sqrt(1.0 - a * a) * (i * xr)

    row = lax.broadcasted_iota(jnp.int32, (SUBLANES, LRU_W), 0)

    def tile_scan(dirn, i, h_prev):
        r0 = pl.multiple_of(i * SUBLANES, SUBLANES)
        a = a_s[dirn, pl.ds(r0, SUBLANES), :]
        bx = b_s[dirn, pl.ds(r0, SUBLANES), :]
        for sh in (1, 2, 4):
            if dirn == 0:
                a_sh, b_sh, m = pltpu.roll(a, sh, 0), pltpu.roll(bx, sh, 0), row >= sh
            else:
                a_sh, b_sh, m = (pltpu.roll(a, SUBLANES - sh, 0), pltpu.roll(bx, SUBLANES - sh, 0),
                                 row < SUBLANES - sh)
            bx = jnp.where(m, a * b_sh + bx, bx)
            a = jnp.where(m, a * a_sh, a)
        h = a * h_prev + bx
        h_s[dirn, pl.ds(r0, SUBLANES), :] = h
        return h[SUBLANES - 1:SUBLANES, :] if dirn == 0 else h[0:1, :]

    n_t = t // SUBLANES
    n_c = c // SUBLANES
    zero = jnp.zeros((1, LRU_W), F32)

    def fwd_step(i, carry):
        hf, hb = carry
        hf = tile_scan(0, i, hf)
        hb = tile_scan(1, jnp.where(i < n_c, n_c - 1 - i, n_t - 1 - (i - n_c)), hb)
        return hf, hb

    lax.fori_loop(0, n_t, fwd_step, (zero, zero))

    for r0 in range(0, t, rt):
        yb = p_ref[r0:r0 + rt, LRU_W:]
        gelu = 0.5 * yb * (1.0 + jnp.tanh(0.7978845608028654 * (yb + 0.044715 * (yb * yb * yb))))
        out_ref[r0:r0 + rt, :] = ((h_s[0, r0:r0 + rt, :] + h_s[1, r0:r0 + rt, :]) * gelu).astype(BF16)


def _lru(layer, p, conv_w, conv_b, wr, br, wi, bi, lam, dims):
    b, c, s, d, tm, nt = dims
    t = c + s
    n = p.shape[0]
    lay = lambda shape: pl.BlockSpec(shape, lambda i: (layer,) + (0,) * (len(shape) - 1))
    return pl.pallas_call(
        functools.partial(_lru_kernel, c, t, tm),
        grid=(b,),
        in_specs=[pl.BlockSpec((t, P_LRU), lambda i: (i, 0)),
                  lay((1, CONV_K, LRU_W)), lay((1, 1, LRU_W)),
                  lay((1, 2, LRU_W, LRU_W)), lay((1, 2, LRU_W)),
                  lay((1, 2, LRU_W, LRU_W)), lay((1, 2, LRU_W)), lay((1, 2, LRU_W))],
        out_specs=pl.BlockSpec((t, LRU_W), lambda i: (i, 0)),
        out_shape=jax.ShapeDtypeStruct((n, LRU_W), BF16),
        scratch_shapes=[pltpu.VMEM((t + 2 * PAD_ROWS, LRU_W), F32),
                        pltpu.VMEM((2, t, LRU_W), F32), pltpu.VMEM((2, t, LRU_W), F32),
                        pltpu.VMEM((2, t, LRU_W), F32)],
        compiler_params=_cparams(("arbitrary",)),
        name="lru",
    )(p, conv_w, conv_b, wr, br, wi, bi, lam)


def _attn_rows(q, k, v):
    outs = []
    for hq in range(ATT_W // HEAD):
        kv = hq // ATT_GROUP
        qh = q[:, hq * HEAD:(hq + 1) * HEAD]
        kh = k[:, kv * HEAD:(kv + 1) * HEAD]
        vh = v[:, kv * HEAD:(kv + 1) * HEAD]
        s = lax.dot_general(qh, kh, (((1,), (1,)), ((), ())), preferred_element_type=F32)
        m = jnp.max(s, axis=-1, keepdims=True)
        p = jnp.exp(s - m)
        l = jnp.sum(p, axis=-1, keepdims=True)
        o = jnp.dot(p.astype(BF16), vh, preferred_element_type=F32)
        outs.append(o * (1.0 / l))
    return jnp.concatenate(outs, axis=1)


def _attn_kernel(c, with_ctx, q_ref, k_ref, v_ref, o_ref):
    def latent():
        o_ref[...] = _attn_rows(q_ref[...], k_ref[...], v_ref[...]).astype(BF16)

    if not with_ctx:
        latent()
        return
    j = pl.program_id(1)

    @pl.when(j == 0)
    def _():
        o_ref[...] = _attn_rows(q_ref[...], k_ref[0:c, :], v_ref[0:c, :]).astype(BF16)

    pl.when(j > 0)(latent)


def _attn(q, k, v, with_ctx, dims):
    b, c, s, d, tm, nt = dims
    t = c + s
    n = q.shape[0]
    off = 0 if with_ctx else 1
    return pl.pallas_call(
        functools.partial(_attn_kernel, c, with_ctx),
        grid=(b, nt - off),
        in_specs=[pl.BlockSpec((tm, ATT_W), lambda i, j: (i * nt + j + off, 0)),
                  pl.BlockSpec((t, ATT_KV_W), lambda i, j: (i, 0)),
                  pl.BlockSpec((t, ATT_KV_W), lambda i, j: (i, 0))],
        out_specs=pl.BlockSpec((tm, ATT_W), lambda i, j: (i * (nt - off) + j, 0)),
        out_shape=jax.ShapeDtypeStruct((b * (nt - off) * tm, ATT_W), BF16),
        compiler_params=_cparams(("arbitrary", "arbitrary")),
        name="attn",
    )(q, k, v)


R_E1, R_E2, R_G1, R_G2, R_RANK1, R_RANK2 = range(6)


def _outproj_kernel(tm, a_ref, b_ref, c_ref, x_ref, g1_ref, sh_ref, sc_ref, ng_ref, wo_ref, wr_ref, br_ref,
                    xo_ref, h_ref, route_ref, cnt_ref, carry_s):
    i = pl.program_id(0)

    @pl.when(i == 0)
    def _():
        carry_s[...] = jnp.zeros(carry_s.shape, F32)

    mix = (jnp.dot(a_ref[...], wo_ref[0, 0:GDN_W, :], preferred_element_type=F32)
           + jnp.dot(b_ref[...], wo_ref[0, GDN_W:GDN_W + LRU_W, :], preferred_element_type=F32)
           + jnp.dot(c_ref[...], wo_ref[0, GDN_W + LRU_W:, :], preferred_element_type=F32))
    x = x_ref[...] + g1_ref[0] * mix
    xo_ref[...] = x
    ms = jnp.mean(x * x, axis=-1, keepdims=True)
    h = (x * lax.rsqrt(ms + EPS) * ng_ref[0]) * (1.0 + sc_ref[0]) + sh_ref[0]
    h_ref[...] = h

    h_hi = h.astype(BF16)
    h_lo = (h - h_hi.astype(F32)).astype(BF16)
    logits = jnp.dot(jnp.concatenate([h_hi, h_lo, h_hi], axis=1), wr_ref[0],
                     preferred_element_type=F32) + br_ref[0]
    lane_i = lax.broadcasted_iota(jnp.int32, logits.shape, 1)
    lane = lane_i.astype(F32)
    lane_group = jnp.right_shift(lane_i, 3).astype(F32)
    big = jnp.float32(4 * LANES)
    neg = jnp.float32(-jnp.inf)
    is_group = (lane_i >= N_EXPERTS) & (lane_i < N_EXPERTS + N_GROUPS)
    gl = jnp.where(is_group, logits, neg)
    gm = jnp.max(gl, axis=-1, keepdims=True)
    pg_top = 1.0 / jnp.sum(jnp.where(is_group, jnp.exp(gl - gm), 0.0), axis=-1, keepdims=True)
    g_idx = jnp.min(jnp.where(gl == gm, lane, big), axis=-1, keepdims=True) - N_EXPERTS
    in_group = (lane_i < N_EXPERTS) & (lane_group == g_idx)
    le = jnp.where(in_group, logits, neg)
    m1 = jnp.max(le, axis=-1, keepdims=True)
    e1 = jnp.min(jnp.where(le == m1, lane, big), axis=-1, keepdims=True)
    le2 = jnp.where(lane == e1, neg, le)
    m2 = jnp.max(le2, axis=-1, keepdims=True)
    e2 = jnp.min(jnp.where(le2 == m2, lane, big), axis=-1, keepdims=True)
    z = jnp.sum(jnp.where(in_group, jnp.exp(le - m1), 0.0), axis=-1, keepdims=True)
    pe1 = 1.0 / z
    pe2 = jnp.exp(m2 - m1) / z
    gate1 = pg_top * pe1 / (pe1 + pe2)
    gate2 = pg_top * pe2 / (pe1 + pe2)

    sel1 = lane == e1
    sel2 = lane == e2
    onehot = jnp.where(sel1 | sel2, 1.0, 0.0)
    ri = lax.broadcasted_iota(jnp.int32, (tm, tm), 0)
    ci = lax.broadcasted_iota(jnp.int32, (tm, tm), 1)
    before = _mm(jnp.where(ri > ci, 1.0, 0.0), onehot) + carry_s[0:1, :]
    rank1 = jnp.sum(jnp.where(sel1, before, 0.0), axis=-1, keepdims=True)
    rank2 = jnp.sum(jnp.where(sel2, before, 0.0), axis=-1, keepdims=True)
    carry = carry_s[0:1, :] + jnp.sum(onehot, axis=0, keepdims=True)
    carry_s[...] = jnp.broadcast_to(carry, carry_s.shape)
    cnt_ref[...] = jnp.broadcast_to(carry, cnt_ref.shape)

    route = jnp.zeros(logits.shape, F32)
    for col, val in ((R_E1, e1), (R_E2, e2), (R_G1, gate1), (R_G2, gate2),
                     (R_RANK1, rank1), (R_RANK2, rank2)):
        route = jnp.where(lane_i == col, val, route)
    route_ref[...] = route


def _tile_map(with_ctx, nt):
    if with_ctx:
        return lambda i: i
    return lambda i: (i // (nt - 1)) * nt + 1 + i % (nt - 1)


def _outproj(layer, a, bm, cm, x, mod, norm2_g, w_out, w_route, b_route, with_ctx, dims):
    b, c, s, d, tm, nt = dims
    n = x.shape[0]
    n_layers = norm2_g.shape[0]
    rows = mod.shape[0] // (6 * n_layers)
    tile = _tile_map(with_ctx, nt)
    n_tiles = b * (nt if with_ctx else nt - 1)

    def mod_idx(chunk):
        def f(i):
            ti = tile(i)
            r = jnp.where(ti % nt == 0, b, ti // nt)
            return ((layer * 6 + chunk) * rows + r, 0, 0)
        return f

    row_spec = lambda w: pl.BlockSpec((tm, w), lambda i: (tile(i), 0))
    compact = lambda w: pl.BlockSpec((tm, w), lambda i: (i, 0))
    lay = lambda shape: pl.BlockSpec(shape, lambda i: (layer,) + (0,) * (len(shape) - 1))
    n_moe = n_tiles * tm
    return pl.pallas_call(
        functools.partial(_outproj_kernel, tm),
        grid=(n_tiles,),
        in_specs=[row_spec(GDN_W), row_spec(LRU_W), compact(ATT_W), row_spec(d),
                  pl.BlockSpec((1, 1, d), mod_idx(2)), pl.BlockSpec((1, 1, d), mod_idx(3)),
                  pl.BlockSpec((1, 1, d), mod_idx(4)),
                  lay((1, 1, d)), lay((1, d, d)), lay((1, 3 * d, LANES)), lay((1, 1, LANES))],
        out_specs=[row_spec(d), compact(d), compact(LANES),
                   pl.BlockSpec((SUBLANES, LANES), lambda i: (0, 0))],
        out_shape=[jax.ShapeDtypeStruct((n, d), F32), jax.ShapeDtypeStruct((n_moe, d), F32),
                   jax.ShapeDtypeStruct((n_moe, LANES), F32), jax.ShapeDtypeStruct((SUBLANES, LANES), F32)],
        scratch_shapes=[pltpu.VMEM((SUBLANES, LANES), F32)],
        input_output_aliases={3: 0},
        compiler_params=_cparams(("arbitrary",)),
        name="outproj",
    )(a, bm, cm, x, mod, mod, mod, norm2_g, w_out, w_route, b_route)


def _dest_kernel(route_ref, cnt_ref, dest_ref):
    route = route_ref[...]
    lane = lax.broadcasted_iota(jnp.int32, route.shape, 1)
    counts = cnt_ref[...]
    padded = jnp.floor((counts + (MOE_BLOCK - 1)) * (1.0 / MOE_BLOCK)) * MOE_BLOCK
    li = lax.broadcasted_iota(jnp.int32, (LANES, LANES), 0)
    lj = lax.broadcasted_iota(jnp.int32, (LANES, LANES), 1)
    start = _mm_exact(padded, jnp.where((li < lj) & (li < N_EXPERTS), 1.0, 0.0))[0:1, :]

    def col(j):
        return jnp.sum(jnp.where(lane == j, route, 0.0), axis=-1, keepdims=True)

    def slot(e, rank):
        return jnp.sum(jnp.where(lane == e.astype(jnp.int32), start, 0.0), axis=-1, keepdims=True) + rank

    d1 = slot(col(R_E1), col(R_RANK1))
    d2 = slot(col(R_E2), col(R_RANK2))
    dest_ref[...] = jnp.where(lane == 0, d1, jnp.where(lane == 1, d2, 0.0)).astype(jnp.int32)


def _dest(route, counts, dims):
    tm = dims[4]
    n = route.shape[0]
    return pl.pallas_call(
        _dest_kernel,
        grid=(n // tm,),
        in_specs=[pl.BlockSpec((tm, LANES), lambda i: (i, 0)),
                  pl.BlockSpec((SUBLANES, LANES), lambda i: (0, 0))],
        out_specs=pl.BlockSpec((tm, LANES), lambda i: (i, 0)),
        out_shape=jax.ShapeDtypeStruct((n, LANES), jnp.int32),
        compiler_params=_cparams(("arbitrary",)),
        name="dest",
    )(route, counts)


def _dispatch_kernel(tm, d1_ref, d2_ref, h_ref, xb_in_ref, xb_ref, sem):
    del xb_in_ref
    base = pl.program_id(0) * tm

    def issue(r, carry):
        src = h_ref.at[pl.ds(r, 1)]
        pltpu.make_async_copy(src, xb_ref.at[pl.ds(d1_ref[base + r], 1)], sem).start()
        pltpu.make_async_copy(src, xb_ref.at[pl.ds(d2_ref[base + r], 1)], sem).start()
        return carry

    lax.fori_loop(0, tm, issue, 0, unroll=8)
    for _ in range(2):
        pltpu.make_async_copy(h_ref, xb_ref.at[pl.ds(0, tm)], sem).wait()


def _dispatch(d1, d2, h, n_slots, dims):
    tm = dims[4]
    d = h.shape[1]
    return pl.pallas_call(
        functools.partial(_dispatch_kernel, tm),
        grid_spec=pltpu.PrefetchScalarGridSpec(
            num_scalar_prefetch=2,
            grid=(h.shape[0] // tm,),
            in_specs=[pl.BlockSpec((tm, d), lambda i, d1, d2: (i, 0)),
                      pl.BlockSpec(memory_space=pl.ANY)],
            out_specs=pl.BlockSpec(memory_space=pl.ANY),
            scratch_shapes=[pltpu.SemaphoreType.DMA(())]),
        out_shape=jax.ShapeDtypeStruct((n_slots, d), F32),
        input_output_aliases={3: 0},
        compiler_params=_cparams(("arbitrary",)),
        name="dispatch",
    )(d1, d2, h, jnp.zeros((n_slots, d), F32))


def _expert_kernel(be_ref, nact_ref, xb_ref, w1_ref, w3_ref, w2_ref, yb_ref, w1_s, w3_s, w2_s):
    i = pl.program_id(0)
    changed = (i == 0) | (be_ref[i] != be_ref[jnp.maximum(i - 1, 0)])

    @pl.when(changed)
    def _():
        w1_s[...] = w1_ref[0, 0].astype(BF16)
        w3_s[...] = w3_ref[0, 0].astype(BF16)
        w2_s[...] = w2_ref[0, 0].astype(BF16)

    @pl.when(i < nact_ref[0])
    def _():
        x = xb_ref[...].astype(BF16)
        h1 = jnp.dot(x, w1_s[...], preferred_element_type=F32)
        h3 = jnp.dot(x, w3_s[...], preferred_element_type=F32)
        act = (_silu(h1) * h3).astype(BF16)
        yb_ref[...] = jnp.dot(act, w2_s[...], preferred_element_type=F32)

    @pl.when(i >= nact_ref[0])
    def _():
        yb_ref[...] = jnp.zeros(yb_ref.shape, F32)


def _experts(layer, block_e, nact, xb, w1, w3, w2):
    n_slots, d = xb.shape
    hid = w1.shape[-1]
    return pl.pallas_call(
        _expert_kernel,
        grid_spec=pltpu.PrefetchScalarGridSpec(
            num_scalar_prefetch=2,
            grid=(n_slots // MOE_BLOCK,),
            in_specs=[pl.BlockSpec((MOE_BLOCK, d), lambda i, be, na: (i, 0)),
                      pl.BlockSpec((1, 1, d, hid), lambda i, be, na: (layer, be[i], 0, 0)),
                      pl.BlockSpec((1, 1, d, hid), lambda i, be, na: (layer, be[i], 0, 0)),
                      pl.BlockSpec((1, 1, hid, d), lambda i, be, na: (layer, be[i], 0, 0))],
            out_specs=pl.BlockSpec((MOE_BLOCK, d), lambda i, be, na: (i, 0)),
            scratch_shapes=[pltpu.VMEM((d, hid), BF16), pltpu.VMEM((d, hid), BF16),
                            pltpu.VMEM((hid, d), BF16)]),
        out_shape=jax.ShapeDtypeStruct((n_slots, d), F32),
        compiler_params=_cparams(("arbitrary",)),
        name="experts",
    )(block_e, nact, xb, w1, w3, w2)


def _combine_kernel(tm, final, d1_ref, d2_ref, x_ref, route_ref, g2_ref, fg_ref, yb_ref, o_ref,
                    y1_s, y2_s, sem):
    base = pl.program_id(0) * tm

    def issue(r, carry):
        pltpu.make_async_copy(yb_ref.at[pl.ds(d1_ref[base + r], 1)], y1_s.at[pl.ds(r, 1)], sem).start()
        pltpu.make_async_copy(yb_ref.at[pl.ds(d2_ref[base + r], 1)], y2_s.at[pl.ds(r, 1)], sem).start()
        return carry

    lax.fori_loop(0, tm, issue, 0, unroll=8)
    pltpu.make_async_copy(yb_ref.at[pl.ds(0, tm)], y1_s, sem).wait()
    pltpu.make_async_copy(yb_ref.at[pl.ds(0, tm)], y2_s, sem).wait()

    route = route_ref[...]
    lane = lax.broadcasted_iota(jnp.int32, route.shape, 1)
    gate1 = jnp.sum(jnp.where(lane == R_G1, route, 0.0), axis=-1, keepdims=True)
    gate2 = jnp.sum(jnp.where(lane == R_G2, route, 0.0), axis=-1, keepdims=True)
    x = x_ref[...] + g2_ref[0] * (y1_s[...] * gate1 + y2_s[...] * gate2)
    if final:
        ms = jnp.mean(x * x, axis=-1, keepdims=True)
        x = x * lax.rsqrt(ms + EPS) * fg_ref[...]
    o_ref[...] = x


def _combine(layer, d1, d2, x, route, mod, final_g, yb, with_ctx, final, n_layers, dims):
    b, c, s, d, tm, nt = dims
    n = x.shape[0]
    rows = mod.shape[0] // (6 * n_layers)
    tile = _tile_map(with_ctx, nt)
    n_tiles = b * (nt if with_ctx else nt - 1)

    def mod_idx(i, d1, d2):
        ti = tile(i)
        r = jnp.where(ti % nt == 0, b, ti // nt)
        return ((layer * 6 + 5) * rows + r, 0, 0)

    if final:
        out_spec = pl.BlockSpec((tm, d), lambda i, d1, d2: (i, 0))
        out_shape = jax.ShapeDtypeStruct((n_tiles * tm, d), F32)
        aliases = {}
    else:
        out_spec = pl.BlockSpec((tm, d), lambda i, d1, d2: (tile(i), 0))
        out_shape = jax.ShapeDtypeStruct((n, d), F32)
        aliases = {2: 0}
    return pl.pallas_call(
        functools.partial(_combine_kernel, tm, final),
        grid_spec=pltpu.PrefetchScalarGridSpec(
            num_scalar_prefetch=2,
            grid=(n_tiles,),
            in_specs=[pl.BlockSpec((tm, d), lambda i, d1, d2: (tile(i), 0)),
                      pl.BlockSpec((tm, LANES), lambda i, d1, d2: (i, 0)),
                      pl.BlockSpec((1, 1, d), mod_idx),
                      pl.BlockSpec((1, d), lambda i, d1, d2: (0, 0)),
                      pl.BlockSpec(memory_space=pl.ANY)],
            out_specs=out_spec,
            scratch_shapes=[pltpu.VMEM((tm, d), F32), pltpu.VMEM((tm, d), F32),
                            pltpu.SemaphoreType.DMA(())]),
        out_shape=out_shape,
        input_output_aliases=aliases,
        compiler_params=_cparams(("arbitrary",)),
        name="combine",
    )(d1, d2, x, route, mod, final_g, yb)


def _rope_tables(s, tm):
    rows = s // GRID_W
    row = jnp.repeat(jnp.arange(rows, dtype=F32), GRID_W)
    col = jnp.tile(jnp.arange(GRID_W, dtype=F32), rows)
    axis_dim = HEAD // 2
    inv_freq = ROPE_THETA ** (-jnp.arange(0, axis_dim, 2, dtype=F32) / axis_dim)
    ar = row[:, None] * inv_freq
    ac = col[:, None] * inv_freq
    cos = jnp.concatenate([jnp.cos(ar), jnp.cos(ar), jnp.cos(ac), jnp.cos(ac)], axis=1)
    sin = jnp.concatenate([-jnp.sin(ar), jnp.sin(ar), -jnp.sin(ac), jnp.sin(ac)], axis=1)
    cos = jnp.concatenate([jnp.ones((tm, HEAD), F32), cos], axis=0)
    sin = jnp.concatenate([jnp.zeros((tm, HEAD), F32), sin], axis=0)
    return jnp.tile(cos, (1, 2)), jnp.tile(sin, (1, 2))


def _block_diag(w):
    n_layers = w.shape[0]
    eye = jnp.eye(LRU_BLOCKS, dtype=w.dtype)
    full = jnp.einsum('ldnij,nm->ldnimj', w, eye)
    return full.reshape(n_layers, 2, LRU_W, LRU_W)


def _pad_lanes(a, width=LANES):
    return jnp.pad(a, [(0, 0)] * (a.ndim - 1) + [(0, width - a.shape[-1])])


def kernel(x, c, ctx, c_ctx, w_ada, b_ada, norm1_g, norm2_g, w_in, w_out, gdn_conv_w, gdn_a_log, gdn_dt_bias, gdn_norm_g, lru_conv_w, lru_conv_b, lru_w_r, lru_b_r, lru_w_i, lru_b_i, lru_lambda, attn_q_norm_g, attn_k_norm_g, moe_w_group, moe_b_group, moe_w_expert, moe_b_expert, moe_w1, moe_w3, moe_w2, final_norm_g):
    bsz, s, d = x.shape
    cl = ctx.shape[1]
    n_layers = w_in.shape[0]
    tm = cl
    assert s % tm == 0 and tm % CHUNK == 0 and s % GRID_W == 0
    nt = (cl + s) // tm
    dims = (bsz, cl, s, d, tm, nt)

    o1 = 4 * GDN_W + 4 * GDN_H
    o2 = o1 + 2 * LRU_W
    w_ab = w_in[:, :, 4 * GDN_W:o1]
    w_pack = jnp.concatenate([w_in[:, :, :4 * GDN_W], _pad_lanes(w_ab), w_in[:, :, o1:o2], w_in[:, :, o2:]],
                             axis=-1).astype(BF16)
    wabt = jnp.swapaxes(w_ab, 1, 2).astype(BF16)
    gq = jnp.tile(attn_q_norm_g, (1, 2))[:, None, :]
    gk = jnp.tile(attn_k_norm_g, (1, 2))[:, None, :]
    cos_t, sin_t = _rope_tables(s, tm)
    alog = gdn_a_log.reshape(n_layers, 2 * GDN_H)
    dtb = gdn_dt_bias.reshape(n_layers, 2 * GDN_H)
    alog_r = _pad_lanes(alog)[:, None, :]
    dt_r = _pad_lanes(dtb)[:, None, :]
    alog_c = jnp.broadcast_to(_pad_lanes(alog, 16)[:, :, None], (n_layers, 16, LANES))
    dt_c = jnp.broadcast_to(_pad_lanes(dtb, 16)[:, :, None], (n_layers, 16, LANES))
    gdn_ng = jnp.tile(gdn_norm_g, (1, GDN_H))[:, None, :]
    wr_bd = _block_diag(lru_w_r).astype(BF16)
    wi_bd = _block_diag(lru_w_i).astype(BF16)
    w_out_b = w_out.astype(BF16)
    w_route = _pad_lanes(jnp.concatenate([moe_w_expert, moe_w_group], axis=-1))
    w_route_hi = w_route.astype(BF16)
    w_route_lo = (w_route - w_route_hi.astype(F32)).astype(BF16)
    w_route = jnp.concatenate([w_route_hi, w_route_hi, w_route_lo], axis=1)
    b_route = _pad_lanes(jnp.concatenate([moe_b_expert, moe_b_group], axis=-1))[:, None, :]

    cv = jnp.concatenate([c, c_ctx[None, :]], axis=0)
    rows = -(-cv.shape[0] // SUBLANES) * SUBLANES
    cv = jnp.pad(cv, ((0, rows - cv.shape[0]), (0, 0)))
    mod = _adaln(cv, w_ada, b_ada).reshape(n_layers * 6 * rows, 1, d)

    xf = jnp.concatenate([ctx, x], axis=1).reshape(bsz * (cl + s), d)
    out = None
    for layer in range(n_layers):
        with_ctx = layer < n_layers - 1
        qkv, gate, ab, abt, plru, q, k, v = _inproj(
            layer, xf, mod, norm1_g[:, None, :], w_pack, wabt, gq, gk, cos_t, sin_t, dims)
        mix_a = _gdn(layer, qkv, gate, ab, abt, gdn_conv_w, alog_r, dt_r, alog_c, dt_c, gdn_ng, dims)
        mix_b = _lru(layer, plru, lru_conv_w, lru_conv_b[:, None, :], wr_bd, lru_b_r, wi_bd, lru_b_i,
                     lru_lambda, dims)
        mix_c = _attn(q, k, v, with_ctx, dims)
        xf, h, route, counts = _outproj(layer, mix_a, mix_b, mix_c, xf, mod, norm2_g[:, None, :], w_out_b,
                                        w_route, b_route, with_ctx, dims)
        dest = _dest(route, counts, dims)
        d1 = dest[:, 0]
        d2 = dest[:, 1]
        n_tok = bsz * ((cl + s) if with_ctx else s)
        n_blocks = -(-(2 * n_tok) // MOE_BLOCK) + N_EXPERTS
        cnt = counts[0, :N_EXPERTS].astype(jnp.int32)
        pad_end = jnp.cumsum((cnt + MOE_BLOCK - 1) // MOE_BLOCK * MOE_BLOCK)
        block_row = jnp.arange(n_blocks, dtype=jnp.int32)[:, None] * MOE_BLOCK
        block_e = jnp.minimum(jnp.sum((pad_end[None, :] <= block_row).astype(jnp.int32), axis=1),
                              N_EXPERTS - 1)
        nact = (pad_end[-1:] // MOE_BLOCK).astype(jnp.int32)
        xb = _dispatch(d1, d2, h, n_blocks * MOE_BLOCK, dims)
        yb = _experts(layer, block_e, nact, xb, moe_w1, moe_w3, moe_w2)
        res = _combine(layer, d1, d2, xf, route, mod, final_norm_g[None, :], yb, with_ctx,
                       not with_ctx, n_layers, dims)
        if with_ctx:
            xf = res
        else:
            out = res
    return out.reshape(bsz, s, d)
```

```python
import functools

import jax
import jax.numpy as jnp
from jax import lax
from jax.experimental import pallas as pl
from jax.experimental.pallas import tpu as pltpu

F32 = jnp.float32
BF16 = jnp.bfloat16
HIGHEST = lax.Precision.HIGHEST

HEAD = 64
GDN_W = 256
GDN_H = GDN_W // HEAD
CHUNK = 64
LRU_W = 256
LRU_BLOCKS = 4
LRU_C = 8.0
ATT_W = 512
ATT_KV_W = 128
ATT_GROUP = 4
CONV_K = 4
N_GROUPS = 8
N_EXPERTS = 64
EXPERTS_PER_GROUP = 8
MOE_HIDDEN = 512
MOE_BLOCK = 128
GRID_W = 64
ROPE_THETA = 10000.0
EPS = 1e-6
LANES = 128
SUBLANES = 8
PAD_ROWS = 8
VMEM_LIMIT = 56 * 1024 * 1024

P_QKV = 3 * GDN_W
P_GATE = GDN_W
P_AB = LANES
P_LRU = 2 * LRU_W
P_ATT = ATT_W + 2 * ATT_KV_W
P_ALL = P_QKV + P_GATE + P_AB + P_LRU + P_ATT


def _cparams(sem):
    return pltpu.CompilerParams(dimension_semantics=sem, vmem_limit_bytes=VMEM_LIMIT)


def _sigmoid(x):
    return 1.0 / (1.0 + jnp.exp(-x))


def _silu(x):
    return x * _sigmoid(x)


def _softplus(x):
    return jnp.maximum(x, 0.0) + jnp.log(1.0 + jnp.exp(-jnp.abs(x)))


def _mm(a, b):
    return jnp.dot(a.astype(BF16), b.astype(BF16), preferred_element_type=F32)


def _mm_nt(a, b):
    return lax.dot_general(a.astype(BF16), b.astype(BF16), (((1,), (1,)), ((), ())),
                           preferred_element_type=F32)


def _mm_tn(a, b):
    return lax.dot_general(a.astype(BF16), b.astype(BF16), (((0,), (0,)), ((), ())),
                           preferred_element_type=F32)


def _mm_exact(a, b):
    return jnp.dot(a, b, precision=HIGHEST, preferred_element_type=F32)


def _group_mean_square(x):
    lane = lax.broadcasted_iota(jnp.int32, x.shape, 1)
    lo = lane < HEAD
    x2 = x * x
    s_lo = jnp.sum(jnp.where(lo, x2, 0.0), axis=-1, keepdims=True)
    s_hi = jnp.sum(jnp.where(lo, 0.0, x2), axis=-1, keepdims=True)
    return jnp.where(lo, s_lo, s_hi) * (1.0 / HEAD)


def _per_head(fn, x):
    n = x.shape[1] // LANES
    return jnp.concatenate([fn(x[:, i * LANES:(i + 1) * LANES], i) for i in range(n)], axis=1)


def _adaln_kernel(cv_ref, w_ref, b_ref, o_ref):
    o_ref[0] = _mm_exact(_silu(cv_ref[...]), w_ref[0]) + b_ref[0]


def _adaln(cv, w_ada, b_ada):
    n_layers, d, _ = w_ada.shape
    rows = cv.shape[0]
    return pl.pallas_call(
        _adaln_kernel,
        grid=(n_layers, 6),
        in_specs=[pl.BlockSpec((rows, d), lambda l, j: (0, 0)),
                  pl.BlockSpec((1, d, d), lambda l, j: (l, 0, j)),
                  pl.BlockSpec((1, 1, d), lambda l, j: (l * 6 + j, 0, 0))],
        out_specs=pl.BlockSpec((1, rows, d), lambda l, j: (l * 6 + j, 0, 0)),
        out_shape=jax.ShapeDtypeStruct((n_layers * 6, rows, d), F32),
        compiler_params=_cparams(("arbitrary", "arbitrary")),
        name="adaln",
    )(cv, w_ada, b_ada.reshape(n_layers * 6, 1, d))


def _inproj_kernel(x_ref, sh_ref, sc_ref, g_ref, w_ref, wabt_ref, gq_ref, gk_ref, cos_ref, sin_ref,
                   qkv_ref, gate_ref, ab_ref, abt_ref, lru_ref, q_ref, k_ref, v_ref):
    x = x_ref[...]
    ms = jnp.mean(x * x, axis=-1, keepdims=True)
    h = (x * lax.rsqrt(ms + EPS) * g_ref[0]) * (1.0 + sc_ref[0]) + sh_ref[0]
    hb = h.astype(BF16)
    p = jnp.dot(hb, w_ref[0], preferred_element_type=F32)
    o = 0
    qkv_ref[...] = p[:, o:o + P_QKV]
    o += P_QKV
    gate_ref[...] = p[:, o:o + P_GATE]
    o += P_GATE
    ab_ref[...] = p[:, o:o + P_AB]
    o += P_AB
    lru_ref[...] = p[:, o:o + P_LRU]
    o += P_LRU
    att = p[:, o:o + P_ATT]
    abt_ref[...] = lax.dot_general(wabt_ref[0], hb, (((1,), (1,)), ((), ())), preferred_element_type=F32)

    cos = cos_ref[...]
    sin = sin_ref[...]
    lane = lax.broadcasted_iota(jnp.int32, cos.shape, 1)
    first_half = (lane & 16) == 0

    def norm_rope(gain):
        def fn(s, _):
            y = s * lax.rsqrt(_group_mean_square(s) + EPS) * gain
            swapped = jnp.where(first_half, pltpu.roll(y, LANES - 16, 1), pltpu.roll(y, 16, 1))
            return y * cos + swapped * sin
        return fn

    q = _per_head(norm_rope(gq_ref[0]), att[:, :ATT_W])
    k = _per_head(norm_rope(gk_ref[0]), att[:, ATT_W:ATT_W + ATT_KV_W])
    q_ref[...] = (q * (HEAD ** -0.5)).astype(BF16)
    k_ref[...] = k.astype(BF16)
    v_ref[...] = att[:, ATT_W + ATT_KV_W:].astype(BF16)


def _inproj(layer, x, mod, norm1_g, w_pack, wabt, gq, gk, cos_t, sin_t, dims):
    b, c, s, d, tm, nt = dims
    n = x.shape[0]
    rows = mod.shape[0] // (6 * norm1_g.shape[0])

    def mod_idx(chunk):
        def f(i):
            r = jnp.where(i % nt == 0, b, i // nt)
            return ((layer * 6 + chunk) * rows + r, 0, 0)
        return f

    row_spec = lambda w: pl.BlockSpec((tm, w), lambda i: (i, 0))
    return pl.pallas_call(
        _inproj_kernel,
        grid=(n // tm,),
        in_specs=[row_spec(d),
                  pl.BlockSpec((1, 1, d), mod_idx(0)),
                  pl.BlockSpec((1, 1, d), mod_idx(1)),
                  pl.BlockSpec((1, 1, d), lambda i: (layer, 0, 0)),
                  pl.BlockSpec((1, d, P_ALL), lambda i: (layer, 0, 0)),
                  pl.BlockSpec((1, 16, d), lambda i: (layer, 0, 0)),
                  pl.BlockSpec((1, 1, LANES), lambda i: (layer, 0, 0)),
                  pl.BlockSpec((1, 1, LANES), lambda i: (layer, 0, 0)),
                  pl.BlockSpec((tm, LANES), lambda i: (i % nt, 0)),
                  pl.BlockSpec((tm, LANES), lambda i: (i % nt, 0))],
        out_specs=[row_spec(P_QKV), row_spec(P_GATE), row_spec(P_AB),
                   pl.BlockSpec((16, tm), lambda i: (0, i)),
                   row_spec(P_LRU), row_spec(ATT_W), row_spec(ATT_KV_W), row_spec(ATT_KV_W)],
        out_shape=[jax.ShapeDtypeStruct((n, P_QKV), F32), jax.ShapeDtypeStruct((n, P_GATE), F32),
                   jax.ShapeDtypeStruct((n, P_AB), F32), jax.ShapeDtypeStruct((16, n), F32),
                   jax.ShapeDtypeStruct((n, P_LRU), F32), jax.ShapeDtypeStruct((n, ATT_W), BF16),
                   jax.ShapeDtypeStruct((n, ATT_KV_W), BF16), jax.ShapeDtypeStruct((n, ATT_KV_W), BF16)],
        compiler_params=_cparams(("arbitrary",)),
        name="inproj",
    )(x, mod, mod, norm1_g, w_pack, wabt, gq, gk, cos_t, sin_t)


def _conv_rows(xp_ref, w, r0, rt, c):
    base = r0 + PAD_ROWS
    xm1 = xp_ref[base - 1:base - 1 + rt, :]
    x0 = xp_ref[base:base + rt, :]
    xp1 = xp_ref[base + 1:base + 1 + rt, :]
    xp2 = xp_ref[base + 2:base + 2 + rt, :]
    row = r0 + lax.broadcasted_iota(jnp.int32, (rt, 1), 0)
    xm1 = jnp.where(row == c, 0.0, xm1)
    xp1 = jnp.where(row == c - 1, 0.0, xp1)
    xp2 = jnp.where((row == c - 1) | (row == c - 2), 0.0, xp2)
    return w[0:1, :] * xm1 + w[1:2, :] * x0 + w[2:3, :] * xp1 + w[3:4, :] * xp2


def _fill_padded(xp_ref, x, t):
    zeros = jnp.zeros((PAD_ROWS, xp_ref.shape[1]), F32)
    xp_ref[0:PAD_ROWS, :] = zeros
    xp_ref[PAD_ROWS + t:2 * PAD_ROWS + t, :] = zeros
    xp_ref[PAD_ROWS:PAD_ROWS + t, :] = x


def _gdn_kernel(c, t, rt, qkv_ref, gate_ref, ab_ref, abt_ref, cw_ref, alog_r_ref, dt_r_ref, alog_c_ref,
                dt_c_ref, ng_ref, out_ref,
                xp_ref, q_s, k_s, v_s, g_s, gcf_s, gcb_s, grp_s, u_s, w_s, qk_s, qg_s, kt_s, eg_s, st_s, o_s):
    nch = t // CHUNK
    nc = c // CHUNK
    nl = nch - nc

    _fill_padded(xp_ref, qkv_ref[...], t)
    cw = cw_ref[0]

    def l2n(s, _):
        return s * lax.rsqrt(_group_mean_square(s) * HEAD + EPS)

    for r0 in range(0, t, rt):
        y = _silu(_conv_rows(xp_ref, cw, r0, rt, c))
        q_s[r0:r0 + rt, :] = _per_head(l2n, y[:, :GDN_W]) * (HEAD ** -0.5)
        k_s[r0:r0 + rt, :] = _per_head(l2n, y[:, GDN_W:2 * GDN_W])
        v_s[r0:r0 + rt, :] = y[:, 2 * GDN_W:]

    ab = ab_ref[...]
    lane = lax.broadcasted_iota(jnp.int32, ab.shape, 1)
    gval = -jnp.exp(alog_r_ref[0]) * _softplus(ab + dt_r_ref[0])
    g_s[...] = jnp.where(lane < 2 * GDN_H, gval, _sigmoid(ab))
    abt = abt_ref[...]
    g_t = -jnp.exp(alog_c_ref[0][:, 0:1]) * _softplus(abt + dt_c_ref[0][:, 0:1])

    ri = lax.broadcasted_iota(jnp.int32, (CHUNK, CHUNK), 0)
    ci = lax.broadcasted_iota(jnp.int32, (CHUNK, CHUNK), 1)
    lower_f = (ri >= ci).astype(F32)
    upper_f = (ri <= ci).astype(F32)
    row8 = lax.broadcasted_iota(jnp.int32, (2 * GDN_H, CHUNK), 0)
    for ch in range(nch):
        r0 = ch * CHUNK
        gch = g_s[r0:r0 + CHUNK, :]
        gcf_s[r0:r0 + CHUNK, :] = _mm_exact(lower_f, gch)
        gcb_s[r0:r0 + CHUNK, :] = _mm_exact(upper_f, gch)
        gtc = g_t[0:2 * GDN_H, r0:r0 + CHUNK]
        gr = jnp.where(row8 < GDN_H, _mm_exact(gtc, upper_f), _mm_exact(gtc, lower_f))
        for direction in range(2):
            rows = [gr[direction * GDN_H + hd:direction * GDN_H + hd + 1, :] for hd in range(GDN_H)]
            edge = CHUNK - 1 if direction == 0 else 0
            tots = [jnp.broadcast_to(r[:, edge:edge + 1], (1, CHUNK)) for r in rows]
            grp_s[ch, direction:direction + 1, :] = jnp.concatenate(rows, axis=1)
            grp_s[ch, 2 + direction:3 + direction, :] = jnp.concatenate(tots, axis=1)

    st_s[...] = jnp.zeros(st_s.shape, F32)
    o_s[...] = jnp.zeros(o_s.shape, F32)

    ri4 = lax.broadcasted_iota(jnp.int32, (CHUNK, GDN_W), 0)
    li4 = lax.broadcasted_iota(jnp.int32, (CHUNK, GDN_W), 1)
    pos4 = li4 & (CHUNK - 1)
    head_masks = [jnp.where((li4 >= hd * HEAD) & (li4 < (hd + 1) * HEAD), 1.0, 0.0).astype(BF16)
                  for hd in range(GDN_H)]
    lane_lo = lax.broadcasted_iota(jnp.int32, (CHUNK, LANES), 1) < HEAD
    bi = jnp.right_shift(lax.broadcasted_iota(jnp.int32, (GDN_W, GDN_W), 0), 6)
    bj = jnp.right_shift(lax.broadcasted_iota(jnp.int32, (GDN_W, GDN_W), 1), 6)
    same_head = bi == bj

    def block_diag(xb):
        return jnp.concatenate([xb * m for m in head_masks], axis=0)

    def expand4(x, base):
        cols = [jnp.broadcast_to(x[:, base + hd:base + hd + 1], (CHUNK, LANES)) for hd in range(GDN_H)]
        return jnp.concatenate([jnp.where(lane_lo, cols[0], cols[1]), jnp.where(lane_lo, cols[2], cols[3])],
                               axis=1)

    def solve_units(units):
        work = []
        for direction, ch in units:
            r0 = pl.multiple_of(ch * CHUNK, CHUNK)
            unit = direction * nch + ch
            q = q_s[pl.ds(r0, CHUNK), :]
            k = k_s[pl.ds(r0, CHUNK), :]
            v = v_s[pl.ds(r0, CHUNK), :]
            gcx = expand4((gcf_s if direction == 0 else gcb_s)[pl.ds(r0, CHUNK), :], direction * GDN_H)
            betax = expand4(g_s[pl.ds(r0, CHUNK), :], 2 * GDN_H + direction * GDN_H)
            grow = grp_s[ch, direction:direction + 1, :]
            gtot = grp_s[ch, 2 + direction:3 + direction, :]
            incl = (ri4 >= pos4) if direction == 0 else (ri4 <= pos4)
            strict = (ri4 > pos4) if direction == 0 else (ri4 < pos4)
            decay = jnp.exp(jnp.minimum(gcx - grow, 0.0))
            eg = jnp.exp(gcx)
            kb = k * betax
            prod = lax.dot_general(jnp.concatenate([q, kb], axis=0).astype(BF16), block_diag(k.astype(BF16)),
                                   (((1,), (1,)), ((), ())), preferred_element_type=F32)
            qk_s[unit] = jnp.where(incl, prod[:CHUNK] * decay, 0.0).astype(BF16)
            qg_s[unit] = (q * eg).astype(BF16)
            kt_s[unit] = (k * jnp.exp(gtot - gcx)).astype(BF16)
            eg_s[unit] = jnp.broadcast_to(jnp.exp(gtot), (SUBLANES, GDN_W))
            a_mat = jnp.where(strict, prod[CHUNK:] * decay, 0.0)
            work.append([unit, a_mat, v * betax, kb * eg])
        for stage in range(6):
            for item in work:
                _, p, ru, rw = item
                pb = p.astype(BF16)
                parts = [block_diag(ru.astype(BF16)), block_diag(rw.astype(BF16))]
                if stage < 5:
                    parts = [block_diag(pb)] + parts
                res = jnp.dot(pb, jnp.concatenate(parts, axis=1), preferred_element_type=F32)
                off = GDN_W if stage < 5 else 0
                du = res[:, off:off + GDN_W]
                dw = res[:, off + GDN_W:]
                item[1] = res[:, :GDN_W] if stage < 5 else None
                item[2] = ru - du if stage == 0 else ru + du
                item[3] = rw - dw if stage == 0 else rw + dw
        for unit, _, ru, rw in work:
            u_s[unit] = ru
            w_s[unit] = rw.astype(BF16)

    per_step = 4 if nch % 4 == 0 else 2

    def solve_step(i, carry):
        solve_units([(direction, per_step * i + j) for j in range(per_step) for direction in range(2)])
        return carry

    lax.fori_loop(0, nch // per_step, solve_step, 0)

    def advance(chunks):
        first = []
        for direction, ch in enumerate(chunks):
            unit = direction * nch + ch
            state = st_s[direction]
            res = jnp.dot(jnp.concatenate([w_s[unit], qg_s[unit]], axis=0), state.astype(BF16),
                          preferred_element_type=F32)
            v_new = (u_s[unit] - res[:CHUNK]).astype(BF16)
            first.append((unit, state, res[CHUNK:], v_new))
        for direction, (unit, state, o_state, v_new) in enumerate(first):
            r0 = pl.multiple_of(chunks[direction] * CHUNK, CHUNK)
            o = o_state + jnp.dot(qk_s[unit], block_diag(v_new), preferred_element_type=F32)
            upd = lax.dot_general(kt_s[unit], v_new, (((0,), (0,)), ((), ())), preferred_element_type=F32)
            st_s[direction] = state * eg_s[unit, 0:1, :] + jnp.where(same_head, upd, 0.0)
            o_s[pl.ds(r0, CHUNK), :] = o_s[pl.ds(r0, CHUNK), :] + o

    def ctx_step(s, carry):
        advance((s, nc - 1 - s))
        return carry

    def lat_step(s, carry):
        advance((nc + s, nch - 1 - s))
        return carry

    lax.fori_loop(0, nc, ctx_step, 0)
    lax.fori_loop(0, nl, lat_step, 0)

    ng = ng_ref[0]

    def rms(s, i):
        return s * lax.rsqrt(_group_mean_square(s) + EPS) * ng[:, i * LANES:(i + 1) * LANES]

    for r0 in range(0, t, rt):
        y = _per_head(rms, o_s[r0:r0 + rt, :]) * _silu(gate_ref[r0:r0 + rt, :])
        out_ref[r0:r0 + rt, :] = y.astype(BF16)


def _gdn(layer, qkv, gate, ab, abt, conv_w, alog_r, dt_r, alog_c, dt_c, ng, dims):
    b, c, s, d, tm, nt = dims
    t = c + s
    n = qkv.shape[0]
    lay = lambda shape: pl.BlockSpec(shape, lambda i: (layer,) + (0,) * (len(shape) - 1))
    nch = t // CHUNK
    assert nch % 2 == 0
    once = pl.Buffered(1)
    return pl.pallas_call(
        functools.partial(_gdn_kernel, c, t, tm),
        grid=(b,),
        in_specs=[pl.BlockSpec((t, P_QKV), lambda i: (i, 0), pipeline_mode=once),
                  pl.BlockSpec((t, P_GATE), lambda i: (i, 0), pipeline_mode=once),
                  pl.BlockSpec((t, P_AB), lambda i: (i, 0)),
                  pl.BlockSpec((16, t), lambda i: (0, i)),
                  lay((1, CONV_K, P_QKV)), lay((1, 1, LANES)), lay((1, 1, LANES)),
                  lay((1, 16, LANES)), lay((1, 16, LANES)), lay((1, 1, GDN_W))],
        out_specs=pl.BlockSpec((t, GDN_W), lambda i: (i, 0)),
        out_shape=jax.ShapeDtypeStruct((n, GDN_W), BF16),
        scratch_shapes=[pltpu.VMEM((t + 2 * PAD_ROWS, P_QKV), F32),
                        pltpu.VMEM((t, GDN_W), F32), pltpu.VMEM((t, GDN_W), F32), pltpu.VMEM((t, GDN_W), F32),
                        pltpu.VMEM((t, LANES), F32), pltpu.VMEM((t, LANES), F32), pltpu.VMEM((t, LANES), F32),
                        pltpu.VMEM((nch, SUBLANES, GDN_W), F32),
                        pltpu.VMEM((2 * nch, CHUNK, GDN_W), F32),
                        pltpu.VMEM((2 * nch, CHUNK, GDN_W), BF16), pltpu.VMEM((2 * nch, CHUNK, GDN_W), BF16),
                        pltpu.VMEM((2 * nch, CHUNK, GDN_W), BF16), pltpu.VMEM((2 * nch, CHUNK, GDN_W), BF16),
                        pltpu.VMEM((2 * nch, SUBLANES, GDN_W), F32),
                        pltpu.VMEM((2, GDN_W, GDN_W), F32),
                        pltpu.VMEM((t, GDN_W), F32)],
        compiler_params=_cparams(("arbitrary",)),
        name="gdn",
    )(qkv, gate, ab, abt, conv_w, alog_r, dt_r, alog_c, dt_c, ng)


def _lru_kernel(c, t, rt, p_ref, cw_ref, cb_ref, wr_ref, br_ref, wi_ref, bi_ref, lam_ref, out_ref,
                xp_ref, a_s, b_s, h_s):
    _fill_padded(xp_ref, p_ref[:, :LRU_W], t)
    cw = cw_ref[0]
    cb = cb_ref[0]
    for r0 in range(0, t, rt):
        xr = _conv_rows(xp_ref, cw, r0, rt, c) + cb
        for dirn in range(2):
            r = _sigmoid(_mm(xr, wr_ref[0, dirn]) + br_ref[0, dirn:dirn + 1, :])
            i = _sigmoid(_mm(xr, wi_ref[0, dirn]) + bi_ref[0, dirn:dirn + 1, :])
            log_a = (-LRU_C) * r * _softplus(-lam_ref[0, dirn:dirn + 1, :])
            a = jnp.exp(log_a)
            a_s[dirn, r0:r0 + rt, :] = a
            b_s[dirn, r0:r0 + rt, :] = jnp.sqrt(1.0 - a * a) * (i * xr)

    row = lax.broadcasted_iota(jnp.int32, (SUBLANES, LRU_W), 0)

    def tile_scan(dirn, i, h_prev):
        r0 = pl.multiple_of(i * SUBLANES, SUBLANES)
        a = a_s[dirn, pl.ds(r0, SUBLANES), :]
        bx = b_s[dirn, pl.ds(r0, SUBLANES), :]
        for sh in (1, 2, 4):
            if dirn == 0:
                a_sh, b_sh, m = pltpu.roll(a, sh, 0), pltpu.roll(bx, sh, 0), row >= sh
            else:
                a_sh, b_sh, m = (pltpu.roll(a, SUBLANES - sh, 0), pltpu.roll(bx, SUBLANES - sh, 0),
                                 row < SUBLANES - sh)
            bx = jnp.where(m, a * b_sh + bx, bx)
            a = jnp.where(m, a * a_sh, a)
        h = a * h_prev + bx
        h_s[dirn, pl.ds(r0, SUBLANES), :] = h
        return h[SUBLANES - 1:SUBLANES, :] if dirn == 0 else h[0:1, :]

    n_t = t // SUBLANES
    n_c = c // SUBLANES
    zero = jnp.zeros((1, LRU_W), F32)

    def fwd_step(i, carry):
        hf, hb = carry
        hf = tile_scan(0, i, hf)
        hb = tile_scan(1, jnp.where(i < n_c, n_c - 1 - i, n_t - 1 - (i - n_c)), hb)
        return hf, hb

    lax.fori_loop(0, n_t, fwd_step, (zero, zero))

    for r0 in range(0, t, rt):
        yb = p_ref[r0:r0 + rt, LRU_W:]
        gelu = 0.5 * yb * (1.0 + jnp.tanh(0.7978845608028654 * (yb + 0.044715 * (yb * yb * yb))))
        out_ref[r0:r0 + rt, :] = ((h_s[0, r0:r0 + rt, :] + h_s[1, r0:r0 + rt, :]) * gelu).astype(BF16)


def _lru(layer, p, conv_w, conv_b, wr, br, wi, bi, lam, dims):
    b, c, s, d, tm, nt = dims
    t = c + s
    n = p.shape[0]
    lay = lambda shape: pl.BlockSpec(shape, lambda i: (layer,) + (0,) * (len(shape) - 1))
    return pl.pallas_call(
        functools.partial(_lru_kernel, c, t, tm),
        grid=(b,),
        in_specs=[pl.BlockSpec((t, P_LRU), lambda i: (i, 0)),
                  lay((1, CONV_K, LRU_W)), lay((1, 1, LRU_W)),
                  lay((1, 2, LRU_W, LRU_W)), lay((1, 2, LRU_W)),
                  lay((1, 2, LRU_W, LRU_W)), lay((1, 2, LRU_W)), lay((1, 2, LRU_W))],
        out_specs=pl.BlockSpec((t, LRU_W), lambda i: (i, 0)),
        out_shape=jax.ShapeDtypeStruct((n, LRU_W), BF16),
        scratch_shapes=[pltpu.VMEM((t + 2 * PAD_ROWS, LRU_W), F32),
                        pltpu.VMEM((2, t, LRU_W), F32), pltpu.VMEM((2, t, LRU_W), F32),
                        pltpu.VMEM((2, t, LRU_W), F32)],
        compiler_params=_cparams(("arbitrary",)),
        name="lru",
    )(p, conv_w, conv_b, wr, br, wi, bi, lam)


def _attn_rows(q, k, v):
    outs = []
    for hq in range(ATT_W // HEAD):
        kv = hq // ATT_GROUP
        qh = q[:, hq * HEAD:(hq + 1) * HEAD]
        kh = k[:, kv * HEAD:(kv + 1) * HEAD]
        vh = v[:, kv * HEAD:(kv + 1) * HEAD]
        s = lax.dot_general(qh, kh, (((1,), (1,)), ((), ())), preferred_element_type=F32)
        m = jnp.max(s, axis=-1, keepdims=True)
        p = jnp.exp(s - m)
        l = jnp.sum(p, axis=-1, keepdims=True)
        o = jnp.dot(p.astype(BF16), vh, preferred_element_type=F32)
        outs.append(o * (1.0 / l))
    return jnp.concatenate(outs, axis=1)


def _attn_kernel(c, with_ctx, q_ref, k_ref, v_ref, o_ref):
    def latent():
        o_ref[...] = _attn_rows(q_ref[...], k_ref[...], v_ref[...]).astype(BF16)

    if not with_ctx:
        latent()
        return
    j = pl.program_id(1)

    @pl.when(j == 0)
    def _():
        o_ref[...] = _attn_rows(q_ref[...], k_ref[0:c, :], v_ref[0:c, :]).astype(BF16)

    pl.when(j > 0)(latent)


def _attn(q, k, v, with_ctx, dims):
    b, c, s, d, tm, nt = dims
    t = c + s
    n = q.shape[0]
    off = 0 if with_ctx else 1
    return pl.pallas_call(
        functools.partial(_attn_kernel, c, with_ctx),
        grid=(b, nt - off),
        in_specs=[pl.BlockSpec((tm, ATT_W), lambda i, j: (i * nt + j + off, 0)),
                  pl.BlockSpec((t, ATT_KV_W), lambda i, j: (i, 0)),
                  pl.BlockSpec((t, ATT_KV_W), lambda i, j: (i, 0))],
        out_specs=pl.BlockSpec((tm, ATT_W), lambda i, j: (i * (nt - off) + j, 0)),
        out_shape=jax.ShapeDtypeStruct((b * (nt - off) * tm, ATT_W), BF16),
        compiler_params=_cparams(("arbitrary", "arbitrary")),
        name="attn",
    )(q, k, v)


R_E1, R_E2, R_G1, R_G2, R_RANK1, R_RANK2 = range(6)


def _outproj_kernel(tm, a_ref, b_ref, c_ref, x_ref, g1_ref, sh_ref, sc_ref, ng_ref, wo_ref, wr_ref, br_ref,
                    xo_ref, h_ref, route_ref, cnt_ref, carry_s):
    i = pl.program_id(0)

    @pl.when(i == 0)
    def _():
        carry_s[...] = jnp.zeros(carry_s.shape, F32)

    mix = (jnp.dot(a_ref[...], wo_ref[0, 0:GDN_W, :], preferred_element_type=F32)
           + jnp.dot(b_ref[...], wo_ref[0, GDN_W:GDN_W + LRU_W, :], preferred_element_type=F32)
           + jnp.dot(c_ref[...], wo_ref[0, GDN_W + LRU_W:, :], preferred_element_type=F32))
    x = x_ref[...] + g1_ref[0] * mix
    xo_ref[...] = x
    ms = jnp.mean(x * x, axis=-1, keepdims=True)
    h = (x * lax.rsqrt(ms + EPS) * ng_ref[0]) * (1.0 + sc_ref[0]) + sh_ref[0]
    h_ref[...] = h

    h_hi = h.astype(BF16)
    h_lo = (h - h_hi.astype(F32)).astype(BF16)
    logits = jnp.dot(jnp.concatenate([h_hi, h_lo, h_hi], axis=1), wr_ref[0],
                     preferred_element_type=F32) + br_ref[0]
    lane_i = lax.broadcasted_iota(jnp.int32, logits.shape, 1)
    lane = lane_i.astype(F32)
    lane_group = jnp.right_shift(lane_i, 3).astype(F32)
    big = jnp.float32(4 * LANES)
    neg = jnp.float32(-jnp.inf)
    is_group = (lane_i >= N_EXPERTS) & (lane_i < N_EXPERTS + N_GROUPS)
    gl = jnp.where(is_group, logits, neg)
    gm = jnp.max(gl, axis=-1, keepdims=True)
    pg_top = 1.0 / jnp.sum(jnp.where(is_group, jnp.exp(gl - gm), 0.0), axis=-1, keepdims=True)
    g_idx = jnp.min(jnp.where(gl == gm, lane, big), axis=-1, keepdims=True) - N_EXPERTS
    in_group = (lane_i < N_EXPERTS) & (lane_group == g_idx)
    le = jnp.where(in_group, logits, neg)
    m1 = jnp.max(le, axis=-1, keepdims=True)
    e1 = jnp.min(jnp.where(le == m1, lane, big), axis=-1, keepdims=True)
    le2 = jnp.where(lane == e1, neg, le)
    m2 = jnp.max(le2, axis=-1, keepdims=True)
    e2 = jnp.min(jnp.where(le2 == m2, lane, big), axis=-1, keepdims=True)
    z = jnp.sum(jnp.where(in_group, jnp.exp(le - m1), 0.0), axis=-1, keepdims=True)
    pe1 = 1.0 / z
    pe2 = jnp.exp(m2 - m1) / z
    gate1 = pg_top * pe1 / (pe1 + pe2)
    gate2 = pg_top * pe2 / (pe1 + pe2)

    sel1 = lane == e1
    sel2 = lane == e2
    onehot = jnp.where(sel1 | sel2, 1.0, 0.0)
    ri = lax.broadcasted_iota(jnp.int32, (tm, tm), 0)
    ci = lax.broadcasted_iota(jnp.int32, (tm, tm), 1)
    before = _mm(jnp.where(ri > ci, 1.0, 0.0), onehot) + carry_s[0:1, :]
    rank1 = jnp.sum(jnp.where(sel1, before, 0.0), axis=-1, keepdims=True)
    rank2 = jnp.sum(jnp.where(sel2, before, 0.0), axis=-1, keepdims=True)
    carry = carry_s[0:1, :] + jnp.sum(onehot, axis=0, keepdims=True)
    carry_s[...] = jnp.broadcast_to(carry, carry_s.shape)
    cnt_ref[...] = jnp.broadcast_to(carry, cnt_ref.shape)

    route = jnp.zeros(logits.shape, F32)
    for col, val in ((R_E1, e1), (R_E2, e2), (R_G1, gate1), (R_G2, gate2),
                     (R_RANK1, rank1), (R_RANK2, rank2)):
        route = jnp.where(lane_i == col, val, route)
    route_ref[...] = route


def _tile_map(with_ctx, nt):
    if with_ctx:
        return lambda i: i
    return lambda i: (i // (nt - 1)) * nt + 1 + i % (nt - 1)


def _outproj(layer, a, bm, cm, x, mod, norm2_g, w_out, w_route, b_route, with_ctx, dims):
    b, c, s, d, tm, nt = dims
    n = x.shape[0]
    n_layers = norm2_g.shape[0]
    rows = mod.shape[0] // (6 * n_layers)
    tile = _tile_map(with_ctx, nt)
    n_tiles = b * (nt if with_ctx else nt - 1)

    def mod_idx(chunk):
        def f(i):
            ti = tile(i)
            r = jnp.where(ti % nt == 0, b, ti // nt)
            return ((layer * 6 + chunk) * rows + r, 0, 0)
        return f

    row_spec = lambda w: pl.BlockSpec((tm, w), lambda i: (tile(i), 0))
    compact = lambda w: pl.BlockSpec((tm, w), lambda i: (i, 0))
    lay = lambda shape: pl.BlockSpec(shape, lambda i: (layer,) + (0,) * (len(shape) - 1))
    n_moe = n_tiles * tm
    return pl.pallas_call(
        functools.partial(_outproj_kernel, tm),
        grid=(n_tiles,),
        in_specs=[row_spec(GDN_W), row_spec(LRU_W), compact(ATT_W), row_spec(d),
                  pl.BlockSpec((1, 1, d), mod_idx(2)), pl.BlockSpec((1, 1, d), mod_idx(3)),
                  pl.BlockSpec((1, 1, d), mod_idx(4)),
                  lay((1, 1, d)), lay((1, d, d)), lay((1, 3 * d, LANES)), lay((1, 1, LANES))],
        out_specs=[row_spec(d), compact(d), compact(LANES),
                   pl.BlockSpec((SUBLANES, LANES), lambda i: (0, 0))],
        out_shape=[jax.ShapeDtypeStruct((n, d), F32), jax.ShapeDtypeStruct((n_moe, d), F32),
                   jax.ShapeDtypeStruct((n_moe, LANES), F32), jax.ShapeDtypeStruct((SUBLANES, LANES), F32)],
        scratch_shapes=[pltpu.VMEM((SUBLANES, LANES), F32)],
        input_output_aliases={3: 0},
        compiler_params=_cparams(("arbitrary",)),
        name="outproj",
    )(a, bm, cm, x, mod, mod, mod, norm2_g, w_out, w_route, b_route)


def _dest_kernel(route_ref, cnt_ref, dest_ref):
    route = route_ref[...]
    lane = lax.broadcasted_iota(jnp.int32, route.shape, 1)
    counts = cnt_ref[...]
    padded = jnp.floor((counts + (MOE_BLOCK - 1)) * (1.0 / MOE_BLOCK)) * MOE_BLOCK
    li = lax.broadcasted_iota(jnp.int32, (LANES, LANES), 0)
    lj = lax.broadcasted_iota(jnp.int32, (LANES, LANES), 1)
    start = _mm_exact(padded, jnp.where((li < lj) & (li < N_EXPERTS), 1.0, 0.0))[0:1, :]

    def col(j):
        return jnp.sum(jnp.where(lane == j, route, 0.0), axis=-1, keepdims=True)

    def slot(e, rank):
        return jnp.sum(jnp.where(lane == e.astype(jnp.int32), start, 0.0), axis=-1, keepdims=True) + rank

    d1 = slot(col(R_E1), col(R_RANK1))
    d2 = slot(col(R_E2), col(R_RANK2))
    dest_ref[...] = jnp.where(lane == 0, d1, jnp.where(lane == 1, d2, 0.0)).astype(jnp.int32)


def _dest(route, counts, dims):
    tm = dims[4]
    n = route.shape[0]
    return pl.pallas_call(
        _dest_kernel,
        grid=(n // tm,),
        in_specs=[pl.BlockSpec((tm, LANES), lambda i: (i, 0)),
                  pl.BlockSpec((SUBLANES, LANES), lambda i: (0, 0))],
        out_specs=pl.BlockSpec((tm, LANES), lambda i: (i, 0)),
        out_shape=jax.ShapeDtypeStruct((n, LANES), jnp.int32),
        compiler_params=_cparams(("arbitrary",)),
        name="dest",
    )(route, counts)


def _invert_kernel(n, d1_ref, d2_ref, slot_ref):
    def clear(s, carry):
        slot_ref[s] = 0
        return carry

    lax.fori_loop(0, slot_ref.shape[0], clear, 0, unroll=8)

    def place(t, carry):
        slot_ref[d1_ref[t]] = t
        slot_ref[d2_ref[t]] = n + t
        return carry

    lax.fori_loop(0, n, place, 0, unroll=8)


def _invert(d1, d2, n_slots):
    n = d1.shape[0]
    return pl.pallas_call(
        functools.partial(_invert_kernel, n),
        grid_spec=pltpu.PrefetchScalarGridSpec(
            num_scalar_prefetch=2,
            grid=(1,),
            in_specs=[],
            out_specs=pl.BlockSpec(memory_space=pltpu.SMEM)),
        out_shape=jax.ShapeDtypeStruct((n_slots,), jnp.int32),
        compiler_params=_cparams(("arbitrary",)),
        name="invert",
    )(d1, d2)


ROW_GROUPS = MOE_BLOCK // SUBLANES


def _expert_kernel(be_ref, nact_ref, valid_ref, slot_ref, h_ref, w1_ref, w3_ref, w2_ref, y_ref,
                   w1_s, w3_s, w2_s, x_s, y_s, gsem, ssem):
    i = pl.program_id(0)
    nact = nact_ref[0]

    def for_rows(blk, fn):
        count = valid_ref[blk]
        groups = count // SUBLANES

        def group(g, carry):
            base = blk * MOE_BLOCK + g * SUBLANES
            for u in range(SUBLANES):
                fn(slot_ref[base + u], g, u)
            return carry

        def single(u, carry):
            fn(slot_ref[blk * MOE_BLOCK + groups * SUBLANES + u], groups, u)
            return carry

        lax.fori_loop(0, groups, group, 0)
        lax.fori_loop(0, count - groups * SUBLANES, single, 0)

    def gather(blk):
        xb = x_s.at[blk % 2]
        sem = gsem.at[blk % 2]
        n = h_ref.shape[0]
        for_rows(blk, lambda dst, g, u: pltpu.make_async_copy(
            h_ref.at[pl.ds(jnp.where(dst >= n, dst - n, dst), 1)], xb.at[g, pl.ds(u, 1)], sem).start())

    def scatter(blk):
        yb = y_s.at[blk % 2]
        sem = ssem.at[blk % 2]
        for_rows(blk, lambda dst, g, u: pltpu.make_async_copy(
            yb.at[g, pl.ds(u, 1)], y_ref.at[pl.ds(dst, 1)], sem).start())

    def wait_rows(blk, buf, sem):
        count = valid_ref[blk]
        groups = count // SUBLANES
        pl.when(groups > 0)(lambda: pltpu.make_async_copy(
            buf.at[pl.ds(0, groups)], buf.at[pl.ds(0, groups)], sem).wait())

        def single(u, carry):
            pltpu.make_async_copy(buf.at[0, pl.ds(0, 1)], buf.at[0, pl.ds(0, 1)], sem).wait()
            return carry

        lax.fori_loop(0, count - groups * SUBLANES, single, 0)

    @pl.when(i == 0)
    def _():
        x_s[...] = jnp.zeros(x_s.shape, F32)
        pl.when(nact > 0)(lambda: gather(0))

    pl.when(i + 1 < nact)(lambda: gather(i + 1))
    pl.when((i >= 2) & (i - 2 < nact))(lambda: wait_rows(i - 2, y_s.at[i % 2], ssem.at[i % 2]))

    @pl.when(i < nact)
    def _():
        @pl.when((i == 0) | (be_ref[i] != be_ref[jnp.maximum(i - 1, 0)]))
        def _():
            w1_s[...] = w1_ref[0, 0].astype(BF16)
            w3_s[...] = w3_ref[0, 0].astype(BF16)
            w2_s[...] = w2_ref[0, 0].astype(BF16)

        buf = i % 2
        wait_rows(i, x_s.at[buf], gsem.at[buf])
        d = x_s.shape[-1]
        x = x_s[buf].reshape(MOE_BLOCK, d).astype(BF16)
        h1 = jnp.dot(x, w1_s[...], preferred_element_type=F32)
        h3 = jnp.dot(x, w3_s[...], preferred_element_type=F32)
        act = (_silu(h1) * h3).astype(BF16)
        y = jnp.dot(act, w2_s[...], preferred_element_type=F32)
        y_s[buf] = y.reshape(ROW_GROUPS, SUBLANES, d)
        scatter(i)


def _experts(layer, block_e, nact, valid, slot_src, h, w1, w3, w2):
    n, d = h.shape
    hid = w1.shape[-1]
    n_steps = block_e.shape[0]
    wspec = lambda shape: pl.BlockSpec(shape, lambda i, be, na, va, sl: (layer, be[i], 0, 0))
    return pl.pallas_call(
        _expert_kernel,
        grid_spec=pltpu.PrefetchScalarGridSpec(
            num_scalar_prefetch=4,
            grid=(n_steps,),
            in_specs=[pl.BlockSpec(memory_space=pl.ANY),
                      wspec((1, 1, d, hid)), wspec((1, 1, d, hid)), wspec((1, 1, hid, d))],
            out_specs=pl.BlockSpec(memory_space=pl.ANY),
            scratch_shapes=[pltpu.VMEM((d, hid), BF16), pltpu.VMEM((d, hid), BF16),
                            pltpu.VMEM((hid, d), BF16),
                            pltpu.VMEM((2, ROW_GROUPS, SUBLANES, d), F32),
                            pltpu.VMEM((2, ROW_GROUPS, SUBLANES, d), F32),
                            pltpu.SemaphoreType.DMA((2,)), pltpu.SemaphoreType.DMA((2,))]),
        out_shape=jax.ShapeDtypeStruct((2 * n, d), F32),
        compiler_params=_cparams(("arbitrary",)),
        name="experts",
    )(block_e, nact, valid, slot_src, h, w1, w3, w2)


def _combine_kernel(final, x_ref, route_ref, g2_ref, fg_ref, y1_ref, y2_ref, o_ref):
    route = route_ref[...]
    lane = lax.broadcasted_iota(jnp.int32, route.shape, 1)
    gate1 = jnp.sum(jnp.where(lane == R_G1, route, 0.0), axis=-1, keepdims=True)
    gate2 = jnp.sum(jnp.where(lane == R_G2, route, 0.0), axis=-1, keepdims=True)
    x = x_ref[...] + g2_ref[0] * (y1_ref[...] * gate1 + y2_ref[...] * gate2)
    if final:
        ms = jnp.mean(x * x, axis=-1, keepdims=True)
        x = x * lax.rsqrt(ms + EPS) * fg_ref[...]
    o_ref[...] = x


def _combine(layer, x, route, mod, final_g, y, with_ctx, final, n_layers, dims):
    b, c, s, d, tm, nt = dims
    n = x.shape[0]
    rows = mod.shape[0] // (6 * n_layers)
    tile = _tile_map(with_ctx, nt)
    n_tiles = b * (nt if with_ctx else nt - 1)

    def mod_idx(i):
        ti = tile(i)
        r = jnp.where(ti % nt == 0, b, ti // nt)
        return ((layer * 6 + 5) * rows + r, 0, 0)

    if final:
        out_spec = pl.BlockSpec((tm, d), lambda i: (i, 0))
        out_shape = jax.ShapeDtypeStruct((n_tiles * tm, d), F32)
        aliases = {}
    else:
        out_spec = pl.BlockSpec((tm, d), lambda i: (tile(i), 0))
        out_shape = jax.ShapeDtypeStruct((n, d), F32)
        aliases = {0: 0}
    return pl.pallas_call(
        functools.partial(_combine_kernel, final),
        grid=(n_tiles,),
        in_specs=[pl.BlockSpec((tm, d), lambda i: (tile(i), 0)),
                  pl.BlockSpec((tm, LANES), lambda i: (i, 0)),
                  pl.BlockSpec((1, 1, d), mod_idx),
                  pl.BlockSpec((1, d), lambda i: (0, 0)),
                  pl.BlockSpec((tm, d), lambda i: (i, 0)),
                  pl.BlockSpec((tm, d), lambda i: (n_tiles + i, 0))],
        out_specs=out_spec,
        out_shape=out_shape,
        input_output_aliases=aliases,
        compiler_params=_cparams(("arbitrary",)),
        name="combine",
    )(x, route, mod, final_g, y, y)


def _rope_tables(s, tm):
    rows = s // GRID_W
    row = jnp.repeat(jnp.arange(rows, dtype=F32), GRID_W)
    col = jnp.tile(jnp.arange(GRID_W, dtype=F32), rows)
    axis_dim = HEAD // 2
    inv_freq = ROPE_THETA ** (-jnp.arange(0, axis_dim, 2, dtype=F32) / axis_dim)
    ar = row[:, None] * inv_freq
    ac = col[:, None] * inv_freq
    cos = jnp.concatenate([jnp.cos(ar), jnp.cos(ar), jnp.cos(ac), jnp.cos(ac)], axis=1)
    sin = jnp.concatenate([-jnp.sin(ar), jnp.sin(ar), -jnp.sin(ac), jnp.sin(ac)], axis=1)
    cos = jnp.concatenate([jnp.ones((tm, HEAD), F32), cos], axis=0)
    sin = jnp.concatenate([jnp.zeros((tm, HEAD), F32), sin], axis=0)
    return jnp.tile(cos, (1, 2)), jnp.tile(sin, (1, 2))


def _block_diag(w):
    n_layers = w.shape[0]
    eye = jnp.eye(LRU_BLOCKS, dtype=w.dtype)
    full = jnp.einsum('ldnij,nm->ldnimj', w, eye)
    return full.reshape(n_layers, 2, LRU_W, LRU_W)


def _pad_lanes(a, width=LANES):
    return jnp.pad(a, [(0, 0)] * (a.ndim - 1) + [(0, width - a.shape[-1])])


def kernel(x, c, ctx, c_ctx, w_ada, b_ada, norm1_g, norm2_g, w_in, w_out, gdn_conv_w, gdn_a_log, gdn_dt_bias, gdn_norm_g, lru_conv_w, lru_conv_b, lru_w_r, lru_b_r, lru_w_i, lru_b_i, lru_lambda, attn_q_norm_g, attn_k_norm_g, moe_w_group, moe_b_group, moe_w_expert, moe_b_expert, moe_w1, moe_w3, moe_w2, final_norm_g):
    bsz, s, d = x.shape
    cl = ctx.shape[1]
    n_layers = w_in.shape[0]
    tm = cl
    assert s % tm == 0 and tm % CHUNK == 0 and s % GRID_W == 0
    nt = (cl + s) // tm
    dims = (bsz, cl, s, d, tm, nt)

    o1 = 4 * GDN_W + 4 * GDN_H
    o2 = o1 + 2 * LRU_W
    w_ab = w_in[:, :, 4 * GDN_W:o1]
    w_pack = jnp.concatenate([w_in[:, :, :4 * GDN_W], _pad_lanes(w_ab), w_in[:, :, o1:o2], w_in[:, :, o2:]],
                             axis=-1).astype(BF16)
    wabt = jnp.swapaxes(w_ab, 1, 2).astype(BF16)
    gq = jnp.tile(attn_q_norm_g, (1, 2))[:, None, :]
    gk = jnp.tile(attn_k_norm_g, (1, 2))[:, None, :]
    cos_t, sin_t = _rope_tables(s, tm)
    alog = gdn_a_log.reshape(n_layers, 2 * GDN_H)
    dtb = gdn_dt_bias.reshape(n_layers, 2 * GDN_H)
    alog_r = _pad_lanes(alog)[:, None, :]
    dt_r = _pad_lanes(dtb)[:, None, :]
    alog_c = jnp.broadcast_to(_pad_lanes(alog, 16)[:, :, None], (n_layers, 16, LANES))
    dt_c = jnp.broadcast_to(_pad_lanes(dtb, 16)[:, :, None], (n_layers, 16, LANES))
    gdn_ng = jnp.tile(gdn_norm_g, (1, GDN_H))[:, None, :]
    wr_bd = _block_diag(lru_w_r).astype(BF16)
    wi_bd = _block_diag(lru_w_i).astype(BF16)
    w_out_b = w_out.astype(BF16)
    w_route = _pad_lanes(jnp.concatenate([moe_w_expert, moe_w_group], axis=-1))
    w_route_hi = w_route.astype(BF16)
    w_route_lo = (w_route - w_route_hi.astype(F32)).astype(BF16)
    w_route = jnp.concatenate([w_route_hi, w_route_hi, w_route_lo], axis=1)
    b_route = _pad_lanes(jnp.concatenate([moe_b_expert, moe_b_group], axis=-1))[:, None, :]

    cv = jnp.concatenate([c, c_ctx[None, :]], axis=0)
    rows = -(-cv.shape[0] // SUBLANES) * SUBLANES
    cv = jnp.pad(cv, ((0, rows - cv.shape[0]), (0, 0)))
    mod = _adaln(cv, w_ada, b_ada).reshape(n_layers * 6 * rows, 1, d)

    xf = jnp.concatenate([ctx, x], axis=1).reshape(bsz * (cl + s), d)
    out = None
    for layer in range(n_layers):
        with_ctx = layer < n_layers - 1
        qkv, gate, ab, abt, plru, q, k, v = _inproj(
            layer, xf, mod, norm1_g[:, None, :], w_pack, wabt, gq, gk, cos_t, sin_t, dims)
        mix_a = _gdn(layer, qkv, gate, ab, abt, gdn_conv_w, alog_r, dt_r, alog_c, dt_c, gdn_ng, dims)
        mix_b = _lru(layer, plru, lru_conv_w, lru_conv_b[:, None, :], wr_bd, lru_b_r, wi_bd, lru_b_i,
                     lru_lambda, dims)
        mix_c = _attn(q, k, v, with_ctx, dims)
        xf, h, route, counts = _outproj(layer, mix_a, mix_b, mix_c, xf, mod, norm2_g[:, None, :], w_out_b,
                                        w_route, b_route, with_ctx, dims)
        dest = _dest(route, counts, dims)
        d1 = dest[:, 0]
        d2 = dest[:, 1]
        n_tok = bsz * ((cl + s) if with_ctx else s)
        n_blocks = -(-(2 * n_tok) // MOE_BLOCK) + N_EXPERTS
        cnt = counts[0, :N_EXPERTS].astype(jnp.int32)
        padded = (cnt + MOE_BLOCK - 1) // MOE_BLOCK * MOE_BLOCK
        pad_end = jnp.cumsum(padded)
        block_row = jnp.arange(n_blocks + 2, dtype=jnp.int32) * MOE_BLOCK
        block_e = jnp.minimum(jnp.sum((pad_end[None, :] <= block_row[:, None]).astype(jnp.int32), axis=1),
                              N_EXPERTS - 1)
        filled_end = (pad_end - padded + cnt)[block_e]
        valid = jnp.where(block_row < pad_end[-1], jnp.clip(filled_end - block_row, 0, MOE_BLOCK), 0)
        nact = (pad_end[-1:] // MOE_BLOCK).astype(jnp.int32)
        slot_src = _invert(d1, d2, n_blocks * MOE_BLOCK)
        y = _experts(layer, block_e, nact, valid.astype(jnp.int32), slot_src, h, moe_w1, moe_w3, moe_w2)
        res = _combine(layer, xf, route, mod, final_norm_g[None, :], y, with_ctx, not with_ctx, n_layers, dims)
        if with_ctx:
            xf = res
        else:
            out = res
    return out.reshape(bsz, s, d)
```

```python
import functools

import jax
import jax.numpy as jnp
from jax import lax
from jax.experimental import pallas as pl
from jax.experimental.pallas import tpu as pltpu

F32 = jnp.float32
BF16 = jnp.bfloat16
HIGHEST = lax.Precision.HIGHEST

HEAD = 64
GDN_W = 256
GDN_H = GDN_W // HEAD
CHUNK = 64
LRU_W = 256
LRU_BLOCKS = 4
LRU_C = 8.0
ATT_W = 512
ATT_KV_W = 128
ATT_GROUP = 4
CONV_K = 4
N_GROUPS = 8
N_EXPERTS = 64
EXPERTS_PER_GROUP = 8
MOE_HIDDEN = 512
MOE_BLOCK = 128
GRID_W = 64
ROPE_THETA = 10000.0
EPS = 1e-6
LANES = 128
SUBLANES = 8
PAD_ROWS = 8
VMEM_LIMIT = 56 * 1024 * 1024

P_QKV = 3 * GDN_W
P_GATE = GDN_W
P_AB = LANES
P_LRU = 2 * LRU_W
P_ATT = ATT_W + 2 * ATT_KV_W
P_ALL = P_QKV + P_GATE + P_AB + P_LRU + P_ATT


def _cparams(sem):
    return pltpu.CompilerParams(dimension_semantics=sem, vmem_limit_bytes=VMEM_LIMIT)


def _sigmoid(x):
    return 1.0 / (1.0 + jnp.exp(-x))


def _silu(x):
    return x * _sigmoid(x)


def _softplus(x):
    return jnp.maximum(x, 0.0) + jnp.log(1.0 + jnp.exp(-jnp.abs(x)))


def _mm(a, b):
    return jnp.dot(a.astype(BF16), b.astype(BF16), preferred_element_type=F32)


def _mm_nt(a, b):
    return lax.dot_general(a.astype(BF16), b.astype(BF16), (((1,), (1,)), ((), ())),
                           preferred_element_type=F32)


def _mm_tn(a, b):
    return lax.dot_general(a.astype(BF16), b.astype(BF16), (((0,), (0,)), ((), ())),
                           preferred_element_type=F32)


def _mm_exact(a, b):
    return jnp.dot(a, b, precision=HIGHEST, preferred_element_type=F32)


def _group_mean_square(x):
    lane = lax.broadcasted_iota(jnp.int32, x.shape, 1)
    lo = lane < HEAD
    x2 = x * x
    s_lo = jnp.sum(jnp.where(lo, x2, 0.0), axis=-1, keepdims=True)
    s_hi = jnp.sum(jnp.where(lo, 0.0, x2), axis=-1, keepdims=True)
    return jnp.where(lo, s_lo, s_hi) * (1.0 / HEAD)


def _per_head(fn, x):
    n = x.shape[1] // LANES
    return jnp.concatenate([fn(x[:, i * LANES:(i + 1) * LANES], i) for i in range(n)], axis=1)


def _adaln_kernel(cv_ref, w_ref, b_ref, o_ref):
    o_ref[0] = _mm_exact(_silu(cv_ref[...]), w_ref[0]) + b_ref[0]


def _adaln(cv, w_ada, b_ada):
    n_layers, d, _ = w_ada.shape
    rows = cv.shape[0]
    return pl.pallas_call(
        _adaln_kernel,
        grid=(n_layers, 6),
        in_specs=[pl.BlockSpec((rows, d), lambda l, j: (0, 0)),
                  pl.BlockSpec((1, d, d), lambda l, j: (l, 0, j)),
                  pl.BlockSpec((1, 1, d), lambda l, j: (l * 6 + j, 0, 0))],
        out_specs=pl.BlockSpec((1, rows, d), lambda l, j: (l * 6 + j, 0, 0)),
        out_shape=jax.ShapeDtypeStruct((n_layers * 6, rows, d), F32),
        compiler_params=_cparams(("arbitrary", "arbitrary")),
        name="adaln",
    )(cv, w_ada, b_ada.reshape(n_layers * 6, 1, d))


def _inproj_kernel(x_ref, sh_ref, sc_ref, g_ref, w_ref, wabt_ref, gq_ref, gk_ref, cos_ref, sin_ref,
                   qkv_ref, gate_ref, ab_ref, abt_ref, lru_ref, q_ref, k_ref, v_ref):
    x = x_ref[...]
    ms = jnp.mean(x * x, axis=-1, keepdims=True)
    h = (x * lax.rsqrt(ms + EPS) * g_ref[0]) * (1.0 + sc_ref[0]) + sh_ref[0]
    hb = h.astype(BF16)
    p = jnp.dot(hb, w_ref[0], preferred_element_type=F32)
    o = 0
    qkv_ref[...] = p[:, o:o + P_QKV]
    o += P_QKV
    gate_ref[...] = p[:, o:o + P_GATE]
    o += P_GATE
    ab_ref[...] = p[:, o:o + P_AB]
    o += P_AB
    lru_ref[...] = p[:, o:o + P_LRU]
    o += P_LRU
    att = p[:, o:o + P_ATT]
    abt_ref[...] = lax.dot_general(wabt_ref[0], hb, (((1,), (1,)), ((), ())), preferred_element_type=F32)

    cos = cos_ref[...]
    sin = sin_ref[...]
    lane = lax.broadcasted_iota(jnp.int32, cos.shape, 1)
    first_half = (lane & 16) == 0

    def norm_rope(gain):
        def fn(s, _):
            y = s * lax.rsqrt(_group_mean_square(s) + EPS) * gain
            swapped = jnp.where(first_half, pltpu.roll(y, LANES - 16, 1), pltpu.roll(y, 16, 1))
            return y * cos + swapped * sin
        return fn

    q = _per_head(norm_rope(gq_ref[0]), att[:, :ATT_W])
    k = _per_head(norm_rope(gk_ref[0]), att[:, ATT_W:ATT_W + ATT_KV_W])
    q_ref[...] = (q * (HEAD ** -0.5)).astype(BF16)
    k_ref[...] = k.astype(BF16)
    v_ref[...] = att[:, ATT_W + ATT_KV_W:].astype(BF16)


def _inproj(layer, x, mod, norm1_g, w_pack, wabt, gq, gk, cos_t, sin_t, dims):
    b, c, s, d, tm, nt = dims
    n = x.shape[0]
    rows = mod.shape[0] // (6 * norm1_g.shape[0])

    def mod_idx(chunk):
        def f(i):
            r = jnp.where(i % nt == 0, b, i // nt)
            return ((layer * 6 + chunk) * rows + r, 0, 0)
        return f

    row_spec = lambda w: pl.BlockSpec((tm, w), lambda i: (i, 0))
    return pl.pallas_call(
        _inproj_kernel,
        grid=(n // tm,),
        in_specs=[row_spec(d),
                  pl.BlockSpec((1, 1, d), mod_idx(0)),
                  pl.BlockSpec((1, 1, d), mod_idx(1)),
                  pl.BlockSpec((1, 1, d), lambda i: (layer, 0, 0)),
                  pl.BlockSpec((1, d, P_ALL), lambda i: (layer, 0, 0)),
                  pl.BlockSpec((1, 16, d), lambda i: (layer, 0, 0)),
                  pl.BlockSpec((1, 1, LANES), lambda i: (layer, 0, 0)),
                  pl.BlockSpec((1, 1, LANES), lambda i: (layer, 0, 0)),
                  pl.BlockSpec((tm, LANES), lambda i: (i % nt, 0)),
                  pl.BlockSpec((tm, LANES), lambda i: (i % nt, 0))],
        out_specs=[row_spec(P_QKV), row_spec(P_GATE), row_spec(P_AB),
                   pl.BlockSpec((16, tm), lambda i: (0, i)),
                   row_spec(P_LRU), row_spec(ATT_W), row_spec(ATT_KV_W), row_spec(ATT_KV_W)],
        out_shape=[jax.ShapeDtypeStruct((n, P_QKV), F32), jax.ShapeDtypeStruct((n, P_GATE), F32),
                   jax.ShapeDtypeStruct((n, P_AB), F32), jax.ShapeDtypeStruct((16, n), F32),
                   jax.ShapeDtypeStruct((n, P_LRU), F32), jax.ShapeDtypeStruct((n, ATT_W), BF16),
                   jax.ShapeDtypeStruct((n, ATT_KV_W), BF16), jax.ShapeDtypeStruct((n, ATT_KV_W), BF16)],
        compiler_params=_cparams(("arbitrary",)),
        name="inproj",
    )(x, mod, mod, norm1_g, w_pack, wabt, gq, gk, cos_t, sin_t)


def _conv_rows(xp_ref, w, r0, rt, c):
    base = r0 + PAD_ROWS
    xm1 = xp_ref[base - 1:base - 1 + rt, :]
    x0 = xp_ref[base:base + rt, :]
    xp1 = xp_ref[base + 1:base + 1 + rt, :]
    xp2 = xp_ref[base + 2:base + 2 + rt, :]
    row = r0 + lax.broadcasted_iota(jnp.int32, (rt, 1), 0)
    xm1 = jnp.where(row == c, 0.0, xm1)
    xp1 = jnp.where(row == c - 1, 0.0, xp1)
    xp2 = jnp.where((row == c - 1) | (row == c - 2), 0.0, xp2)
    return w[0:1, :] * xm1 + w[1:2, :] * x0 + w[2:3, :] * xp1 + w[3:4, :] * xp2


def _fill_padded(xp_ref, x, t):
    zeros = jnp.zeros((PAD_ROWS, xp_ref.shape[1]), F32)
    xp_ref[0:PAD_ROWS, :] = zeros
    xp_ref[PAD_ROWS + t:2 * PAD_ROWS + t, :] = zeros
    xp_ref[PAD_ROWS:PAD_ROWS + t, :] = x


def _gdn_kernel(c, t, rt, qkv_ref, gate_ref, ab_ref, abt_ref, cw_ref, alog_r_ref, dt_r_ref, alog_c_ref,
                dt_c_ref, ng_ref, out_ref,
                xp_ref, q_s, k_s, v_s, g_s, gcf_s, gcb_s, grp_s, u_s, w_s, qk_s, qg_s, kt_s, eg_s, st_s, o_s):
    nch = t // CHUNK
    nc = c // CHUNK
    nl = nch - nc

    _fill_padded(xp_ref, qkv_ref[...], t)
    cw = cw_ref[0]

    def l2n(s, _):
        return s * lax.rsqrt(_group_mean_square(s) * HEAD + EPS)

    for r0 in range(0, t, rt):
        y = _silu(_conv_rows(xp_ref, cw, r0, rt, c))
        q_s[r0:r0 + rt, :] = _per_head(l2n, y[:, :GDN_W]) * (HEAD ** -0.5)
        k_s[r0:r0 + rt, :] = _per_head(l2n, y[:, GDN_W:2 * GDN_W])
        v_s[r0:r0 + rt, :] = y[:, 2 * GDN_W:]

    ab = ab_ref[...]
    lane = lax.broadcasted_iota(jnp.int32, ab.shape, 1)
    gval = -jnp.exp(alog_r_ref[0]) * _softplus(ab + dt_r_ref[0])
    g_s[...] = jnp.where(lane < 2 * GDN_H, gval, _sigmoid(ab))
    abt = abt_ref[...]
    g_t = -jnp.exp(alog_c_ref[0][:, 0:1]) * _softplus(abt + dt_c_ref[0][:, 0:1])

    ri = lax.broadcasted_iota(jnp.int32, (CHUNK, CHUNK), 0)
    ci = lax.broadcasted_iota(jnp.int32, (CHUNK, CHUNK), 1)
    lower_f = (ri >= ci).astype(F32)
    upper_f = (ri <= ci).astype(F32)
    row8 = lax.broadcasted_iota(jnp.int32, (2 * GDN_H, CHUNK), 0)
    for ch in range(nch):
        r0 = ch * CHUNK
        gch = g_s[r0:r0 + CHUNK, :]
        gcf_s[r0:r0 + CHUNK, :] = _mm_exact(lower_f, gch)
        gcb_s[r0:r0 + CHUNK, :] = _mm_exact(upper_f, gch)
        gtc = g_t[0:2 * GDN_H, r0:r0 + CHUNK]
        gr = jnp.where(row8 < GDN_H, _mm_exact(gtc, upper_f), _mm_exact(gtc, lower_f))
        for direction in range(2):
            rows = [gr[direction * GDN_H + hd:direction * GDN_H + hd + 1, :] for hd in range(GDN_H)]
            edge = CHUNK - 1 if direction == 0 else 0
            tots = [jnp.broadcast_to(r[:, edge:edge + 1], (1, CHUNK)) for r in rows]
            grp_s[ch, direction:direction + 1, :] = jnp.concatenate(rows, axis=1)
            grp_s[ch, 2 + direction:3 + direction, :] = jnp.concatenate(tots, axis=1)

    st_s[...] = jnp.zeros(st_s.shape, F32)
    o_s[...] = jnp.zeros(o_s.shape, F32)

    ri4 = lax.broadcasted_iota(jnp.int32, (CHUNK, GDN_W), 0)
    li4 = lax.broadcasted_iota(jnp.int32, (CHUNK, GDN_W), 1)
    pos4 = li4 & (CHUNK - 1)
    head_masks = [jnp.where((li4 >= hd * HEAD) & (li4 < (hd + 1) * HEAD), 1.0, 0.0).astype(BF16)
                  for hd in range(GDN_H)]
    lane_lo = lax.broadcasted_iota(jnp.int32, (CHUNK, LANES), 1) < HEAD
    bi = jnp.right_shift(lax.broadcasted_iota(jnp.int32, (GDN_W, GDN_W), 0), 6)
    bj = jnp.right_shift(lax.broadcasted_iota(jnp.int32, (GDN_W, GDN_W), 1), 6)
    same_head = bi == bj

    def block_diag(xb):
        return jnp.concatenate([xb * m for m in head_masks], axis=0)

    def expand4(x, base):
        cols = [jnp.broadcast_to(x[:, base + hd:base + hd + 1], (CHUNK, LANES)) for hd in range(GDN_H)]
        return jnp.concatenate([jnp.where(lane_lo, cols[0], cols[1]), jnp.where(lane_lo, cols[2], cols[3])],
                               axis=1)

    def solve_units(units):
        work = []
        for direction, ch in units:
            r0 = pl.multiple_of(ch * CHUNK, CHUNK)
            unit = direction * nch + ch
            q = q_s[pl.ds(r0, CHUNK), :]
            k = k_s[pl.ds(r0, CHUNK), :]
            v = v_s[pl.ds(r0, CHUNK), :]
            gcx = expand4((gcf_s if direction == 0 else gcb_s)[pl.ds(r0, CHUNK), :], direction * GDN_H)
            betax = expand4(g_s[pl.ds(r0, CHUNK), :], 2 * GDN_H + direction * GDN_H)
            grow = grp_s[ch, direction:direction + 1, :]
            gtot = grp_s[ch, 2 + direction:3 + direction, :]
            incl = (ri4 >= pos4) if direction == 0 else (ri4 <= pos4)
            strict = (ri4 > pos4) if direction == 0 else (ri4 < pos4)
            decay = jnp.exp(jnp.minimum(gcx - grow, 0.0))
            eg = jnp.exp(gcx)
            kb = k * betax
            prod = lax.dot_general(jnp.concatenate([q, kb], axis=0).astype(BF16), block_diag(k.astype(BF16)),
                                   (((1,), (1,)), ((), ())), preferred_element_type=F32)
            qk_s[unit] = jnp.where(incl, prod[:CHUNK] * decay, 0.0).astype(BF16)
            qg_s[unit] = (q * eg).astype(BF16)
            kt_s[unit] = (k * jnp.exp(gtot - gcx)).astype(BF16)
            eg_s[unit] = jnp.broadcast_to(jnp.exp(gtot), (SUBLANES, GDN_W))
            a_mat = jnp.where(strict, prod[CHUNK:] * decay, 0.0)
            work.append([unit, a_mat, v * betax, kb * eg])
        for stage in range(6):
            for item in work:
                _, p, ru, rw = item
                pb = p.astype(BF16)
                parts = [block_diag(ru.astype(BF16)), block_diag(rw.astype(BF16))]
                if stage < 5:
                    parts = [block_diag(pb)] + parts
                res = jnp.dot(pb, jnp.concatenate(parts, axis=1), preferred_element_type=F32)
                off = GDN_W if stage < 5 else 0
                du = res[:, off:off + GDN_W]
                dw = res[:, off + GDN_W:]
                item[1] = res[:, :GDN_W] if stage < 5 else None
                item[2] = ru - du if stage == 0 else ru + du
                item[3] = rw - dw if stage == 0 else rw + dw
        for unit, _, ru, rw in work:
            u_s[unit] = ru
            w_s[unit] = rw.astype(BF16)

    per_step = 4 if nch % 4 == 0 else 2

    def solve_step(i, carry):
        solve_units([(direction, per_step * i + j) for j in range(per_step) for direction in range(2)])
        return carry

    lax.fori_loop(0, nch // per_step, solve_step, 0)

    def advance(chunks):
        first = []
        for direction, ch in enumerate(chunks):
            unit = direction * nch + ch
            state = st_s[direction]
            res = jnp.dot(jnp.concatenate([w_s[unit], qg_s[unit]], axis=0), state.astype(BF16),
                          preferred_element_type=F32)
            v_new = (u_s[unit] - res[:CHUNK]).astype(BF16)
            first.append((unit, state, res[CHUNK:], v_new))
        for direction, (unit, state, o_state, v_new) in enumerate(first):
            r0 = pl.multiple_of(chunks[direction] * CHUNK, CHUNK)
            o = o_state + jnp.dot(qk_s[unit], block_diag(v_new), preferred_element_type=F32)
            upd = lax.dot_general(kt_s[unit], v_new, (((0,), (0,)), ((), ())), preferred_element_type=F32)
            st_s[direction] = state * eg_s[unit, 0:1, :] + jnp.where(same_head, upd, 0.0)
            o_s[pl.ds(r0, CHUNK), :] = o_s[pl.ds(r0, CHUNK), :] + o

    def ctx_step(s, carry):
        advance((s, nc - 1 - s))
        return carry

    def lat_step(s, carry):
        advance((nc + s, nch - 1 - s))
        return carry

    lax.fori_loop(0, nc, ctx_step, 0)
    lax.fori_loop(0, nl, lat_step, 0)

    ng = ng_ref[0]

    def rms(s, i):
        return s * lax.rsqrt(_group_mean_square(s) + EPS) * ng[:, i * LANES:(i + 1) * LANES]

    for r0 in range(0, t, rt):
        y = _per_head(rms, o_s[r0:r0 + rt, :]) * _silu(gate_ref[r0:r0 + rt, :])
        out_ref[r0:r0 + rt, :] = y.astype(BF16)


def _gdn(layer, qkv, gate, ab, abt, conv_w, alog_r, dt_r, alog_c, dt_c, ng, dims):
    b, c, s, d, tm, nt = dims
    t = c + s
    n = qkv.shape[0]
    lay = lambda shape: pl.BlockSpec(shape, lambda i: (layer,) + (0,) * (len(shape) - 1))
    nch = t // CHUNK
    assert nch % 2 == 0
    once = pl.Buffered(1)
    return pl.pallas_call(
        functools.partial(_gdn_kernel, c, t, tm),
        grid=(b,),
        in_specs=[pl.BlockSpec((t, P_QKV), lambda i: (i, 0), pipeline_mode=once),
                  pl.BlockSpec((t, P_GATE), lambda i: (i, 0), pipeline_mode=once),
                  pl.BlockSpec((t, P_AB), lambda i: (i, 0)),
                  pl.BlockSpec((16, t), lambda i: (0, i)),
                  lay((1, CONV_K, P_QKV)), lay((1, 1, LANES)), lay((1, 1, LANES)),
                  lay((1, 16, LANES)), lay((1, 16, LANES)), lay((1, 1, GDN_W))],
        out_specs=pl.BlockSpec((t, GDN_W), lambda i: (i, 0)),
        out_shape=jax.ShapeDtypeStruct((n, GDN_W), BF16),
        scratch_shapes=[pltpu.VMEM((t + 2 * PAD_ROWS, P_QKV), F32),
                        pltpu.VMEM((t, GDN_W), F32), pltpu.VMEM((t, GDN_W), F32), pltpu.VMEM((t, GDN_W), F32),
                        pltpu.VMEM((t, LANES), F32), pltpu.VMEM((t, LANES), F32), pltpu.VMEM((t, LANES), F32),
                        pltpu.VMEM((nch, SUBLANES, GDN_W), F32),
                        pltpu.VMEM((2 * nch, CHUNK, GDN_W), F32),
                        pltpu.VMEM((2 * nch, CHUNK, GDN_W), BF16), pltpu.VMEM((2 * nch, CHUNK, GDN_W), BF16),
                        pltpu.VMEM((2 * nch, CHUNK, GDN_W), BF16), pltpu.VMEM((2 * nch, CHUNK, GDN_W), BF16),
                        pltpu.VMEM((2 * nch, SUBLANES, GDN_W), F32),
                        pltpu.VMEM((2, GDN_W, GDN_W), F32),
                        pltpu.VMEM((t, GDN_W), F32)],
        compiler_params=_cparams(("arbitrary",)),
        name="gdn",
    )(qkv, gate, ab, abt, conv_w, alog_r, dt_r, alog_c, dt_c, ng)


def _lru_kernel(c, t, rt, p_ref, cw_ref, cb_ref, wr_ref, br_ref, wi_ref, bi_ref, lam_ref, out_ref,
                xp_ref, a_s, b_s, h_s):
    _fill_padded(xp_ref, p_ref[:, :LRU_W], t)
    cw = cw_ref[0]
    cb = cb_ref[0]
    for r0 in range(0, t, rt):
        xr = _conv_rows(xp_ref, cw, r0, rt, c) + cb
        for dirn in range(2):
            r = _sigmoid(_mm(xr, wr_ref[0, dirn]) + br_ref[0, dirn:dirn + 1, :])
            i = _sigmoid(_mm(xr, wi_ref[0, dirn]) + bi_ref[0, dirn:dirn + 1, :])
            log_a = (-LRU_C) * r * _softplus(-lam_ref[0, dirn:dirn + 1, :])
            a = jnp.exp(log_a)
            a_s[dirn, r0:r0 + rt, :] = a
            b_s[dirn, r0:r0 + rt, :] = jnp.sqrt(1.0 - a * a) * (i * xr)

    row = lax.broadcasted_iota(jnp.int32, (SUBLANES, LRU_W), 0)

    def tile_scan(dirn, i, h_prev):
        r0 = pl.multiple_of(i * SUBLANES, SUBLANES)
        a = a_s[dirn, pl.ds(r0, SUBLANES), :]
        bx = b_s[dirn, pl.ds(r0, SUBLANES), :]
        for sh in (1, 2, 4):
            if dirn == 0:
                a_sh, b_sh, m = pltpu.roll(a, sh, 0), pltpu.roll(bx, sh, 0), row >= sh
            else:
                a_sh, b_sh, m = (pltpu.roll(a, SUBLANES - sh, 0), pltpu.roll(bx, SUBLANES - sh, 0),
                                 row < SUBLANES - sh)
            bx = jnp.where(m, a * b_sh + bx, bx)
            a = jnp.where(m, a * a_sh, a)
        h = a * h_prev + bx
        h_s[dirn, pl.ds(r0, SUBLANES), :] = h
        return h[SUBLANES - 1:SUBLANES, :] if dirn == 0 else h[0:1, :]

    n_t = t // SUBLANES
    n_c = c // SUBLANES
    zero = jnp.zeros((1, LRU_W), F32)

    def fwd_step(i, carry):
        hf, hb = carry
        hf = tile_scan(0, i, hf)
        hb = tile_scan(1, jnp.where(i < n_c, n_c - 1 - i, n_t - 1 - (i - n_c)), hb)
        return hf, hb

    lax.fori_loop(0, n_t, fwd_step, (zero, zero))

    for r0 in range(0, t, rt):
        yb = p_ref[r0:r0 + rt, LRU_W:]
        gelu = 0.5 * yb * (1.0 + jnp.tanh(0.7978845608028654 * (yb + 0.044715 * (yb * yb * yb))))
        out_ref[r0:r0 + rt, :] = ((h_s[0, r0:r0 + rt, :] + h_s[1, r0:r0 + rt, :]) * gelu).astype(BF16)


def _lru(layer, p, conv_w, conv_b, wr, br, wi, bi, lam, dims):
    b, c, s, d, tm, nt = dims
    t = c + s
    n = p.shape[0]
    lay = lambda shape: pl.BlockSpec(shape, lambda i: (layer,) + (0,) * (len(shape) - 1))
    return pl.pallas_call(
        functools.partial(_lru_kernel, c, t, tm),
        grid=(b,),
        in_specs=[pl.BlockSpec((t, P_LRU), lambda i: (i, 0)),
                  lay((1, CONV_K, LRU_W)), lay((1, 1, LRU_W)),
                  lay((1, 2, LRU_W, LRU_W)), lay((1, 2, LRU_W)),
                  lay((1, 2, LRU_W, LRU_W)), lay((1, 2, LRU_W)), lay((1, 2, LRU_W))],
        out_specs=pl.BlockSpec((t, LRU_W), lambda i: (i, 0)),
        out_shape=jax.ShapeDtypeStruct((n, LRU_W), BF16),
        scratch_shapes=[pltpu.VMEM((t + 2 * PAD_ROWS, LRU_W), F32),
                        pltpu.VMEM((2, t, LRU_W), F32), pltpu.VMEM((2, t, LRU_W), F32),
                        pltpu.VMEM((2, t, LRU_W), F32)],
        compiler_params=_cparams(("arbitrary",)),
        name="lru",
    )(p, conv_w, conv_b, wr, br, wi, bi, lam)


def _attn_rows(q, k, v):
    outs = []
    for hq in range(ATT_W // HEAD):
        kv = hq // ATT_GROUP
        qh = q[:, hq * HEAD:(hq + 1) * HEAD]
        kh = k[:, kv * HEAD:(kv + 1) * HEAD]
        vh = v[:, kv * HEAD:(kv + 1) * HEAD]
        s = lax.dot_general(qh, kh, (((1,), (1,)), ((), ())), preferred_element_type=F32)
        m = jnp.max(s, axis=-1, keepdims=True)
        p = jnp.exp(s - m)
        l = jnp.sum(p, axis=-1, keepdims=True)
        o = jnp.dot(p.astype(BF16), vh, preferred_element_type=F32)
        outs.append(o * (1.0 / l))
    return jnp.concatenate(outs, axis=1)


def _attn_kernel(c, with_ctx, q_ref, k_ref, v_ref, o_ref):
    def latent():
        o_ref[...] = _attn_rows(q_ref[...], k_ref[...], v_ref[...]).astype(BF16)

    if not with_ctx:
        latent()
        return
    j = pl.program_id(1)

    @pl.when(j == 0)
    def _():
        o_ref[...] = _attn_rows(q_ref[...], k_ref[0:c, :], v_ref[0:c, :]).astype(BF16)

    pl.when(j > 0)(latent)


def _attn(q, k, v, with_ctx, dims):
    b, c, s, d, tm, nt = dims
    t = c + s
    n = q.shape[0]
    off = 0 if with_ctx else 1
    return pl.pallas_call(
        functools.partial(_attn_kernel, c, with_ctx),
        grid=(b, nt - off),
        in_specs=[pl.BlockSpec((tm, ATT_W), lambda i, j: (i * nt + j + off, 0)),
                  pl.BlockSpec((t, ATT_KV_W), lambda i, j: (i, 0)),
                  pl.BlockSpec((t, ATT_KV_W), lambda i, j: (i, 0))],
        out_specs=pl.BlockSpec((tm, ATT_W), lambda i, j: (i * (nt - off) + j, 0)),
        out_shape=jax.ShapeDtypeStruct((b * (nt - off) * tm, ATT_W), BF16),
        compiler_params=_cparams(("arbitrary", "arbitrary")),
        name="attn",
    )(q, k, v)


R_E1, R_E2, R_G1, R_G2, R_RANK1, R_RANK2 = range(6)


def _outproj_kernel(tm, a_ref, b_ref, c_ref, x_ref, g1_ref, sh_ref, sc_ref, ng_ref, wo_ref, wr_ref, br_ref,
                    xo_ref, h_ref, route_ref, cnt_ref, carry_s):
    i = pl.program_id(0)

    @pl.when(i == 0)
    def _():
        carry_s[...] = jnp.zeros(carry_s.shape, F32)

    mix = (jnp.dot(a_ref[...], wo_ref[0, 0:GDN_W, :], preferred_element_type=F32)
           + jnp.dot(b_ref[...], wo_ref[0, GDN_W:GDN_W + LRU_W, :], preferred_element_type=F32)
           + jnp.dot(c_ref[...], wo_ref[0, GDN_W + LRU_W:, :], preferred_element_type=F32))
    x = x_ref[...] + g1_ref[0] * mix
    xo_ref[...] = x
    ms = jnp.mean(x * x, axis=-1, keepdims=True)
    h = (x * lax.rsqrt(ms + EPS) * ng_ref[0]) * (1.0 + sc_ref[0]) + sh_ref[0]
    h_ref[...] = h

    h_hi = h.astype(BF16)
    h_lo = (h - h_hi.astype(F32)).astype(BF16)
    logits = jnp.dot(jnp.concatenate([h_hi, h_lo, h_hi], axis=1), wr_ref[0],
                     preferred_element_type=F32) + br_ref[0]
    lane_i = lax.broadcasted_iota(jnp.int32, logits.shape, 1)
    lane = lane_i.astype(F32)
    lane_group = jnp.right_shift(lane_i, 3).astype(F32)
    big = jnp.float32(4 * LANES)
    neg = jnp.float32(-jnp.inf)
    is_group = (lane_i >= N_EXPERTS) & (lane_i < N_EXPERTS + N_GROUPS)
    gl = jnp.where(is_group, logits, neg)
    gm = jnp.max(gl, axis=-1, keepdims=True)
    pg_top = 1.0 / jnp.sum(jnp.where(is_group, jnp.exp(gl - gm), 0.0), axis=-1, keepdims=True)
    g_idx = jnp.min(jnp.where(gl == gm, lane, big), axis=-1, keepdims=True) - N_EXPERTS
    in_group = (lane_i < N_EXPERTS) & (lane_group == g_idx)
    le = jnp.where(in_group, logits, neg)
    m1 = jnp.max(le, axis=-1, keepdims=True)
    e1 = jnp.min(jnp.where(le == m1, lane, big), axis=-1, keepdims=True)
    le2 = jnp.where(lane == e1, neg, le)
    m2 = jnp.max(le2, axis=-1, keepdims=True)
    e2 = jnp.min(jnp.where(le2 == m2, lane, big), axis=-1, keepdims=True)
    z = jnp.sum(jnp.where(in_group, jnp.exp(le - m1), 0.0), axis=-1, keepdims=True)
    pe1 = 1.0 / z
    pe2 = jnp.exp(m2 - m1) / z
    gate1 = pg_top * pe1 / (pe1 + pe2)
    gate2 = pg_top * pe2 / (pe1 + pe2)

    sel1 = lane == e1
    sel2 = lane == e2
    onehot = jnp.where(sel1 | sel2, 1.0, 0.0)
    ri = lax.broadcasted_iota(jnp.int32, (tm, tm), 0)
    ci = lax.broadcasted_iota(jnp.int32, (tm, tm), 1)
    before = _mm(jnp.where(ri > ci, 1.0, 0.0), onehot) + carry_s[0:1, :]
    rank1 = jnp.sum(jnp.where(sel1, before, 0.0), axis=-1, keepdims=True)
    rank2 = jnp.sum(jnp.where(sel2, before, 0.0), axis=-1, keepdims=True)
    carry = carry_s[0:1, :] + jnp.sum(onehot, axis=0, keepdims=True)
    carry_s[...] = jnp.broadcast_to(carry, carry_s.shape)
    cnt_ref[...] = jnp.broadcast_to(carry, cnt_ref.shape)

    route = jnp.zeros(logits.shape, F32)
    for col, val in ((R_E1, e1), (R_E2, e2), (R_G1, gate1), (R_G2, gate2),
                     (R_RANK1, rank1), (R_RANK2, rank2)):
        route = jnp.where(lane_i == col, val, route)
    route_ref[...] = route


def _tile_map(with_ctx, nt):
    if with_ctx:
        return lambda i: i
    return lambda i: (i // (nt - 1)) * nt + 1 + i % (nt - 1)


def _outproj(layer, a, bm, cm, x, mod, norm2_g, w_out, w_route, b_route, with_ctx, dims):
    b, c, s, d, tm, nt = dims
    n = x.shape[0]
    n_layers = norm2_g.shape[0]
    rows = mod.shape[0] // (6 * n_layers)
    tile = _tile_map(with_ctx, nt)
    n_tiles = b * (nt if with_ctx else nt - 1)

    def mod_idx(chunk):
        def f(i):
            ti = tile(i)
            r = jnp.where(ti % nt == 0, b, ti // nt)
            return ((layer * 6 + chunk) * rows + r, 0, 0)
        return f

    row_spec = lambda w: pl.BlockSpec((tm, w), lambda i: (tile(i), 0))
    compact = lambda w: pl.BlockSpec((tm, w), lambda i: (i, 0))
    lay = lambda shape: pl.BlockSpec(shape, lambda i: (layer,) + (0,) * (len(shape) - 1))
    n_moe = n_tiles * tm
    return pl.pallas_call(
        functools.partial(_outproj_kernel, tm),
        grid=(n_tiles,),
        in_specs=[row_spec(GDN_W), row_spec(LRU_W), compact(ATT_W), row_spec(d),
                  pl.BlockSpec((1, 1, d), mod_idx(2)), pl.BlockSpec((1, 1, d), mod_idx(3)),
                  pl.BlockSpec((1, 1, d), mod_idx(4)),
                  lay((1, 1, d)), lay((1, d, d)), lay((1, 3 * d, LANES)), lay((1, 1, LANES))],
        out_specs=[row_spec(d), compact(d), compact(LANES),
                   pl.BlockSpec((SUBLANES, LANES), lambda i: (0, 0))],
        out_shape=[jax.ShapeDtypeStruct((n, d), F32), jax.ShapeDtypeStruct((n_moe, d), F32),
                   jax.ShapeDtypeStruct((n_moe, LANES), F32), jax.ShapeDtypeStruct((SUBLANES, LANES), F32)],
        scratch_shapes=[pltpu.VMEM((SUBLANES, LANES), F32)],
        input_output_aliases={3: 0},
        compiler_params=_cparams(("arbitrary",)),
        name="outproj",
    )(a, bm, cm, x, mod, mod, mod, norm2_g, w_out, w_route, b_route)


def _dest_kernel(route_ref, cnt_ref, dest_ref):
    route = route_ref[...]
    lane = lax.broadcasted_iota(jnp.int32, route.shape, 1)
    counts = cnt_ref[...]
    padded = jnp.floor((counts + (MOE_BLOCK - 1)) * (1.0 / MOE_BLOCK)) * MOE_BLOCK
    li = lax.broadcasted_iota(jnp.int32, (LANES, LANES), 0)
    lj = lax.broadcasted_iota(jnp.int32, (LANES, LANES), 1)
    start = _mm_exact(padded, jnp.where((li < lj) & (li < N_EXPERTS), 1.0, 0.0))[0:1, :]

    def col(j):
        return jnp.sum(jnp.where(lane == j, route, 0.0), axis=-1, keepdims=True)

    def slot(e, rank):
        return jnp.sum(jnp.where(lane == e.astype(jnp.int32), start, 0.0), axis=-1, keepdims=True) + rank

    d1 = slot(col(R_E1), col(R_RANK1))
    d2 = slot(col(R_E2), col(R_RANK2))
    dest_ref[...] = jnp.where(lane == 0, d1, jnp.where(lane == 1, d2, 0.0)).astype(jnp.int32)


def _dest(route, counts, dims):
    tm = dims[4]
    n = route.shape[0]
    return pl.pallas_call(
        _dest_kernel,
        grid=(n // tm,),
        in_specs=[pl.BlockSpec((tm, LANES), lambda i: (i, 0)),
                  pl.BlockSpec((SUBLANES, LANES), lambda i: (0, 0))],
        out_specs=pl.BlockSpec((tm, LANES), lambda i: (i, 0)),
        out_shape=jax.ShapeDtypeStruct((n, LANES), jnp.int32),
        compiler_params=_cparams(("arbitrary",)),
        name="dest",
    )(route, counts)


def _invert_kernel(n, d1_ref, d2_ref, slot_ref):
    def clear(s, carry):
        slot_ref[s] = 2 * n + (s & (SPARE_ROWS - 1))
        return carry

    lax.fori_loop(0, slot_ref.shape[0], clear, 0, unroll=8)

    def place(t, carry):
        slot_ref[d1_ref[t]] = t
        slot_ref[d2_ref[t]] = n + t
        return carry

    lax.fori_loop(0, n, place, 0, unroll=8)


def _invert(d1, d2, n_slots):
    n = d1.shape[0]
    return pl.pallas_call(
        functools.partial(_invert_kernel, n),
        grid_spec=pltpu.PrefetchScalarGridSpec(
            num_scalar_prefetch=2,
            grid=(1,),
            in_specs=[],
            out_specs=pl.BlockSpec(memory_space=pltpu.SMEM)),
        out_shape=jax.ShapeDtypeStruct((n_slots,), jnp.int32),
        compiler_params=_cparams(("arbitrary",)),
        name="invert",
    )(d1, d2)


SPARE_ROWS = 2 * MOE_BLOCK


def _expert_kernel(be_ref, nact_ref, slot_ref, h_ref, w1_ref, w3_ref, w2_ref, y_ref,
                   w1_s, w3_s, w2_s, x_s, y_s, gsem, ssem):
    i = pl.program_id(0)
    nact = nact_ref[0]
    n = h_ref.shape[0]
    d = x_s.shape[-1]
    quarter = MOE_BLOCK // 4

    def gather_rows(blk, buf, rows):
        for r in rows:
            dst = slot_ref[blk * MOE_BLOCK + r]
            tok = jnp.minimum(jnp.where(dst >= n, dst - n, dst), n - 1)
            pltpu.make_async_copy(h_ref.at[pl.ds(tok, 1)], x_s.at[buf, pl.ds(r, 1)], gsem.at[buf]).start()

    def scatter_rows(blk, buf, rows):
        for r in rows:
            dst = slot_ref[blk * MOE_BLOCK + r]
            pltpu.make_async_copy(y_s.at[buf, pl.ds(r, 1)], y_ref.at[pl.ds(dst, 1)], ssem.at[buf]).start()

    def wait_block(bufs, sem, buf):
        pltpu.make_async_copy(bufs.at[buf], bufs.at[buf], sem.at[buf]).wait()

    def compute(buf, between):
        x = x_s[buf].astype(BF16)
        between[0]()
        h1 = jnp.dot(x, w1_s[...], preferred_element_type=F32)
        between[1]()
        h3 = jnp.dot(x, w3_s[...], preferred_element_type=F32)
        between[2]()
        act = (_silu(h1) * h3).astype(BF16)
        y = jnp.dot(act, w2_s[...], preferred_element_type=F32)
        between[3]()
        return y

    @pl.when(i == 0)
    def _():
        y_s[...] = jnp.zeros(y_s.shape, F32)
        for buf in range(2):
            spare = pltpu.make_async_copy(y_s.at[buf], y_ref.at[pl.ds(2 * n + buf * MOE_BLOCK, MOE_BLOCK)],
                                          ssem.at[buf])
            spare.start()
            spare.wait()

    @pl.when((i < nact) & ((i == 0) | (be_ref[i] != be_ref[jnp.maximum(i - 1, 0)])))
    def _():
        w1_s[...] = w1_ref[0, 0].astype(BF16)
        w3_s[...] = w3_ref[0, 0].astype(BF16)
        w2_s[...] = w2_ref[0, 0].astype(BF16)

    steady = (i >= 2) & (i + 1 < nact)
    for par in range(2):
        @pl.when(steady & (i % 2 == par))
        def _(par=par):
            wait_block(x_s, gsem, par)

            def part(j):
                def issue():
                    rows = range(j * quarter, (j + 1) * quarter)
                    gather_rows(i + 1, 1 - par, rows)
                    scatter_rows(i - 1, 1 - par, rows)
                return issue

            y = compute(par, [part(j) for j in range(4)])
            wait_block(y_s, ssem, par)
            y_s[par] = y

    @pl.when(jnp.logical_not(steady))
    def _():
        buf = i % 2
        pl.when((i == 0) & (nact > 0))(lambda: gather_rows(0, 0, range(MOE_BLOCK)))
        pl.when(i + 1 < nact)(lambda: gather_rows(i + 1, 1 - buf, range(MOE_BLOCK)))
        pl.when((i >= 1) & (i - 1 < nact))(lambda: scatter_rows(i - 1, 1 - buf, range(MOE_BLOCK)))
        pl.when((i >= 2) & (i - 2 < nact))(lambda: wait_block(y_s, ssem, buf))

        @pl.when(i < nact)
        def _():
            wait_block(x_s, gsem, buf)
            y_s[buf] = compute(buf, [lambda: None] * 4)


def _experts(layer, block_e, nact, slot_src, h, w1, w3, w2):
    n, d = h.shape
    hid = w1.shape[-1]
    n_steps = block_e.shape[0]
    n_slots = slot_src.shape[0]
    wspec = lambda shape: pl.BlockSpec(shape, lambda i, be, na, sl: (layer, be[i], 0, 0))
    return pl.pallas_call(
        _expert_kernel,
        grid_spec=pltpu.PrefetchScalarGridSpec(
            num_scalar_prefetch=3,
            grid=(n_steps,),
            in_specs=[pl.BlockSpec(memory_space=pl.ANY),
                      wspec((1, 1, d, hid)), wspec((1, 1, d, hid)), wspec((1, 1, hid, d))],
            out_specs=pl.BlockSpec(memory_space=pl.ANY),
            scratch_shapes=[pltpu.VMEM((d, hid), BF16), pltpu.VMEM((d, hid), BF16),
                            pltpu.VMEM((hid, d), BF16),
                            pltpu.VMEM((2, MOE_BLOCK, d), F32), pltpu.VMEM((2, MOE_BLOCK, d), F32),
                            pltpu.SemaphoreType.DMA((2,)), pltpu.SemaphoreType.DMA((2,))]),
        out_shape=jax.ShapeDtypeStruct((2 * n + SPARE_ROWS, d), F32),
        compiler_params=_cparams(("arbitrary",)),
        name="experts",
    )(block_e, nact, slot_src, h, w1, w3, w2)


def _combine_kernel(final, x_ref, route_ref, g2_ref, fg_ref, y1_ref, y2_ref, o_ref):
    route = route_ref[...]
    lane = lax.broadcasted_iota(jnp.int32, route.shape, 1)
    gate1 = jnp.sum(jnp.where(lane == R_G1, route, 0.0), axis=-1, keepdims=True)
    gate2 = jnp.sum(jnp.where(lane == R_G2, route, 0.0), axis=-1, keepdims=True)
    x = x_ref[...] + g2_ref[0] * (y1_ref[...] * gate1 + y2_ref[...] * gate2)
    if final:
        ms = jnp.mean(x * x, axis=-1, keepdims=True)
        x = x * lax.rsqrt(ms + EPS) * fg_ref[...]
    o_ref[...] = x


def _combine(layer, x, route, mod, final_g, y, with_ctx, final, n_layers, dims):
    b, c, s, d, tm, nt = dims
    n = x.shape[0]
    rows = mod.shape[0] // (6 * n_layers)
    tile = _tile_map(with_ctx, nt)
    n_tiles = b * (nt if with_ctx else nt - 1)

    def mod_idx(i):
        ti = tile(i)
        r = jnp.where(ti % nt == 0, b, ti // nt)
        return ((layer * 6 + 5) * rows + r, 0, 0)

    if final:
        out_spec = pl.BlockSpec((tm, d), lambda i: (i, 0))
        out_shape = jax.ShapeDtypeStruct((n_tiles * tm, d), F32)
        aliases = {}
    else:
        out_spec = pl.BlockSpec((tm, d), lambda i: (tile(i), 0))
        out_shape = jax.ShapeDtypeStruct((n, d), F32)
        aliases = {0: 0}
    return pl.pallas_call(
        functools.partial(_combine_kernel, final),
        grid=(n_tiles,),
        in_specs=[pl.BlockSpec((tm, d), lambda i: (tile(i), 0)),
                  pl.BlockSpec((tm, LANES), lambda i: (i, 0)),
                  pl.BlockSpec((1, 1, d), mod_idx),
                  pl.BlockSpec((1, d), lambda i: (0, 0)),
                  pl.BlockSpec((tm, d), lambda i: (i, 0)),
                  pl.BlockSpec((tm, d), lambda i: (n_tiles + i, 0))],
        out_specs=out_spec,
        out_shape=out_shape,
        input_output_aliases=aliases,
        compiler_params=_cparams(("arbitrary",)),
        name="combine",
    )(x, route, mod, final_g, y, y)


def _rope_tables(s, tm):
    rows = s // GRID_W
    row = jnp.repeat(jnp.arange(rows, dtype=F32), GRID_W)
    col = jnp.tile(jnp.arange(GRID_W, dtype=F32), rows)
    axis_dim = HEAD // 2
    inv_freq = ROPE_THETA ** (-jnp.arange(0, axis_dim, 2, dtype=F32) / axis_dim)
    ar = row[:, None] * inv_freq
    ac = col[:, None] * inv_freq
    cos = jnp.concatenate([jnp.cos(ar), jnp.cos(ar), jnp.cos(ac), jnp.cos(ac)], axis=1)
    sin = jnp.concatenate([-jnp.sin(ar), jnp.sin(ar), -jnp.sin(ac), jnp.sin(ac)], axis=1)
    cos = jnp.concatenate([jnp.ones((tm, HEAD), F32), cos], axis=0)
    sin = jnp.concatenate([jnp.zeros((tm, HEAD), F32), sin], axis=0)
    return jnp.tile(cos, (1, 2)), jnp.tile(sin, (1, 2))


def _block_diag(w):
    n_layers = w.shape[0]
    eye = jnp.eye(LRU_BLOCKS, dtype=w.dtype)
    full = jnp.einsum('ldnij,nm->ldnimj', w, eye)
    return full.reshape(n_layers, 2, LRU_W, LRU_W)


def _pad_lanes(a, width=LANES):
    return jnp.pad(a, [(0, 0)] * (a.ndim - 1) + [(0, width - a.shape[-1])])


def kernel(x, c, ctx, c_ctx, w_ada, b_ada, norm1_g, norm2_g, w_in, w_out, gdn_conv_w, gdn_a_log, gdn_dt_bias, gdn_norm_g, lru_conv_w, lru_conv_b, lru_w_r, lru_b_r, lru_w_i, lru_b_i, lru_lambda, attn_q_norm_g, attn_k_norm_g, moe_w_group, moe_b_group, moe_w_expert, moe_b_expert, moe_w1, moe_w3, moe_w2, final_norm_g):
    bsz, s, d = x.shape
    cl = ctx.shape[1]
    n_layers = w_in.shape[0]
    tm = cl
    assert s % tm == 0 and tm % CHUNK == 0 and s % GRID_W == 0
    nt = (cl + s) // tm
    dims = (bsz, cl, s, d, tm, nt)

    o1 = 4 * GDN_W + 4 * GDN_H
    o2 = o1 + 2 * LRU_W
    w_ab = w_in[:, :, 4 * GDN_W:o1]
    w_pack = jnp.concatenate([w_in[:, :, :4 * GDN_W], _pad_lanes(w_ab), w_in[:, :, o1:o2], w_in[:, :, o2:]],
                             axis=-1).astype(BF16)
    wabt = jnp.swapaxes(w_ab, 1, 2).astype(BF16)
    gq = jnp.tile(attn_q_norm_g, (1, 2))[:, None, :]
    gk = jnp.tile(attn_k_norm_g, (1, 2))[:, None, :]
    cos_t, sin_t = _rope_tables(s, tm)
    alog = gdn_a_log.reshape(n_layers, 2 * GDN_H)
    dtb = gdn_dt_bias.reshape(n_layers, 2 * GDN_H)
    alog_r = _pad_lanes(alog)[:, None, :]
    dt_r = _pad_lanes(dtb)[:, None, :]
    alog_c = jnp.broadcast_to(_pad_lanes(alog, 16)[:, :, None], (n_layers, 16, LANES))
    dt_c = jnp.broadcast_to(_pad_lanes(dtb, 16)[:, :, None], (n_layers, 16, LANES))
    gdn_ng = jnp.tile(gdn_norm_g, (1, GDN_H))[:, None, :]
    wr_bd = _block_diag(lru_w_r).astype(BF16)
    wi_bd = _block_diag(lru_w_i).astype(BF16)
    w_out_b = w_out.astype(BF16)
    w_route = _pad_lanes(jnp.concatenate([moe_w_expert, moe_w_group], axis=-1))
    w_route_hi = w_route.astype(BF16)
    w_route_lo = (w_route - w_route_hi.astype(F32)).astype(BF16)
    w_route = jnp.concatenate([w_route_hi, w_route_hi, w_route_lo], axis=1)
    b_route = _pad_lanes(jnp.concatenate([moe_b_expert, moe_b_group], axis=-1))[:, None, :]

    cv = jnp.concatenate([c, c_ctx[None, :]], axis=0)
    rows = -(-cv.shape[0] // SUBLANES) * SUBLANES
    cv = jnp.pad(cv, ((0, rows - cv.shape[0]), (0, 0)))
    mod = _adaln(cv, w_ada, b_ada).reshape(n_layers * 6 * rows, 1, d)

    xf = jnp.concatenate([ctx, x], axis=1).reshape(bsz * (cl + s), d)
    out = None
    for layer in range(n_layers):
        with_ctx = layer < n_layers - 1
        qkv, gate, ab, abt, plru, q, k, v = _inproj(
            layer, xf, mod, norm1_g[:, None, :], w_pack, wabt, gq, gk, cos_t, sin_t, dims)
        mix_a = _gdn(layer, qkv, gate, ab, abt, gdn_conv_w, alog_r, dt_r, alog_c, dt_c, gdn_ng, dims)
        mix_b = _lru(layer, plru, lru_conv_w, lru_conv_b[:, None, :], wr_bd, lru_b_r, wi_bd, lru_b_i,
                     lru_lambda, dims)
        mix_c = _attn(q, k, v, with_ctx, dims)
        xf, h, route, counts = _outproj(layer, mix_a, mix_b, mix_c, xf, mod, norm2_g[:, None, :], w_out_b,
                                        w_route, b_route, with_ctx, dims)
        dest = _dest(route, counts, dims)
        d1 = dest[:, 0]
        d2 = dest[:, 1]
        n_tok = bsz * ((cl + s) if with_ctx else s)
        n_blocks = -(-(2 * n_tok) // MOE_BLOCK) + N_EXPERTS
        cnt = counts[0, :N_EXPERTS].astype(jnp.int32)
        padded = (cnt + MOE_BLOCK - 1) // MOE_BLOCK * MOE_BLOCK
        pad_end = jnp.cumsum(padded)
        block_row = jnp.arange(n_blocks + 2, dtype=jnp.int32) * MOE_BLOCK
        block_e = jnp.minimum(jnp.sum((pad_end[None, :] <= block_row[:, None]).astype(jnp.int32), axis=1),
                              N_EXPERTS - 1)
        nact = (pad_end[-1:] // MOE_BLOCK).astype(jnp.int32)
        slot_src = _invert(d1, d2, n_blocks * MOE_BLOCK)
        y = _experts(layer, block_e, nact, slot_src, h, moe_w1, moe_w3, moe_w2)
        res = _combine(layer, xf, route, mod, final_norm_g[None, :], y, with_ctx, not with_ctx, n_layers, dims)
        if with_ctx:
            xf = res
        else:
            out = res
    return out.reshape(bsz, s, d)
```

```python
import functools

import jax
import jax.numpy as jnp
from jax import lax
from jax.experimental import pallas as pl
from jax.experimental.pallas import tpu as pltpu

F32 = jnp.float32
BF16 = jnp.bfloat16
HIGHEST = lax.Precision.HIGHEST

HEAD = 64
GDN_W = 256
GDN_H = GDN_W // HEAD
CHUNK = 64
LRU_W = 256
LRU_BLOCKS = 4
LRU_C = 8.0
ATT_W = 512
ATT_KV_W = 128
ATT_GROUP = 4
CONV_K = 4
N_GROUPS = 8
N_EXPERTS = 64
EXPERTS_PER_GROUP = 8
MOE_HIDDEN = 512
MOE_BLOCK = 128
GRID_W = 64
ROPE_THETA = 10000.0
EPS = 1e-6
LANES = 128
SUBLANES = 8
PAD_ROWS = 8
VMEM_LIMIT = 56 * 1024 * 1024

P_QKV = 3 * GDN_W
P_GATE = GDN_W
P_AB = LANES
P_LRU = 2 * LRU_W
P_ATT = ATT_W + 2 * ATT_KV_W
P_ALL = P_QKV + P_GATE + P_AB + P_LRU + P_ATT


def _cparams(sem):
    return pltpu.CompilerParams(dimension_semantics=sem, vmem_limit_bytes=VMEM_LIMIT)


def _sigmoid(x):
    return 1.0 / (1.0 + jnp.exp(-x))


def _silu(x):
    return x * _sigmoid(x)


def _softplus(x):
    return jnp.maximum(x, 0.0) + jnp.log(1.0 + jnp.exp(-jnp.abs(x)))


def _mm(a, b):
    return jnp.dot(a.astype(BF16), b.astype(BF16), preferred_element_type=F32)


def _mm_nt(a, b):
    return lax.dot_general(a.astype(BF16), b.astype(BF16), (((1,), (1,)), ((), ())),
                           preferred_element_type=F32)


def _mm_tn(a, b):
    return lax.dot_general(a.astype(BF16), b.astype(BF16), (((0,), (0,)), ((), ())),
                           preferred_element_type=F32)


def _mm_exact(a, b):
    return jnp.dot(a, b, precision=HIGHEST, preferred_element_type=F32)


def _group_mean_square(x):
    lane = lax.broadcasted_iota(jnp.int32, x.shape, 1)
    lo = lane < HEAD
    x2 = x * x
    s_lo = jnp.sum(jnp.where(lo, x2, 0.0), axis=-1, keepdims=True)
    s_hi = jnp.sum(jnp.where(lo, 0.0, x2), axis=-1, keepdims=True)
    return jnp.where(lo, s_lo, s_hi) * (1.0 / HEAD)


def _per_head(fn, x):
    n = x.shape[1] // LANES
    return jnp.concatenate([fn(x[:, i * LANES:(i + 1) * LANES], i) for i in range(n)], axis=1)


def _adaln_kernel(cv_ref, w_ref, b_ref, o_ref):
    o_ref[0] = _mm_exact(_silu(cv_ref[...]), w_ref[0]) + b_ref[0]


def _adaln(cv, w_ada, b_ada):
    n_layers, d, _ = w_ada.shape
    rows = cv.shape[0]
    return pl.pallas_call(
        _adaln_kernel,
        grid=(n_layers, 6),
        in_specs=[pl.BlockSpec((rows, d), lambda l, j: (0, 0)),
                  pl.BlockSpec((1, d, d), lambda l, j: (l, 0, j)),
                  pl.BlockSpec((1, 1, d), lambda l, j: (l * 6 + j, 0, 0))],
        out_specs=pl.BlockSpec((1, rows, d), lambda l, j: (l * 6 + j, 0, 0)),
        out_shape=jax.ShapeDtypeStruct((n_layers * 6, rows, d), F32),
        compiler_params=_cparams(("arbitrary", "arbitrary")),
        name="adaln",
    )(cv, w_ada, b_ada.reshape(n_layers * 6, 1, d))


def _inproj_kernel(x_ref, sh_ref, sc_ref, g_ref, w_ref, wabt_ref, gq_ref, gk_ref, cos_ref, sin_ref,
                   qkv_ref, gate_ref, ab_ref, abt_ref, lru_ref, q_ref, k_ref, v_ref):
    x = x_ref[...]
    ms = jnp.mean(x * x, axis=-1, keepdims=True)
    h = (x * lax.rsqrt(ms + EPS) * g_ref[0]) * (1.0 + sc_ref[0]) + sh_ref[0]
    hb = h.astype(BF16)
    p = jnp.dot(hb, w_ref[0], preferred_element_type=F32)
    o = 0
    qkv_ref[...] = p[:, o:o + P_QKV]
    o += P_QKV
    gate_ref[...] = p[:, o:o + P_GATE]
    o += P_GATE
    ab_ref[...] = p[:, o:o + P_AB]
    o += P_AB
    lru_ref[...] = p[:, o:o + P_LRU]
    o += P_LRU
    att = p[:, o:o + P_ATT]
    abt_ref[...] = lax.dot_general(wabt_ref[0], hb, (((1,), (1,)), ((), ())), preferred_element_type=F32)

    cos = cos_ref[...]
    sin = sin_ref[...]
    lane = lax.broadcasted_iota(jnp.int32, cos.shape, 1)
    first_half = (lane & 16) == 0

    def norm_rope(gain):
        def fn(s, _):
            y = s * lax.rsqrt(_group_mean_square(s) + EPS) * gain
            swapped = jnp.where(first_half, pltpu.roll(y, LANES - 16, 1), pltpu.roll(y, 16, 1))
            return y * cos + swapped * sin
        return fn

    q = _per_head(norm_rope(gq_ref[0]), att[:, :ATT_W])
    k = _per_head(norm_rope(gk_ref[0]), att[:, ATT_W:ATT_W + ATT_KV_W])
    q_ref[...] = (q * (HEAD ** -0.5)).astype(BF16)
    k_ref[...] = k.astype(BF16)
    v_ref[...] = att[:, ATT_W + ATT_KV_W:].astype(BF16)


def _inproj(layer, x, mod, norm1_g, w_pack, wabt, gq, gk, cos_t, sin_t, dims):
    b, c, s, d, tm, nt = dims
    n = x.shape[0]
    rows = mod.shape[0] // (6 * norm1_g.shape[0])

    def mod_idx(chunk):
        def f(i):
            r = jnp.where(i % nt == 0, b, i // nt)
            return ((layer * 6 + chunk) * rows + r, 0, 0)
        return f

    row_spec = lambda w: pl.BlockSpec((tm, w), lambda i: (i, 0))
    return pl.pallas_call(
        _inproj_kernel,
        grid=(n // tm,),
        in_specs=[row_spec(d),
                  pl.BlockSpec((1, 1, d), mod_idx(0)),
                  pl.BlockSpec((1, 1, d), mod_idx(1)),
                  pl.BlockSpec((1, 1, d), lambda i: (layer, 0, 0)),
                  pl.BlockSpec((1, d, P_ALL), lambda i: (layer, 0, 0)),
                  pl.BlockSpec((1, 16, d), lambda i: (layer, 0, 0)),
                  pl.BlockSpec((1, 1, LANES), lambda i: (layer, 0, 0)),
                  pl.BlockSpec((1, 1, LANES), lambda i: (layer, 0, 0)),
                  pl.BlockSpec((tm, LANES), lambda i: (i % nt, 0)),
                  pl.BlockSpec((tm, LANES), lambda i: (i % nt, 0))],
        out_specs=[row_spec(P_QKV), row_spec(P_GATE), row_spec(P_AB),
                   pl.BlockSpec((16, tm), lambda i: (0, i)),
                   row_spec(P_LRU), row_spec(ATT_W), row_spec(ATT_KV_W), row_spec(ATT_KV_W)],
        out_shape=[jax.ShapeDtypeStruct((n, P_QKV), F32), jax.ShapeDtypeStruct((n, P_GATE), F32),
                   jax.ShapeDtypeStruct((n, P_AB), F32), jax.ShapeDtypeStruct((16, n), F32),
                   jax.ShapeDtypeStruct((n, P_LRU), F32), jax.ShapeDtypeStruct((n, ATT_W), BF16),
                   jax.ShapeDtypeStruct((n, ATT_KV_W), BF16), jax.ShapeDtypeStruct((n, ATT_KV_W), BF16)],
        compiler_params=_cparams(("arbitrary",)),
        name="inproj",
    )(x, mod, mod, norm1_g, w_pack, wabt, gq, gk, cos_t, sin_t)


def _conv_rows(xp_ref, w, r0, rt, c):
    base = r0 + PAD_ROWS
    xm1 = xp_ref[base - 1:base - 1 + rt, :]
    x0 = xp_ref[base:base + rt, :]
    xp1 = xp_ref[base + 1:base + 1 + rt, :]
    xp2 = xp_ref[base + 2:base + 2 + rt, :]
    row = r0 + lax.broadcasted_iota(jnp.int32, (rt, 1), 0)
    xm1 = jnp.where(row == c, 0.0, xm1)
    xp1 = jnp.where(row == c - 1, 0.0, xp1)
    xp2 = jnp.where((row == c - 1) | (row == c - 2), 0.0, xp2)
    return w[0:1, :] * xm1 + w[1:2, :] * x0 + w[2:3, :] * xp1 + w[3:4, :] * xp2


def _fill_padded(xp_ref, x, t):
    zeros = jnp.zeros((PAD_ROWS, xp_ref.shape[1]), F32)
    xp_ref[0:PAD_ROWS, :] = zeros
    xp_ref[PAD_ROWS + t:2 * PAD_ROWS + t, :] = zeros
    xp_ref[PAD_ROWS:PAD_ROWS + t, :] = x


def _gdn_kernel(c, t, rt, qkv_ref, gate_ref, ab_ref, abt_ref, cw_ref, alog_r_ref, dt_r_ref, alog_c_ref,
                dt_c_ref, ng_ref, out_ref,
                xp_ref, q_s, k_s, v_s, g_s, gcf_s, gcb_s, grp_s, u_s, w_s, qk_s, qg_s, kt_s, eg_s, st_s, o_s):
    nch = t // CHUNK
    nc = c // CHUNK
    nl = nch - nc

    _fill_padded(xp_ref, qkv_ref[...], t)
    cw = cw_ref[0]

    def l2n(s, _):
        return s * lax.rsqrt(_group_mean_square(s) * HEAD + EPS)

    for r0 in range(0, t, rt):
        y = _silu(_conv_rows(xp_ref, cw, r0, rt, c))
        q_s[r0:r0 + rt, :] = _per_head(l2n, y[:, :GDN_W]) * (HEAD ** -0.5)
        k_s[r0:r0 + rt, :] = _per_head(l2n, y[:, GDN_W:2 * GDN_W])
        v_s[r0:r0 + rt, :] = y[:, 2 * GDN_W:]

    ab = ab_ref[...]
    lane = lax.broadcasted_iota(jnp.int32, ab.shape, 1)
    gval = -jnp.exp(alog_r_ref[0]) * _softplus(ab + dt_r_ref[0])
    g_s[...] = jnp.where(lane < 2 * GDN_H, gval, _sigmoid(ab))
    abt = abt_ref[...]
    g_t = -jnp.exp(alog_c_ref[0][:, 0:1]) * _softplus(abt + dt_c_ref[0][:, 0:1])

    ri = lax.broadcasted_iota(jnp.int32, (CHUNK, CHUNK), 0)
    ci = lax.broadcasted_iota(jnp.int32, (CHUNK, CHUNK), 1)
    lower_f = (ri >= ci).astype(F32)
    upper_f = (ri <= ci).astype(F32)
    row8 = lax.broadcasted_iota(jnp.int32, (2 * GDN_H, CHUNK), 0)
    for ch in range(nch):
        r0 = ch * CHUNK
        gch = g_s[r0:r0 + CHUNK, :]
        gcf_s[r0:r0 + CHUNK, :] = _mm_exact(lower_f, gch)
        gcb_s[r0:r0 + CHUNK, :] = _mm_exact(upper_f, gch)
        gtc = g_t[0:2 * GDN_H, r0:r0 + CHUNK]
        gr = jnp.where(row8 < GDN_H, _mm_exact(gtc, upper_f), _mm_exact(gtc, lower_f))
        for direction in range(2):
            rows = [gr[direction * GDN_H + hd:direction * GDN_H + hd + 1, :] for hd in range(GDN_H)]
            edge = CHUNK - 1 if direction == 0 else 0
            tots = [jnp.broadcast_to(r[:, edge:edge + 1], (1, CHUNK)) for r in rows]
            grp_s[ch, direction:direction + 1, :] = jnp.concatenate(rows, axis=1)
            grp_s[ch, 2 + direction:3 + direction, :] = jnp.concatenate(tots, axis=1)

    st_s[...] = jnp.zeros(st_s.shape, F32)
    o_s[...] = jnp.zeros(o_s.shape, F32)

    ri4 = lax.broadcasted_iota(jnp.int32, (CHUNK, GDN_W), 0)
    li4 = lax.broadcasted_iota(jnp.int32, (CHUNK, GDN_W), 1)
    pos4 = li4 & (CHUNK - 1)
    head_masks = [jnp.where((li4 >= hd * HEAD) & (li4 < (hd + 1) * HEAD), 1.0, 0.0).astype(BF16)
                  for hd in range(GDN_H)]
    lane_lo = lax.broadcasted_iota(jnp.int32, (CHUNK, LANES), 1) < HEAD
    bi = jnp.right_shift(lax.broadcasted_iota(jnp.int32, (GDN_W, GDN_W), 0), 6)
    bj = jnp.right_shift(lax.broadcasted_iota(jnp.int32, (GDN_W, GDN_W), 1), 6)
    same_head = bi == bj

    def block_diag(xb):
        return jnp.concatenate([xb * m for m in head_masks], axis=0)

    def expand4(x, base):
        cols = [jnp.broadcast_to(x[:, base + hd:base + hd + 1], (CHUNK, LANES)) for hd in range(GDN_H)]
        return jnp.concatenate([jnp.where(lane_lo, cols[0], cols[1]), jnp.where(lane_lo, cols[2], cols[3])],
                               axis=1)

    def solve_units(units):
        work = []
        for direction, ch in units:
            r0 = pl.multiple_of(ch * CHUNK, CHUNK)
            unit = direction * nch + ch
            q = q_s[pl.ds(r0, CHUNK), :]
            k = k_s[pl.ds(r0, CHUNK), :]
            v = v_s[pl.ds(r0, CHUNK), :]
            gcx = expand4((gcf_s if direction == 0 else gcb_s)[pl.ds(r0, CHUNK), :], direction * GDN_H)
            betax = expand4(g_s[pl.ds(r0, CHUNK), :], 2 * GDN_H + direction * GDN_H)
            grow = grp_s[ch, direction:direction + 1, :]
            gtot = grp_s[ch, 2 + direction:3 + direction, :]
            incl = (ri4 >= pos4) if direction == 0 else (ri4 <= pos4)
            strict = (ri4 > pos4) if direction == 0 else (ri4 < pos4)
            decay = jnp.exp(jnp.minimum(gcx - grow, 0.0))
            eg = jnp.exp(gcx)
            kb = k * betax
            prod = lax.dot_general(jnp.concatenate([q, kb], axis=0).astype(BF16), block_diag(k.astype(BF16)),
                                   (((1,), (1,)), ((), ())), preferred_element_type=F32)
            qk_s[unit] = jnp.where(incl, prod[:CHUNK] * decay, 0.0).astype(BF16)
            qg_s[unit] = (q * eg).astype(BF16)
            kt_s[unit] = (k * jnp.exp(gtot - gcx)).astype(BF16)
            eg_s[unit] = jnp.broadcast_to(jnp.exp(gtot), (SUBLANES, GDN_W))
            a_mat = jnp.where(strict, prod[CHUNK:] * decay, 0.0)
            work.append([unit, a_mat, v * betax, kb * eg])
        for stage in range(6):
            for item in work:
                _, p, ru, rw = item
                pb = p.astype(BF16)
                parts = [block_diag(ru.astype(BF16)), block_diag(rw.astype(BF16))]
                if stage < 5:
                    parts = [block_diag(pb)] + parts
                res = jnp.dot(pb, jnp.concatenate(parts, axis=1), preferred_element_type=F32)
                off = GDN_W if stage < 5 else 0
                du = res[:, off:off + GDN_W]
                dw = res[:, off + GDN_W:]
                item[1] = res[:, :GDN_W] if stage < 5 else None
                item[2] = ru - du if stage == 0 else ru + du
                item[3] = rw - dw if stage == 0 else rw + dw
        for unit, _, ru, rw in work:
            u_s[unit] = ru
            w_s[unit] = rw.astype(BF16)

    per_step = 4 if nch % 4 == 0 else 2

    def solve_step(i, carry):
        solve_units([(direction, per_step * i + j) for j in range(per_step) for direction in range(2)])
        return carry

    lax.fori_loop(0, nch // per_step, solve_step, 0)

    def advance(chunks):
        first = []
        for direction, ch in enumerate(chunks):
            unit = direction * nch + ch
            state = st_s[direction]
            res = jnp.dot(jnp.concatenate([w_s[unit], qg_s[unit]], axis=0), state.astype(BF16),
                          preferred_element_type=F32)
            v_new = (u_s[unit] - res[:CHUNK]).astype(BF16)
            first.append((unit, state, res[CHUNK:], v_new))
        for direction, (unit, state, o_state, v_new) in enumerate(first):
            r0 = pl.multiple_of(chunks[direction] * CHUNK, CHUNK)
            o = o_state + jnp.dot(qk_s[unit], block_diag(v_new), preferred_element_type=F32)
            upd = lax.dot_general(kt_s[unit], v_new, (((0,), (0,)), ((), ())), preferred_element_type=F32)
            st_s[direction] = state * eg_s[unit, 0:1, :] + jnp.where(same_head, upd, 0.0)
            o_s[pl.ds(r0, CHUNK), :] = o_s[pl.ds(r0, CHUNK), :] + o

    def ctx_step(s, carry):
        advance((s, nc - 1 - s))
        return carry

    def lat_step(s, carry):
        advance((nc + s, nch - 1 - s))
        return carry

    lax.fori_loop(0, nc, ctx_step, 0)
    lax.fori_loop(0, nl, lat_step, 0)

    ng = ng_ref[0]

    def rms(s, i):
        return s * lax.rsqrt(_group_mean_square(s) + EPS) * ng[:, i * LANES:(i + 1) * LANES]

    for r0 in range(0, t, rt):
        y = _per_head(rms, o_s[r0:r0 + rt, :]) * _silu(gate_ref[r0:r0 + rt, :])
        out_ref[r0:r0 + rt, :] = y.astype(BF16)


def _gdn(layer, qkv, gate, ab, abt, conv_w, alog_r, dt_r, alog_c, dt_c, ng, dims):
    b, c, s, d, tm, nt = dims
    t = c + s
    n = qkv.shape[0]
    lay = lambda shape: pl.BlockSpec(shape, lambda i: (layer,) + (0,) * (len(shape) - 1))
    nch = t // CHUNK
    assert nch % 2 == 0
    once = pl.Buffered(1)
    return pl.pallas_call(
        functools.partial(_gdn_kernel, c, t, tm),
        grid=(b,),
        in_specs=[pl.BlockSpec((t, P_QKV), lambda i: (i, 0), pipeline_mode=once),
                  pl.BlockSpec((t, P_GATE), lambda i: (i, 0), pipeline_mode=once),
                  pl.BlockSpec((t, P_AB), lambda i: (i, 0)),
                  pl.BlockSpec((16, t), lambda i: (0, i)),
                  lay((1, CONV_K, P_QKV)), lay((1, 1, LANES)), lay((1, 1, LANES)),
                  lay((1, 16, LANES)), lay((1, 16, LANES)), lay((1, 1, GDN_W))],
        out_specs=pl.BlockSpec((t, GDN_W), lambda i: (i, 0)),
        out_shape=jax.ShapeDtypeStruct((n, GDN_W), BF16),
        scratch_shapes=[pltpu.VMEM((t + 2 * PAD_ROWS, P_QKV), F32),
                        pltpu.VMEM((t, GDN_W), F32), pltpu.VMEM((t, GDN_W), F32), pltpu.VMEM((t, GDN_W), F32),
                        pltpu.VMEM((t, LANES), F32), pltpu.VMEM((t, LANES), F32), pltpu.VMEM((t, LANES), F32),
                        pltpu.VMEM((nch, SUBLANES, GDN_W), F32),
                        pltpu.VMEM((2 * nch, CHUNK, GDN_W), F32),
                        pltpu.VMEM((2 * nch, CHUNK, GDN_W), BF16), pltpu.VMEM((2 * nch, CHUNK, GDN_W), BF16),
                        pltpu.VMEM((2 * nch, CHUNK, GDN_W), BF16), pltpu.VMEM((2 * nch, CHUNK, GDN_W), BF16),
                        pltpu.VMEM((2 * nch, SUBLANES, GDN_W), F32),
                        pltpu.VMEM((2, GDN_W, GDN_W), F32),
                        pltpu.VMEM((t, GDN_W), F32)],
        compiler_params=_cparams(("arbitrary",)),
        name="gdn",
    )(qkv, gate, ab, abt, conv_w, alog_r, dt_r, alog_c, dt_c, ng)


def _lru_kernel(c, t, rt, p_ref, cw_ref, cb_ref, wr_ref, br_ref, wi_ref, bi_ref, lam_ref, out_ref,
                xp_ref, a_s, b_s, h_s):
    _fill_padded(xp_ref, p_ref[:, :LRU_W], t)
    cw = cw_ref[0]
    cb = cb_ref[0]
    for r0 in range(0, t, rt):
        xr = _conv_rows(xp_ref, cw, r0, rt, c) + cb
        for dirn in range(2):
            r = _sigmoid(_mm(xr, wr_ref[0, dirn]) + br_ref[0, dirn:dirn + 1, :])
            i = _sigmoid(_mm(xr, wi_ref[0, dirn]) + bi_ref[0, dirn:dirn + 1, :])
            log_a = (-LRU_C) * r * _softplus(-lam_ref[0, dirn:dirn + 1, :])
            a = jnp.exp(log_a)
            a_s[dirn, r0:r0 + rt, :] = a
            b_s[dirn, r0:r0 + rt, :] = jnp.sqrt(1.0 - a * a) * (i * xr)

    row = lax.broadcasted_iota(jnp.int32, (SUBLANES, LRU_W), 0)

    def tile_scan(dirn, i, h_prev):
        r0 = pl.multiple_of(i * SUBLANES, SUBLANES)
        a = a_s[dirn, pl.ds(r0, SUBLANES), :]
        bx = b_s[dirn, pl.ds(r0, SUBLANES), :]
        for sh in (1, 2, 4):
            if dirn == 0:
                a_sh, b_sh, m = pltpu.roll(a, sh, 0), pltpu.roll(bx, sh, 0), row >= sh
            else:
                a_sh, b_sh, m = (pltpu.roll(a, SUBLANES - sh, 0), pltpu.roll(bx, SUBLANES - sh, 0),
                                 row < SUBLANES - sh)
            bx = jnp.where(m, a * b_sh + bx, bx)
            a = jnp.where(m, a * a_sh, a)
        h = a * h_prev + bx
        h_s[dirn, pl.ds(r0, SUBLANES), :] = h
        return h[SUBLANES - 1:SUBLANES, :] if dirn == 0 else h[0:1, :]

    n_t = t // SUBLANES
    n_c = c // SUBLANES
    zero = jnp.zeros((1, LRU_W), F32)

    def fwd_step(i, carry):
        hf, hb = carry
        hf = tile_scan(0, i, hf)
        hb = tile_scan(1, jnp.where(i < n_c, n_c - 1 - i, n_t - 1 - (i - n_c)), hb)
        return hf, hb

    lax.fori_loop(0, n_t, fwd_step, (zero, zero))

    for r0 in range(0, t, rt):
        yb = p_ref[r0:r0 + rt, LRU_W:]
        gelu = 0.5 * yb * (1.0 + jnp.tanh(0.7978845608028654 * (yb + 0.044715 * (yb * yb * yb))))
        out_ref[r0:r0 + rt, :] = ((h_s[0, r0:r0 + rt, :] + h_s[1, r0:r0 + rt, :]) * gelu).astype(BF16)


def _lru(layer, p, conv_w, conv_b, wr, br, wi, bi, lam, dims):
    b, c, s, d, tm, nt = dims
    t = c + s
    n = p.shape[0]
    lay = lambda shape: pl.BlockSpec(shape, lambda i: (layer,) + (0,) * (len(shape) - 1))
    return pl.pallas_call(
        functools.partial(_lru_kernel, c, t, tm),
        grid=(b,),
        in_specs=[pl.BlockSpec((t, P_LRU), lambda i: (i, 0)),
                  lay((1, CONV_K, LRU_W)), lay((1, 1, LRU_W)),
                  lay((1, 2, LRU_W, LRU_W)), lay((1, 2, LRU_W)),
                  lay((1, 2, LRU_W, LRU_W)), lay((1, 2, LRU_W)), lay((1, 2, LRU_W))],
        out_specs=pl.BlockSpec((t, LRU_W), lambda i: (i, 0)),
        out_shape=jax.ShapeDtypeStruct((n, LRU_W), BF16),
        scratch_shapes=[pltpu.VMEM((t + 2 * PAD_ROWS, LRU_W), F32),
                        pltpu.VMEM((2, t, LRU_W), F32), pltpu.VMEM((2, t, LRU_W), F32),
                        pltpu.VMEM((2, t, LRU_W), F32)],
        compiler_params=_cparams(("arbitrary",)),
        name="lru",
    )(p, conv_w, conv_b, wr, br, wi, bi, lam)


def _attn_rows(q, k, v):
    outs = []
    for hq in range(ATT_W // HEAD):
        kv = hq // ATT_GROUP
        qh = q[:, hq * HEAD:(hq + 1) * HEAD]
        kh = k[:, kv * HEAD:(kv + 1) * HEAD]
        vh = v[:, kv * HEAD:(kv + 1) * HEAD]
        s = lax.dot_general(qh, kh, (((1,), (1,)), ((), ())), preferred_element_type=F32)
        m = jnp.max(s, axis=-1, keepdims=True)
        p = jnp.exp(s - m)
        l = jnp.sum(p, axis=-1, keepdims=True)
        o = jnp.dot(p.astype(BF16), vh, preferred_element_type=F32)
        outs.append(o * (1.0 / l))
    return jnp.concatenate(outs, axis=1)


def _attn_kernel(c, with_ctx, q_ref, k_ref, v_ref, o_ref):
    def latent():
        o_ref[...] = _attn_rows(q_ref[...], k_ref[...], v_ref[...]).astype(BF16)

    if not with_ctx:
        latent()
        return
    j = pl.program_id(1)

    @pl.when(j == 0)
    def _():
        o_ref[...] = _attn_rows(q_ref[...], k_ref[0:c, :], v_ref[0:c, :]).astype(BF16)

    pl.when(j > 0)(latent)


def _attn(q, k, v, with_ctx, dims):
    b, c, s, d, tm, nt = dims
    t = c + s
    n = q.shape[0]
    off = 0 if with_ctx else 1
    return pl.pallas_call(
        functools.partial(_attn_kernel, c, with_ctx),
        grid=(b, nt - off),
        in_specs=[pl.BlockSpec((tm, ATT_W), lambda i, j: (i * nt + j + off, 0)),
                  pl.BlockSpec((t, ATT_KV_W), lambda i, j: (i, 0)),
                  pl.BlockSpec((t, ATT_KV_W), lambda i, j: (i, 0))],
        out_specs=pl.BlockSpec((tm, ATT_W), lambda i, j: (i * (nt - off) + j, 0)),
        out_shape=jax.ShapeDtypeStruct((b * (nt - off) * tm, ATT_W), BF16),
        compiler_params=_cparams(("arbitrary", "arbitrary")),
        name="attn",
    )(q, k, v)


R_E1, R_E2, R_G1, R_G2, R_RANK1, R_RANK2 = range(6)


def _store_token_tiles(ref, x):
    for s in range(SUBLANES):
        ref[..., s, :] = x[:, s * LANES:(s + 1) * LANES]


def _load_token_tiles(ref):
    return jnp.concatenate([ref[:, s, :] for s in range(SUBLANES)], axis=1)


def _outproj_kernel(tm, a_ref, b_ref, c_ref, x_ref, g1_ref, sh_ref, sc_ref, ng_ref, wo_ref, wr_ref, br_ref,
                    xo_ref, h_ref, route_ref, cnt_ref, carry_s):
    i = pl.program_id(0)

    @pl.when(i == 0)
    def _():
        carry_s[...] = jnp.zeros(carry_s.shape, F32)

    mix = (jnp.dot(a_ref[...], wo_ref[0, 0:GDN_W, :], preferred_element_type=F32)
           + jnp.dot(b_ref[...], wo_ref[0, GDN_W:GDN_W + LRU_W, :], preferred_element_type=F32)
           + jnp.dot(c_ref[...], wo_ref[0, GDN_W + LRU_W:, :], preferred_element_type=F32))
    x = x_ref[...] + g1_ref[0] * mix
    xo_ref[...] = x
    ms = jnp.mean(x * x, axis=-1, keepdims=True)
    h = (x * lax.rsqrt(ms + EPS) * ng_ref[0]) * (1.0 + sc_ref[0]) + sh_ref[0]
    _store_token_tiles(h_ref, h)

    h_hi = h.astype(BF16)
    h_lo = (h - h_hi.astype(F32)).astype(BF16)
    logits = jnp.dot(jnp.concatenate([h_hi, h_lo, h_hi], axis=1), wr_ref[0],
                     preferred_element_type=F32) + br_ref[0]
    lane_i = lax.broadcasted_iota(jnp.int32, logits.shape, 1)
    lane = lane_i.astype(F32)
    lane_group = jnp.right_shift(lane_i, 3).astype(F32)
    big = jnp.float32(4 * LANES)
    neg = jnp.float32(-jnp.inf)
    is_group = (lane_i >= N_EXPERTS) & (lane_i < N_EXPERTS + N_GROUPS)
    gl = jnp.where(is_group, logits, neg)
    gm = jnp.max(gl, axis=-1, keepdims=True)
    pg_top = 1.0 / jnp.sum(jnp.where(is_group, jnp.exp(gl - gm), 0.0), axis=-1, keepdims=True)
    g_idx = jnp.min(jnp.where(gl == gm, lane, big), axis=-1, keepdims=True) - N_EXPERTS
    in_group = (lane_i < N_EXPERTS) & (lane_group == g_idx)
    le = jnp.where(in_group, logits, neg)
    m1 = jnp.max(le, axis=-1, keepdims=True)
    e1 = jnp.min(jnp.where(le == m1, lane, big), axis=-1, keepdims=True)
    le2 = jnp.where(lane == e1, neg, le)
    m2 = jnp.max(le2, axis=-1, keepdims=True)
    e2 = jnp.min(jnp.where(le2 == m2, lane, big), axis=-1, keepdims=True)
    z = jnp.sum(jnp.where(in_group, jnp.exp(le - m1), 0.0), axis=-1, keepdims=True)
    pe1 = 1.0 / z
    pe2 = jnp.exp(m2 - m1) / z
    gate1 = pg_top * pe1 / (pe1 + pe2)
    gate2 = pg_top * pe2 / (pe1 + pe2)

    sel1 = lane == e1
    sel2 = lane == e2
    onehot = jnp.where(sel1 | sel2, 1.0, 0.0)
    ri = lax.broadcasted_iota(jnp.int32, (tm, tm), 0)
    ci = lax.broadcasted_iota(jnp.int32, (tm, tm), 1)
    before = _mm(jnp.where(ri > ci, 1.0, 0.0), onehot) + carry_s[0:1, :]
    rank1 = jnp.sum(jnp.where(sel1, before, 0.0), axis=-1, keepdims=True)
    rank2 = jnp.sum(jnp.where(sel2, before, 0.0), axis=-1, keepdims=True)
    carry = carry_s[0:1, :] + jnp.sum(onehot, axis=0, keepdims=True)
    carry_s[...] = jnp.broadcast_to(carry, carry_s.shape)
    cnt_ref[...] = jnp.broadcast_to(carry, cnt_ref.shape)

    route = jnp.zeros(logits.shape, F32)
    for col, val in ((R_E1, e1), (R_E2, e2), (R_G1, gate1), (R_G2, gate2),
                     (R_RANK1, rank1), (R_RANK2, rank2)):
        route = jnp.where(lane_i == col, val, route)
    route_ref[...] = route


def _tile_map(with_ctx, nt):
    if with_ctx:
        return lambda i: i
    return lambda i: (i // (nt - 1)) * nt + 1 + i % (nt - 1)


def _outproj(layer, a, bm, cm, x, mod, norm2_g, w_out, w_route, b_route, with_ctx, dims):
    b, c, s, d, tm, nt = dims
    n = x.shape[0]
    n_layers = norm2_g.shape[0]
    rows = mod.shape[0] // (6 * n_layers)
    tile = _tile_map(with_ctx, nt)
    n_tiles = b * (nt if with_ctx else nt - 1)

    def mod_idx(chunk):
        def f(i):
            ti = tile(i)
            r = jnp.where(ti % nt == 0, b, ti // nt)
            return ((layer * 6 + chunk) * rows + r, 0, 0)
        return f

    row_spec = lambda w: pl.BlockSpec((tm, w), lambda i: (tile(i), 0))
    compact = lambda w: pl.BlockSpec((tm, w), lambda i: (i, 0))
    lay = lambda shape: pl.BlockSpec(shape, lambda i: (layer,) + (0,) * (len(shape) - 1))
    n_moe = n_tiles * tm
    return pl.pallas_call(
        functools.partial(_outproj_kernel, tm),
        grid=(n_tiles,),
        in_specs=[row_spec(GDN_W), row_spec(LRU_W), compact(ATT_W), row_spec(d),
                  pl.BlockSpec((1, 1, d), mod_idx(2)), pl.BlockSpec((1, 1, d), mod_idx(3)),
                  pl.BlockSpec((1, 1, d), mod_idx(4)),
                  lay((1, 1, d)), lay((1, d, d)), lay((1, 3 * d, LANES)), lay((1, 1, LANES))],
        out_specs=[row_spec(d), pl.BlockSpec((tm, SUBLANES, LANES), lambda i: (i, 0, 0)), compact(LANES),
                   pl.BlockSpec((SUBLANES, LANES), lambda i: (0, 0))],
        out_shape=[jax.ShapeDtypeStruct((n, d), F32), jax.ShapeDtypeStruct((n_moe, SUBLANES, LANES), F32),
                   jax.ShapeDtypeStruct((n_moe, LANES), F32), jax.ShapeDtypeStruct((SUBLANES, LANES), F32)],
        scratch_shapes=[pltpu.VMEM((SUBLANES, LANES), F32)],
        input_output_aliases={3: 0},
        compiler_params=_cparams(("arbitrary",)),
        name="outproj",
    )(a, bm, cm, x, mod, mod, mod, norm2_g, w_out, w_route, b_route)


def _dest_kernel(route_ref, cnt_ref, dest_ref):
    route = route_ref[...]
    lane = lax.broadcasted_iota(jnp.int32, route.shape, 1)
    counts = cnt_ref[...]
    padded = jnp.floor((counts + (MOE_BLOCK - 1)) * (1.0 / MOE_BLOCK)) * MOE_BLOCK
    li = lax.broadcasted_iota(jnp.int32, (LANES, LANES), 0)
    lj = lax.broadcasted_iota(jnp.int32, (LANES, LANES), 1)
    start = _mm_exact(padded, jnp.where((li < lj) & (li < N_EXPERTS), 1.0, 0.0))[0:1, :]

    def col(j):
        return jnp.sum(jnp.where(lane == j, route, 0.0), axis=-1, keepdims=True)

    def slot(e, rank):
        return jnp.sum(jnp.where(lane == e.astype(jnp.int32), start, 0.0), axis=-1, keepdims=True) + rank

    d1 = slot(col(R_E1), col(R_RANK1))
    d2 = slot(col(R_E2), col(R_RANK2))
    dest_ref[...] = jnp.where(lane == 0, d1, jnp.where(lane == 1, d2, 0.0)).astype(jnp.int32)


def _dest(route, counts, dims):
    tm = dims[4]
    n = route.shape[0]
    return pl.pallas_call(
        _dest_kernel,
        grid=(n // tm,),
        in_specs=[pl.BlockSpec((tm, LANES), lambda i: (i, 0)),
                  pl.BlockSpec((SUBLANES, LANES), lambda i: (0, 0))],
        out_specs=pl.BlockSpec((tm, LANES), lambda i: (i, 0)),
        out_shape=jax.ShapeDtypeStruct((n, LANES), jnp.int32),
        compiler_params=_cparams(("arbitrary",)),
        name="dest",
    )(route, counts)


def _invert_kernel(n, d1_ref, d2_ref, slot_ref):
    def clear(s, carry):
        slot_ref[s] = 2 * n + (s & (SPARE_ROWS - 1))
        return carry

    lax.fori_loop(0, slot_ref.shape[0], clear, 0, unroll=8)

    def place(t, carry):
        slot_ref[d1_ref[t]] = t
        slot_ref[d2_ref[t]] = n + t
        return carry

    lax.fori_loop(0, n, place, 0, unroll=8)


def _invert(d1, d2, n_slots):
    n = d1.shape[0]
    return pl.pallas_call(
        functools.partial(_invert_kernel, n),
        grid_spec=pltpu.PrefetchScalarGridSpec(
            num_scalar_prefetch=2,
            grid=(1,),
            in_specs=[],
            out_specs=pl.BlockSpec(memory_space=pltpu.SMEM)),
        out_shape=jax.ShapeDtypeStruct((n_slots,), jnp.int32),
        compiler_params=_cparams(("arbitrary",)),
        name="invert",
    )(d1, d2)


SPARE_ROWS = 2 * MOE_BLOCK


def _expert_kernel(be_ref, nact_ref, slot_ref, h_ref, w1_ref, w3_ref, w2_ref, y_ref,
                   w1_s, w3_s, w2_s, x_s, y_s, gsem, ssem):
    i = pl.program_id(0)
    nact = nact_ref[0]
    n = h_ref.shape[0]
    quarter = MOE_BLOCK // 4

    def gather_rows(blk, buf, rows):
        for r in rows:
            dst = slot_ref[blk * MOE_BLOCK + r]
            tok = jnp.minimum(jnp.where(dst >= n, dst - n, dst), n - 1)
            pltpu.make_async_copy(h_ref.at[pl.ds(tok, 1)], x_s.at[buf, pl.ds(r, 1)], gsem.at[buf]).start()

    def scatter_rows(blk, buf, rows):
        for r in rows:
            dst = slot_ref[blk * MOE_BLOCK + r]
            pltpu.make_async_copy(y_s.at[buf, pl.ds(r, 1)], y_ref.at[pl.ds(dst, 1)], ssem.at[buf]).start()

    def wait_block(bufs, sem, buf):
        pltpu.make_async_copy(bufs.at[buf], bufs.at[buf], sem.at[buf]).wait()

    def compute(buf, between):
        x = _load_token_tiles(x_s.at[buf]).astype(BF16)
        between[0]()
        h1 = jnp.dot(x, w1_s[...], preferred_element_type=F32)
        between[1]()
        h3 = jnp.dot(x, w3_s[...], preferred_element_type=F32)
        between[2]()
        act = (_silu(h1) * h3).astype(BF16)
        y = jnp.dot(act, w2_s[...], preferred_element_type=F32)
        between[3]()
        return y

    @pl.when(i == 0)
    def _():
        y_s[...] = jnp.zeros(y_s.shape, F32)
        for buf in range(2):
            spare = pltpu.make_async_copy(y_s.at[buf], y_ref.at[pl.ds(2 * n + buf * MOE_BLOCK, MOE_BLOCK)],
                                          ssem.at[buf])
            spare.start()
            spare.wait()

    @pl.when((i < nact) & ((i == 0) | (be_ref[i] != be_ref[jnp.maximum(i - 1, 0)])))
    def _():
        w1_s[...] = w1_ref[0, 0].astype(BF16)
        w3_s[...] = w3_ref[0, 0].astype(BF16)
        w2_s[...] = w2_ref[0, 0].astype(BF16)

    steady = (i >= 2) & (i + 1 < nact)
    for par in range(2):
        @pl.when(steady & (i % 2 == par))
        def _(par=par):
            wait_block(x_s, gsem, par)

            def part(j):
                def issue():
                    rows = range(j * quarter, (j + 1) * quarter)
                    gather_rows(i + 1, 1 - par, rows)
                    scatter_rows(i - 1, 1 - par, rows)
                return issue

            y = compute(par, [part(j) for j in range(4)])
            wait_block(y_s, ssem, par)
            _store_token_tiles(y_s.at[par], y)

    @pl.when(jnp.logical_not(steady))
    def _():
        buf = i % 2
        pl.when((i == 0) & (nact > 0))(lambda: gather_rows(0, 0, range(MOE_BLOCK)))
        pl.when(i + 1 < nact)(lambda: gather_rows(i + 1, 1 - buf, range(MOE_BLOCK)))
        pl.when((i >= 1) & (i - 1 < nact))(lambda: scatter_rows(i - 1, 1 - buf, range(MOE_BLOCK)))
        pl.when((i >= 2) & (i - 2 < nact))(lambda: wait_block(y_s, ssem, buf))

        @pl.when(i < nact)
        def _():
            wait_block(x_s, gsem, buf)
            _store_token_tiles(y_s.at[buf], compute(buf, [lambda: None] * 4))


def _experts(layer, block_e, nact, slot_src, h, w1, w3, w2):
    n = h.shape[0]
    d = w1.shape[-2]
    hid = w1.shape[-1]
    n_steps = block_e.shape[0]
    n_slots = slot_src.shape[0]
    wspec = lambda shape: pl.BlockSpec(shape, lambda i, be, na, sl: (layer, be[i], 0, 0))
    return pl.pallas_call(
        _expert_kernel,
        grid_spec=pltpu.PrefetchScalarGridSpec(
            num_scalar_prefetch=3,
            grid=(n_steps,),
            in_specs=[pl.BlockSpec(memory_space=pl.ANY),
                      wspec((1, 1, d, hid)), wspec((1, 1, d, hid)), wspec((1, 1, hid, d))],
            out_specs=pl.BlockSpec(memory_space=pl.ANY),
            scratch_shapes=[pltpu.VMEM((d, hid), BF16), pltpu.VMEM((d, hid), BF16),
                            pltpu.VMEM((hid, d), BF16),
                            pltpu.VMEM((2, MOE_BLOCK, SUBLANES, LANES), F32),
                            pltpu.VMEM((2, MOE_BLOCK, SUBLANES, LANES), F32),
                            pltpu.SemaphoreType.DMA((2,)), pltpu.SemaphoreType.DMA((2,))]),
        out_shape=jax.ShapeDtypeStruct((2 * n + SPARE_ROWS, SUBLANES, LANES), F32),
        compiler_params=_cparams(("arbitrary",)),
        name="experts",
    )(block_e, nact, slot_src, h, w1, w3, w2)


def _combine_kernel(final, x_ref, route_ref, g2_ref, fg_ref, y1_ref, y2_ref, o_ref):
    route = route_ref[...]
    lane = lax.broadcasted_iota(jnp.int32, route.shape, 1)
    gate1 = jnp.sum(jnp.where(lane == R_G1, route, 0.0), axis=-1, keepdims=True)
    gate2 = jnp.sum(jnp.where(lane == R_G2, route, 0.0), axis=-1, keepdims=True)
    x = x_ref[...] + g2_ref[0] * (_load_token_tiles(y1_ref) * gate1 + _load_token_tiles(y2_ref) * gate2)
    if final:
        ms = jnp.mean(x * x, axis=-1, keepdims=True)
        x = x * lax.rsqrt(ms + EPS) * fg_ref[...]
    o_ref[...] = x


def _combine(layer, x, route, mod, final_g, y, with_ctx, final, n_layers, dims):
    b, c, s, d, tm, nt = dims
    n = x.shape[0]
    rows = mod.shape[0] // (6 * n_layers)
    tile = _tile_map(with_ctx, nt)
    n_tiles = b * (nt if with_ctx else nt - 1)

    def mod_idx(i):
        ti = tile(i)
        r = jnp.where(ti % nt == 0, b, ti // nt)
        return ((layer * 6 + 5) * rows + r, 0, 0)

    if final:
        out_spec = pl.BlockSpec((tm, d), lambda i: (i, 0))
        out_shape = jax.ShapeDtypeStruct((n_tiles * tm, d), F32)
        aliases = {}
    else:
        out_spec = pl.BlockSpec((tm, d), lambda i: (tile(i), 0))
        out_shape = jax.ShapeDtypeStruct((n, d), F32)
        aliases = {0: 0}
    return pl.pallas_call(
        functools.partial(_combine_kernel, final),
        grid=(n_tiles,),
        in_specs=[pl.BlockSpec((tm, d), lambda i: (tile(i), 0)),
                  pl.BlockSpec((tm, LANES), lambda i: (i, 0)),
                  pl.BlockSpec((1, 1, d), mod_idx),
                  pl.BlockSpec((1, d), lambda i: (0, 0)),
                  pl.BlockSpec((tm, SUBLANES, LANES), lambda i: (i, 0, 0)),
                  pl.BlockSpec((tm, SUBLANES, LANES), lambda i: (n_tiles + i, 0, 0))],
        out_specs=out_spec,
        out_shape=out_shape,
        input_output_aliases=aliases,
        compiler_params=_cparams(("arbitrary",)),
        name="combine",
    )(x, route, mod, final_g, y, y)


def _rope_tables(s, tm):
    rows = s // GRID_W
    row = jnp.repeat(jnp.arange(rows, dtype=F32), GRID_W)
    col = jnp.tile(jnp.arange(GRID_W, dtype=F32), rows)
    axis_dim = HEAD // 2
    inv_freq = ROPE_THETA ** (-jnp.arange(0, axis_dim, 2, dtype=F32) / axis_dim)
    ar = row[:, None] * inv_freq
    ac = col[:, None] * inv_freq
    cos = jnp.concatenate([jnp.cos(ar), jnp.cos(ar), jnp.cos(ac), jnp.cos(ac)], axis=1)
    sin = jnp.concatenate([-jnp.sin(ar), jnp.sin(ar), -jnp.sin(ac), jnp.sin(ac)], axis=1)
    cos = jnp.concatenate([jnp.ones((tm, HEAD), F32), cos], axis=0)
    sin = jnp.concatenate([jnp.zeros((tm, HEAD), F32), sin], axis=0)
    return jnp.tile(cos, (1, 2)), jnp.tile(sin, (1, 2))


def _block_diag(w):
    n_layers = w.shape[0]
    eye = jnp.eye(LRU_BLOCKS, dtype=w.dtype)
    full = jnp.einsum('ldnij,nm->ldnimj', w, eye)
    return full.reshape(n_layers, 2, LRU_W, LRU_W)


def _pad_lanes(a, width=LANES):
    return jnp.pad(a, [(0, 0)] * (a.ndim - 1) + [(0, width - a.shape[-1])])


def kernel(x, c, ctx, c_ctx, w_ada, b_ada, norm1_g, norm2_g, w_in, w_out, gdn_conv_w, gdn_a_log, gdn_dt_bias, gdn_norm_g, lru_conv_w, lru_conv_b, lru_w_r, lru_b_r, lru_w_i, lru_b_i, lru_lambda, attn_q_norm_g, attn_k_norm_g, moe_w_group, moe_b_group, moe_w_expert, moe_b_expert, moe_w1, moe_w3, moe_w2, final_norm_g):
    bsz, s, d = x.shape
    cl = ctx.shape[1]
    n_layers = w_in.shape[0]
    tm = cl
    assert s % tm == 0 and tm % CHUNK == 0 and s % GRID_W == 0 and d == SUBLANES * LANES
    nt = (cl + s) // tm
    dims = (bsz, cl, s, d, tm, nt)

    o1 = 4 * GDN_W + 4 * GDN_H
    o2 = o1 + 2 * LRU_W
    w_ab = w_in[:, :, 4 * GDN_W:o1]
    w_pack = jnp.concatenate([w_in[:, :, :4 * GDN_W], _pad_lanes(w_ab), w_in[:, :, o1:o2], w_in[:, :, o2:]],
                             axis=-1).astype(BF16)
    wabt = jnp.swapaxes(w_ab, 1, 2).astype(BF16)
    gq = jnp.tile(attn_q_norm_g, (1, 2))[:, None, :]
    gk = jnp.tile(attn_k_norm_g, (1, 2))[:, None, :]
    cos_t, sin_t = _rope_tables(s, tm)
    alog = gdn_a_log.reshape(n_layers, 2 * GDN_H)
    dtb = gdn_dt_bias.reshape(n_layers, 2 * GDN_H)
    alog_r = _pad_lanes(alog)[:, None, :]
    dt_r = _pad_lanes(dtb)[:, None, :]
    alog_c = jnp.broadcast_to(_pad_lanes(alog, 16)[:, :, None], (n_layers, 16, LANES))
    dt_c = jnp.broadcast_to(_pad_lanes(dtb, 16)[:, :, None], (n_layers, 16, LANES))
    gdn_ng = jnp.tile(gdn_norm_g, (1, GDN_H))[:, None, :]
    wr_bd = _block_diag(lru_w_r).astype(BF16)
    wi_bd = _block_diag(lru_w_i).astype(BF16)
    w_out_b = w_out.astype(BF16)
    w_route = _pad_lanes(jnp.concatenate([moe_w_expert, moe_w_group], axis=-1))
    w_route_hi = w_route.astype(BF16)
    w_route_lo = (w_route - w_route_hi.astype(F32)).astype(BF16)
    w_route = jnp.concatenate([w_route_hi, w_route_hi, w_route_lo], axis=1)
    b_route = _pad_lanes(jnp.concatenate([moe_b_expert, moe_b_group], axis=-1))[:, None, :]

    cv = jnp.concatenate([c, c_ctx[None, :]], axis=0)
    rows = -(-cv.shape[0] // SUBLANES) * SUBLANES
    cv = jnp.pad(cv, ((0, rows - cv.shape[0]), (0, 0)))
    mod = _adaln(cv, w_ada, b_ada).reshape(n_layers * 6 * rows, 1, d)

    xf = jnp.concatenate([ctx, x], axis=1).reshape(bsz * (cl + s), d)
    out = None
    for layer in range(n_layers):
        with_ctx = layer < n_layers - 1
        qkv, gate, ab, abt, plru, q, k, v = _inproj(
            layer, xf, mod, norm1_g[:, None, :], w_pack, wabt, gq, gk, cos_t, sin_t, dims)
        mix_a = _gdn(layer, qkv, gate, ab, abt, gdn_conv_w, alog_r, dt_r, alog_c, dt_c, gdn_ng, dims)
        mix_b = _lru(layer, plru, lru_conv_w, lru_conv_b[:, None, :], wr_bd, lru_b_r, wi_bd, lru_b_i,
                     lru_lambda, dims)
        mix_c = _attn(q, k, v, with_ctx, dims)
        xf, h, route, counts = _outproj(layer, mix_a, mix_b, mix_c, xf, mod, norm2_g[:, None, :], w_out_b,
                                        w_route, b_route, with_ctx, dims)
        dest = _dest(route, counts, dims)
        d1 = dest[:, 0]
        d2 = dest[:, 1]
        n_tok = bsz * ((cl + s) if with_ctx else s)
        n_blocks = -(-(2 * n_tok) // MOE_BLOCK) + N_EXPERTS
        cnt = counts[0, :N_EXPERTS].astype(jnp.int32)
        padded = (cnt + MOE_BLOCK - 1) // MOE_BLOCK * MOE_BLOCK
        pad_end = jnp.cumsum(padded)
        block_row = jnp.arange(n_blocks + 2, dtype=jnp.int32) * MOE_BLOCK
        block_e = jnp.minimum(jnp.sum((pad_end[None, :] <= block_row[:, None]).astype(jnp.int32), axis=1),
                              N_EXPERTS - 1)
        nact = (pad_end[-1:] // MOE_BLOCK).astype(jnp.int32)
        slot_src = _invert(d1, d2, n_blocks * MOE_BLOCK)
        y = _experts(layer, block_e, nact, slot_src, h, moe_w1, moe_w3, moe_w2)
        res = _combine(layer, xf, route, mod, final_norm_g[None, :], y, with_ctx, not with_ctx, n_layers, dims)
        if with_ctx:
            xf = res
        else:
            out = res
    return out.reshape(bsz, s, d)
```

```python
import functools

import jax
import jax.numpy as jnp
from jax import lax
from jax.experimental import pallas as pl
from jax.experimental.pallas import tpu as pltpu

F32 = jnp.float32
BF16 = jnp.bfloat16
HIGHEST = lax.Precision.HIGHEST

HEAD = 64
GDN_W = 256
GDN_H = GDN_W // HEAD
CHUNK = 64
LRU_W = 256
LRU_BLOCKS = 4
LRU_C = 8.0
ATT_W = 512
ATT_KV_W = 128
ATT_GROUP = 4
CONV_K = 4
N_GROUPS = 8
N_EXPERTS = 64
EXPERTS_PER_GROUP = 8
MOE_HIDDEN = 512
MOE_BLOCK = 128
GRID_W = 64
ROPE_THETA = 10000.0
EPS = 1e-6
LANES = 128
SUBLANES = 8
PAD_ROWS = 8
VMEM_LIMIT = 56 * 1024 * 1024

P_QKV = 3 * GDN_W
P_GATE = GDN_W
P_AB = LANES
P_LRU = 2 * LRU_W
P_ATT = ATT_W + 2 * ATT_KV_W
P_ALL = P_QKV + P_GATE + P_AB + P_LRU + P_ATT


def _cparams(sem):
    return pltpu.CompilerParams(dimension_semantics=sem, vmem_limit_bytes=VMEM_LIMIT)


def _sigmoid(x):
    return 1.0 / (1.0 + jnp.exp(-x))


def _silu(x):
    return x * _sigmoid(x)


def _softplus(x):
    return jnp.maximum(x, 0.0) + jnp.log(1.0 + jnp.exp(-jnp.abs(x)))


def _mm(a, b):
    return jnp.dot(a.astype(BF16), b.astype(BF16), preferred_element_type=F32)


def _mm_nt(a, b):
    return lax.dot_general(a.astype(BF16), b.astype(BF16), (((1,), (1,)), ((), ())),
                           preferred_element_type=F32)


def _mm_tn(a, b):
    return lax.dot_general(a.astype(BF16), b.astype(BF16), (((0,), (0,)), ((), ())),
                           preferred_element_type=F32)


def _mm_exact(a, b):
    return jnp.dot(a, b, precision=HIGHEST, preferred_element_type=F32)


def _group_mean_square(x):
    lane = lax.broadcasted_iota(jnp.int32, x.shape, 1)
    lo = lane < HEAD
    x2 = x * x
    s_lo = jnp.sum(jnp.where(lo, x2, 0.0), axis=-1, keepdims=True)
    s_hi = jnp.sum(jnp.where(lo, 0.0, x2), axis=-1, keepdims=True)
    return jnp.where(lo, s_lo, s_hi) * (1.0 / HEAD)


def _per_head(fn, x):
    n = x.shape[1] // LANES
    return jnp.concatenate([fn(x[:, i * LANES:(i + 1) * LANES], i) for i in range(n)], axis=1)


def _adaln_kernel(cv_ref, w_ref, b_ref, o_ref):
    o_ref[0] = _mm_exact(_silu(cv_ref[...]), w_ref[0]) + b_ref[0]


def _adaln(cv, w_ada, b_ada):
    n_layers, d, _ = w_ada.shape
    rows = cv.shape[0]
    return pl.pallas_call(
        _adaln_kernel,
        grid=(n_layers, 6),
        in_specs=[pl.BlockSpec((rows, d), lambda l, j: (0, 0)),
                  pl.BlockSpec((1, d, d), lambda l, j: (l, 0, j)),
                  pl.BlockSpec((1, 1, d), lambda l, j: (l * 6 + j, 0, 0))],
        out_specs=pl.BlockSpec((1, rows, d), lambda l, j: (l * 6 + j, 0, 0)),
        out_shape=jax.ShapeDtypeStruct((n_layers * 6, rows, d), F32),
        compiler_params=_cparams(("arbitrary", "arbitrary")),
        name="adaln",
    )(cv, w_ada, b_ada.reshape(n_layers * 6, 1, d))


def _inproj_kernel(x_ref, sh_ref, sc_ref, g_ref, w_ref, wabt_ref, gq_ref, gk_ref, cos_ref, sin_ref,
                   qkv_ref, gate_ref, ab_ref, abt_ref, lru_ref, q_ref, k_ref, v_ref):
    x = x_ref[...]
    ms = jnp.mean(x * x, axis=-1, keepdims=True)
    h = (x * lax.rsqrt(ms + EPS) * g_ref[0]) * (1.0 + sc_ref[0]) + sh_ref[0]
    hb = h.astype(BF16)
    p = jnp.dot(hb, w_ref[0], preferred_element_type=F32)
    o = 0
    qkv_ref[...] = p[:, o:o + P_QKV]
    o += P_QKV
    gate_ref[...] = p[:, o:o + P_GATE]
    o += P_GATE
    ab_ref[...] = p[:, o:o + P_AB]
    o += P_AB
    lru_ref[...] = p[:, o:o + P_LRU]
    o += P_LRU
    att = p[:, o:o + P_ATT]
    abt_ref[...] = lax.dot_general(wabt_ref[0], hb, (((1,), (1,)), ((), ())), preferred_element_type=F32)

    cos = cos_ref[...]
    sin = sin_ref[...]
    lane = lax.broadcasted_iota(jnp.int32, cos.shape, 1)
    first_half = (lane & 16) == 0

    def norm_rope(gain):
        def fn(s, _):
            y = s * lax.rsqrt(_group_mean_square(s) + EPS) * gain
            swapped = jnp.where(first_half, pltpu.roll(y, LANES - 16, 1), pltpu.roll(y, 16, 1))
            return y * cos + swapped * sin
        return fn

    q = _per_head(norm_rope(gq_ref[0]), att[:, :ATT_W])
    k = _per_head(norm_rope(gk_ref[0]), att[:, ATT_W:ATT_W + ATT_KV_W])
    q_ref[...] = (q * (HEAD ** -0.5)).astype(BF16)
    k_ref[...] = k.astype(BF16)
    v_ref[...] = att[:, ATT_W + ATT_KV_W:].astype(BF16)


def _inproj(layer, x, mod, norm1_g, w_pack, wabt, gq, gk, cos_t, sin_t, dims):
    b, c, s, d, tm, nt = dims
    n = x.shape[0]
    rows = mod.shape[0] // (6 * norm1_g.shape[0])

    def mod_idx(chunk):
        def f(i):
            r = jnp.where(i % nt == 0, b, i // nt)
            return ((layer * 6 + chunk) * rows + r, 0, 0)
        return f

    row_spec = lambda w: pl.BlockSpec((tm, w), lambda i: (i, 0))
    return pl.pallas_call(
        _inproj_kernel,
        grid=(n // tm,),
        in_specs=[row_spec(d),
                  pl.BlockSpec((1, 1, d), mod_idx(0)),
                  pl.BlockSpec((1, 1, d), mod_idx(1)),
                  pl.BlockSpec((1, 1, d), lambda i: (layer, 0, 0)),
                  pl.BlockSpec((1, d, P_ALL), lambda i: (layer, 0, 0)),
                  pl.BlockSpec((1, 16, d), lambda i: (layer, 0, 0)),
                  pl.BlockSpec((1, 1, LANES), lambda i: (layer, 0, 0)),
                  pl.BlockSpec((1, 1, LANES), lambda i: (layer, 0, 0)),
                  pl.BlockSpec((tm, LANES), lambda i: (i % nt, 0)),
                  pl.BlockSpec((tm, LANES), lambda i: (i % nt, 0))],
        out_specs=[row_spec(P_QKV), row_spec(P_GATE), row_spec(P_AB),
                   pl.BlockSpec((16, tm), lambda i: (0, i)),
                   row_spec(P_LRU), row_spec(ATT_W), row_spec(ATT_KV_W), row_spec(ATT_KV_W)],
        out_shape=[jax.ShapeDtypeStruct((n, P_QKV), F32), jax.ShapeDtypeStruct((n, P_GATE), F32),
                   jax.ShapeDtypeStruct((n, P_AB), F32), jax.ShapeDtypeStruct((16, n), F32),
                   jax.ShapeDtypeStruct((n, P_LRU), F32), jax.ShapeDtypeStruct((n, ATT_W), BF16),
                   jax.ShapeDtypeStruct((n, ATT_KV_W), BF16), jax.ShapeDtypeStruct((n, ATT_KV_W), BF16)],
        compiler_params=_cparams(("arbitrary",)),
        name="inproj",
    )(x, mod, mod, norm1_g, w_pack, wabt, gq, gk, cos_t, sin_t)


def _conv_rows(xp_ref, w, r0, rt, c):
    base = r0 + PAD_ROWS
    xm1 = xp_ref[base - 1:base - 1 + rt, :]
    x0 = xp_ref[base:base + rt, :]
    xp1 = xp_ref[base + 1:base + 1 + rt, :]
    xp2 = xp_ref[base + 2:base + 2 + rt, :]
    row = r0 + lax.broadcasted_iota(jnp.int32, (rt, 1), 0)
    xm1 = jnp.where(row == c, 0.0, xm1)
    xp1 = jnp.where(row == c - 1, 0.0, xp1)
    xp2 = jnp.where((row == c - 1) | (row == c - 2), 0.0, xp2)
    return w[0:1, :] * xm1 + w[1:2, :] * x0 + w[2:3, :] * xp1 + w[3:4, :] * xp2


def _fill_padded(xp_ref, x, t):
    zeros = jnp.zeros((PAD_ROWS, xp_ref.shape[1]), F32)
    xp_ref[0:PAD_ROWS, :] = zeros
    xp_ref[PAD_ROWS + t:2 * PAD_ROWS + t, :] = zeros
    xp_ref[PAD_ROWS:PAD_ROWS + t, :] = x


def _gdn_kernel(c, t, rt, qkv_ref, gate_ref, ab_ref, abt_ref, cw_ref, alog_r_ref, dt_r_ref, alog_c_ref,
                dt_c_ref, ng_ref, out_ref,
                xp_ref, q_s, k_s, v_s, g_s, gcf_s, gcb_s, grp_s, u_s, w_s, qk_s, qg_s, kt_s, eg_s, st_s, o_s):
    nch = t // CHUNK
    nc = c // CHUNK
    nl = nch - nc

    _fill_padded(xp_ref, qkv_ref[...], t)
    cw = cw_ref[0]

    def l2n(s, _):
        return s * lax.rsqrt(_group_mean_square(s) * HEAD + EPS)

    for r0 in range(0, t, rt):
        y = _silu(_conv_rows(xp_ref, cw, r0, rt, c))
        q_s[r0:r0 + rt, :] = _per_head(l2n, y[:, :GDN_W]) * (HEAD ** -0.5)
        k_s[r0:r0 + rt, :] = _per_head(l2n, y[:, GDN_W:2 * GDN_W])
        v_s[r0:r0 + rt, :] = y[:, 2 * GDN_W:]

    ab = ab_ref[...]
    lane = lax.broadcasted_iota(jnp.int32, ab.shape, 1)
    gval = -jnp.exp(alog_r_ref[0]) * _softplus(ab + dt_r_ref[0])
    g_s[...] = jnp.where(lane < 2 * GDN_H, gval, _sigmoid(ab))
    abt = abt_ref[...]
    g_t = -jnp.exp(alog_c_ref[0][:, 0:1]) * _softplus(abt + dt_c_ref[0][:, 0:1])

    ri = lax.broadcasted_iota(jnp.int32, (CHUNK, CHUNK), 0)
    ci = lax.broadcasted_iota(jnp.int32, (CHUNK, CHUNK), 1)
    lower_f = (ri >= ci).astype(F32)
    upper_f = (ri <= ci).astype(F32)
    row8 = lax.broadcasted_iota(jnp.int32, (2 * GDN_H, CHUNK), 0)
    for ch in range(nch):
        r0 = ch * CHUNK
        gch = g_s[r0:r0 + CHUNK, :]
        gcf_s[r0:r0 + CHUNK, :] = _mm_exact(lower_f, gch)
        gcb_s[r0:r0 + CHUNK, :] = _mm_exact(upper_f, gch)
        gtc = g_t[0:2 * GDN_H, r0:r0 + CHUNK]
        gr = jnp.where(row8 < GDN_H, _mm_exact(gtc, upper_f), _mm_exact(gtc, lower_f))
        for direction in range(2):
            rows = [gr[direction * GDN_H + hd:direction * GDN_H + hd + 1, :] for hd in range(GDN_H)]
            edge = CHUNK - 1 if direction == 0 else 0
            tots = [jnp.broadcast_to(r[:, edge:edge + 1], (1, CHUNK)) for r in rows]
            grp_s[ch, direction:direction + 1, :] = jnp.concatenate(rows, axis=1)
            grp_s[ch, 2 + direction:3 + direction, :] = jnp.concatenate(tots, axis=1)

    st_s[...] = jnp.zeros(st_s.shape, F32)
    o_s[...] = jnp.zeros(o_s.shape, F32)

    ri4 = lax.broadcasted_iota(jnp.int32, (CHUNK, GDN_W), 0)
    li4 = lax.broadcasted_iota(jnp.int32, (CHUNK, GDN_W), 1)
    pos4 = li4 & (CHUNK - 1)
    head_masks = [jnp.where((li4 >= hd * HEAD) & (li4 < (hd + 1) * HEAD), 1.0, 0.0).astype(BF16)
                  for hd in range(GDN_H)]
    lane_lo = lax.broadcasted_iota(jnp.int32, (CHUNK, LANES), 1) < HEAD
    bi = jnp.right_shift(lax.broadcasted_iota(jnp.int32, (GDN_W, GDN_W), 0), 6)
    bj = jnp.right_shift(lax.broadcasted_iota(jnp.int32, (GDN_W, GDN_W), 1), 6)
    same_head = bi == bj

    def block_diag(xb):
        return jnp.concatenate([xb * m for m in head_masks], axis=0)

    def expand4(x, base):
        cols = [jnp.broadcast_to(x[:, base + hd:base + hd + 1], (CHUNK, LANES)) for hd in range(GDN_H)]
        return jnp.concatenate([jnp.where(lane_lo, cols[0], cols[1]), jnp.where(lane_lo, cols[2], cols[3])],
                               axis=1)

    def solve_units(units):
        work = []
        for direction, ch in units:
            r0 = pl.multiple_of(ch * CHUNK, CHUNK)
            unit = direction * nch + ch
            q = q_s[pl.ds(r0, CHUNK), :]
            k = k_s[pl.ds(r0, CHUNK), :]
            v = v_s[pl.ds(r0, CHUNK), :]
            gcx = expand4((gcf_s if direction == 0 else gcb_s)[pl.ds(r0, CHUNK), :], direction * GDN_H)
            betax = expand4(g_s[pl.ds(r0, CHUNK), :], 2 * GDN_H + direction * GDN_H)
            grow = grp_s[ch, direction:direction + 1, :]
            gtot = grp_s[ch, 2 + direction:3 + direction, :]
            incl = (ri4 >= pos4) if direction == 0 else (ri4 <= pos4)
            strict = (ri4 > pos4) if direction == 0 else (ri4 < pos4)
            decay = jnp.exp(jnp.minimum(gcx - grow, 0.0))
            eg = jnp.exp(gcx)
            kb = k * betax
            prod = lax.dot_general(jnp.concatenate([q, kb], axis=0).astype(BF16), block_diag(k.astype(BF16)),
                                   (((1,), (1,)), ((), ())), preferred_element_type=F32)
            qk_s[unit] = jnp.where(incl, prod[:CHUNK] * decay, 0.0).astype(BF16)
            qg_s[unit] = (q * eg).astype(BF16)
            kt_s[unit] = (k * jnp.exp(gtot - gcx)).astype(BF16)
            eg_s[unit] = jnp.broadcast_to(jnp.exp(gtot), (SUBLANES, GDN_W))
            a_mat = jnp.where(strict, prod[CHUNK:] * decay, 0.0)
            work.append([unit, a_mat, v * betax, kb * eg])
        for stage in range(6):
            for item in work:
                _, p, ru, rw = item
                pb = p.astype(BF16)
                parts = [block_diag(ru.astype(BF16)), block_diag(rw.astype(BF16))]
                if stage < 5:
                    parts = [block_diag(pb)] + parts
                res = jnp.dot(pb, jnp.concatenate(parts, axis=1), preferred_element_type=F32)
                off = GDN_W if stage < 5 else 0
                du = res[:, off:off + GDN_W]
                dw = res[:, off + GDN_W:]
                item[1] = res[:, :GDN_W] if stage < 5 else None
                item[2] = ru - du if stage == 0 else ru + du
                item[3] = rw - dw if stage == 0 else rw + dw
        for unit, _, ru, rw in work:
            u_s[unit] = ru
            w_s[unit] = rw.astype(BF16)

    per_step = 4 if nch % 4 == 0 else 2

    def solve_step(i, carry):
        solve_units([(direction, per_step * i + j) for j in range(per_step) for direction in range(2)])
        return carry

    lax.fori_loop(0, nch // per_step, solve_step, 0)

    def advance(chunks):
        first = []
        for direction, ch in enumerate(chunks):
            unit = direction * nch + ch
            state = st_s[direction]
            res = jnp.dot(jnp.concatenate([w_s[unit], qg_s[unit]], axis=0), state.astype(BF16),
                          preferred_element_type=F32)
            v_new = (u_s[unit] - res[:CHUNK]).astype(BF16)
            first.append((unit, state, res[CHUNK:], v_new))
        for direction, (unit, state, o_state, v_new) in enumerate(first):
            r0 = pl.multiple_of(chunks[direction] * CHUNK, CHUNK)
            o = o_state + jnp.dot(qk_s[unit], block_diag(v_new), preferred_element_type=F32)
            upd = lax.dot_general(kt_s[unit], v_new, (((0,), (0,)), ((), ())), preferred_element_type=F32)
            st_s[direction] = state * eg_s[unit, 0:1, :] + jnp.where(same_head, upd, 0.0)
            o_s[pl.ds(r0, CHUNK), :] = o_s[pl.ds(r0, CHUNK), :] + o

    def ctx_step(s, carry):
        advance((s, nc - 1 - s))
        return carry

    def lat_step(s, carry):
        advance((nc + s, nch - 1 - s))
        return carry

    lax.fori_loop(0, nc, ctx_step, 0)
    lax.fori_loop(0, nl, lat_step, 0)

    ng = ng_ref[0]

    def rms(s, i):
        return s * lax.rsqrt(_group_mean_square(s) + EPS) * ng[:, i * LANES:(i + 1) * LANES]

    for r0 in range(0, t, rt):
        y = _per_head(rms, o_s[r0:r0 + rt, :]) * _silu(gate_ref[r0:r0 + rt, :])
        out_ref[r0:r0 + rt, :] = y.astype(BF16)


def _gdn(layer, qkv, gate, ab, abt, conv_w, alog_r, dt_r, alog_c, dt_c, ng, dims):
    b, c, s, d, tm, nt = dims
    t = c + s
    n = qkv.shape[0]
    lay = lambda shape: pl.BlockSpec(shape, lambda i: (layer,) + (0,) * (len(shape) - 1))
    nch = t // CHUNK
    assert nch % 2 == 0
    once = pl.Buffered(1)
    return pl.pallas_call(
        functools.partial(_gdn_kernel, c, t, tm),
        grid=(b,),
        in_specs=[pl.BlockSpec((t, P_QKV), lambda i: (i, 0), pipeline_mode=once),
                  pl.BlockSpec((t, P_GATE), lambda i: (i, 0), pipeline_mode=once),
                  pl.BlockSpec((t, P_AB), lambda i: (i, 0)),
                  pl.BlockSpec((16, t), lambda i: (0, i)),
                  lay((1, CONV_K, P_QKV)), lay((1, 1, LANES)), lay((1, 1, LANES)),
                  lay((1, 16, LANES)), lay((1, 16, LANES)), lay((1, 1, GDN_W))],
        out_specs=pl.BlockSpec((t, GDN_W), lambda i: (i, 0)),
        out_shape=jax.ShapeDtypeStruct((n, GDN_W), BF16),
        scratch_shapes=[pltpu.VMEM((t + 2 * PAD_ROWS, P_QKV), F32),
                        pltpu.VMEM((t, GDN_W), F32), pltpu.VMEM((t, GDN_W), F32), pltpu.VMEM((t, GDN_W), F32),
                        pltpu.VMEM((t, LANES), F32), pltpu.VMEM((t, LANES), F32), pltpu.VMEM((t, LANES), F32),
                        pltpu.VMEM((nch, SUBLANES, GDN_W), F32),
                        pltpu.VMEM((2 * nch, CHUNK, GDN_W), F32),
                        pltpu.VMEM((2 * nch, CHUNK, GDN_W), BF16), pltpu.VMEM((2 * nch, CHUNK, GDN_W), BF16),
                        pltpu.VMEM((2 * nch, CHUNK, GDN_W), BF16), pltpu.VMEM((2 * nch, CHUNK, GDN_W), BF16),
                        pltpu.VMEM((2 * nch, SUBLANES, GDN_W), F32),
                        pltpu.VMEM((2, GDN_W, GDN_W), F32),
                        pltpu.VMEM((t, GDN_W), F32)],
        compiler_params=_cparams(("arbitrary",)),
        name="gdn",
    )(qkv, gate, ab, abt, conv_w, alog_r, dt_r, alog_c, dt_c, ng)


def _lru_kernel(c, t, rt, p_ref, cw_ref, cb_ref, wr_ref, br_ref, wi_ref, bi_ref, lam_ref, out_ref,
                xp_ref, a_s, b_s, h_s):
    _fill_padded(xp_ref, p_ref[:, :LRU_W], t)
    cw = cw_ref[0]
    cb = cb_ref[0]
    for r0 in range(0, t, rt):
        xr = _conv_rows(xp_ref, cw, r0, rt, c) + cb
        for dirn in range(2):
            r = _sigmoid(_mm(xr, wr_ref[0, dirn]) + br_ref[0, dirn:dirn + 1, :])
            i = _sigmoid(_mm(xr, wi_ref[0, dirn]) + bi_ref[0, dirn:dirn + 1, :])
            log_a = (-LRU_C) * r * _softplus(-lam_ref[0, dirn:dirn + 1, :])
            a = jnp.exp(log_a)
            a_s[dirn, r0:r0 + rt, :] = a
            b_s[dirn, r0:r0 + rt, :] = jnp.sqrt(1.0 - a * a) * (i * xr)

    row = lax.broadcasted_iota(jnp.int32, (SUBLANES, LRU_W), 0)

    def tile_scan(dirn, i, h_prev):
        r0 = pl.multiple_of(i * SUBLANES, SUBLANES)
        a = a_s[dirn, pl.ds(r0, SUBLANES), :]
        bx = b_s[dirn, pl.ds(r0, SUBLANES), :]
        for sh in (1, 2, 4):
            if dirn == 0:
                a_sh, b_sh, m = pltpu.roll(a, sh, 0), pltpu.roll(bx, sh, 0), row >= sh
            else:
                a_sh, b_sh, m = (pltpu.roll(a, SUBLANES - sh, 0), pltpu.roll(bx, SUBLANES - sh, 0),
                                 row < SUBLANES - sh)
            bx = jnp.where(m, a * b_sh + bx, bx)
            a = jnp.where(m, a * a_sh, a)
        h = a * h_prev + bx
        h_s[dirn, pl.ds(r0, SUBLANES), :] = h
        return h[SUBLANES - 1:SUBLANES, :] if dirn == 0 else h[0:1, :]

    n_t = t // SUBLANES
    n_c = c // SUBLANES
    zero = jnp.zeros((1, LRU_W), F32)

    def fwd_step(i, carry):
        hf, hb = carry
        hf = tile_scan(0, i, hf)
        hb = tile_scan(1, jnp.where(i < n_c, n_c - 1 - i, n_t - 1 - (i - n_c)), hb)
        return hf, hb

    lax.fori_loop(0, n_t, fwd_step, (zero, zero))

    for r0 in range(0, t, rt):
        yb = p_ref[r0:r0 + rt, LRU_W:]
        gelu = 0.5 * yb * (1.0 + jnp.tanh(0.7978845608028654 * (yb + 0.044715 * (yb * yb * yb))))
        out_ref[r0:r0 + rt, :] = ((h_s[0, r0:r0 + rt, :] + h_s[1, r0:r0 + rt, :]) * gelu).astype(BF16)


def _lru(layer, p, conv_w, conv_b, wr, br, wi, bi, lam, dims):
    b, c, s, d, tm, nt = dims
    t = c + s
    n = p.shape[0]
    lay = lambda shape: pl.BlockSpec(shape, lambda i: (layer,) + (0,) * (len(shape) - 1))
    return pl.pallas_call(
        functools.partial(_lru_kernel, c, t, tm),
        grid=(b,),
        in_specs=[pl.BlockSpec((t, P_LRU), lambda i: (i, 0)),
                  lay((1, CONV_K, LRU_W)), lay((1, 1, LRU_W)),
                  lay((1, 2, LRU_W, LRU_W)), lay((1, 2, LRU_W)),
                  lay((1, 2, LRU_W, LRU_W)), lay((1, 2, LRU_W)), lay((1, 2, LRU_W))],
        out_specs=pl.BlockSpec((t, LRU_W), lambda i: (i, 0)),
        out_shape=jax.ShapeDtypeStruct((n, LRU_W), BF16),
        scratch_shapes=[pltpu.VMEM((t + 2 * PAD_ROWS, LRU_W), F32),
                        pltpu.VMEM((2, t, LRU_W), F32), pltpu.VMEM((2, t, LRU_W), F32),
                        pltpu.VMEM((2, t, LRU_W), F32)],
        compiler_params=_cparams(("arbitrary",)),
        name="lru",
    )(p, conv_w, conv_b, wr, br, wi, bi, lam)


def _attn_rows(q, k, v):
    outs = []
    for hq in range(ATT_W // HEAD):
        kv = hq // ATT_GROUP
        qh = q[:, hq * HEAD:(hq + 1) * HEAD]
        kh = k[:, kv * HEAD:(kv + 1) * HEAD]
        vh = v[:, kv * HEAD:(kv + 1) * HEAD]
        s = lax.dot_general(qh, kh, (((1,), (1,)), ((), ())), preferred_element_type=F32)
        m = jnp.max(s, axis=-1, keepdims=True)
        p = jnp.exp(s - m)
        l = jnp.sum(p, axis=-1, keepdims=True)
        o = jnp.dot(p.astype(BF16), vh, preferred_element_type=F32)
        outs.append(o * (1.0 / l))
    return jnp.concatenate(outs, axis=1)


def _attn_kernel(c, with_ctx, q_ref, k_ref, v_ref, o_ref):
    def latent():
        o_ref[...] = _attn_rows(q_ref[...], k_ref[...], v_ref[...]).astype(BF16)

    if not with_ctx:
        latent()
        return
    j = pl.program_id(1)

    @pl.when(j == 0)
    def _():
        o_ref[...] = _attn_rows(q_ref[...], k_ref[0:c, :], v_ref[0:c, :]).astype(BF16)

    pl.when(j > 0)(latent)


def _attn(q, k, v, with_ctx, dims):
    b, c, s, d, tm, nt = dims
    t = c + s
    n = q.shape[0]
    off = 0 if with_ctx else 1
    return pl.pallas_call(
        functools.partial(_attn_kernel, c, with_ctx),
        grid=(b, nt - off),
        in_specs=[pl.BlockSpec((tm, ATT_W), lambda i, j: (i * nt + j + off, 0)),
                  pl.BlockSpec((t, ATT_KV_W), lambda i, j: (i, 0)),
                  pl.BlockSpec((t, ATT_KV_W), lambda i, j: (i, 0))],
        out_specs=pl.BlockSpec((tm, ATT_W), lambda i, j: (i * (nt - off) + j, 0)),
        out_shape=jax.ShapeDtypeStruct((b * (nt - off) * tm, ATT_W), BF16),
        compiler_params=_cparams(("arbitrary", "arbitrary")),
        name="attn",
    )(q, k, v)


R_E1, R_E2, R_G1, R_G2, R_RANK1, R_RANK2 = range(6)


def _outproj_kernel(tm, a_ref, b_ref, c_ref, x_ref, g1_ref, sh_ref, sc_ref, ng_ref, wo_ref, wr_ref, br_ref,
                    xo_ref, h_ref, route_ref, cnt_ref, carry_s):
    i = pl.program_id(0)

    @pl.when(i == 0)
    def _():
        carry_s[...] = jnp.zeros(carry_s.shape, F32)

    mix = (jnp.dot(a_ref[...], wo_ref[0, 0:GDN_W, :], preferred_element_type=F32)
           + jnp.dot(b_ref[...], wo_ref[0, GDN_W:GDN_W + LRU_W, :], preferred_element_type=F32)
           + jnp.dot(c_ref[...], wo_ref[0, GDN_W + LRU_W:, :], preferred_element_type=F32))
    x = x_ref[...] + g1_ref[0] * mix
    xo_ref[...] = x
    ms = jnp.mean(x * x, axis=-1, keepdims=True)
    h = (x * lax.rsqrt(ms + EPS) * ng_ref[0]) * (1.0 + sc_ref[0]) + sh_ref[0]
    h_ref[...] = h

    h_hi = h.astype(BF16)
    h_lo = (h - h_hi.astype(F32)).astype(BF16)
    logits = jnp.dot(jnp.concatenate([h_hi, h_lo, h_hi], axis=1), wr_ref[0],
                     preferred_element_type=F32) + br_ref[0]
    lane_i = lax.broadcasted_iota(jnp.int32, logits.shape, 1)
    lane = lane_i.astype(F32)
    lane_group = jnp.right_shift(lane_i, 3).astype(F32)
    big = jnp.float32(4 * LANES)
    neg = jnp.float32(-jnp.inf)
    is_group = (lane_i >= N_EXPERTS) & (lane_i < N_EXPERTS + N_GROUPS)
    gl = jnp.where(is_group, logits, neg)
    gm = jnp.max(gl, axis=-1, keepdims=True)
    pg_top = 1.0 / jnp.sum(jnp.where(is_group, jnp.exp(gl - gm), 0.0), axis=-1, keepdims=True)
    g_idx = jnp.min(jnp.where(gl == gm, lane, big), axis=-1, keepdims=True) - N_EXPERTS
    in_group = (lane_i < N_EXPERTS) & (lane_group == g_idx)
    le = jnp.where(in_group, logits, neg)
    m1 = jnp.max(le, axis=-1, keepdims=True)
    e1 = jnp.min(jnp.where(le == m1, lane, big), axis=-1, keepdims=True)
    le2 = jnp.where(lane == e1, neg, le)
    m2 = jnp.max(le2, axis=-1, keepdims=True)
    e2 = jnp.min(jnp.where(le2 == m2, lane, big), axis=-1, keepdims=True)
    z = jnp.sum(jnp.where(in_group, jnp.exp(le - m1), 0.0), axis=-1, keepdims=True)
    pe1 = 1.0 / z
    pe2 = jnp.exp(m2 - m1) / z
    gate1 = pg_top * pe1 / (pe1 + pe2)
    gate2 = pg_top * pe2 / (pe1 + pe2)

    sel1 = lane == e1
    sel2 = lane == e2
    onehot = jnp.where(sel1 | sel2, 1.0, 0.0)
    ri = lax.broadcasted_iota(jnp.int32, (tm, tm), 0)
    ci = lax.broadcasted_iota(jnp.int32, (tm, tm), 1)
    before = _mm(jnp.where(ri > ci, 1.0, 0.0), onehot) + carry_s[0:1, :]
    rank1 = jnp.sum(jnp.where(sel1, before, 0.0), axis=-1, keepdims=True)
    rank2 = jnp.sum(jnp.where(sel2, before, 0.0), axis=-1, keepdims=True)
    carry = carry_s[0:1, :] + jnp.sum(onehot, axis=0, keepdims=True)
    carry_s[...] = jnp.broadcast_to(carry, carry_s.shape)
    cnt_ref[...] = jnp.broadcast_to(carry, cnt_ref.shape)

    route = jnp.zeros(logits.shape, F32)
    for col, val in ((R_E1, e1), (R_E2, e2), (R_G1, gate1), (R_G2, gate2),
                     (R_RANK1, rank1), (R_RANK2, rank2)):
        route = jnp.where(lane_i == col, val, route)
    route_ref[...] = route


def _tile_map(with_ctx, nt):
    if with_ctx:
        return lambda i: i
    return lambda i: (i // (nt - 1)) * nt + 1 + i % (nt - 1)


def _outproj(layer, a, bm, cm, x, mod, norm2_g, w_out, w_route, b_route, with_ctx, dims):
    b, c, s, d, tm, nt = dims
    n = x.shape[0]
    n_layers = norm2_g.shape[0]
    rows = mod.shape[0] // (6 * n_layers)
    tile = _tile_map(with_ctx, nt)
    n_tiles = b * (nt if with_ctx else nt - 1)

    def mod_idx(chunk):
        def f(i):
            ti = tile(i)
            r = jnp.where(ti % nt == 0, b, ti // nt)
            return ((layer * 6 + chunk) * rows + r, 0, 0)
        return f

    row_spec = lambda w: pl.BlockSpec((tm, w), lambda i: (tile(i), 0))
    compact = lambda w: pl.BlockSpec((tm, w), lambda i: (i, 0))
    lay = lambda shape: pl.BlockSpec(shape, lambda i: (layer,) + (0,) * (len(shape) - 1))
    n_moe = n_tiles * tm
    return pl.pallas_call(
        functools.partial(_outproj_kernel, tm),
        grid=(n_tiles,),
        in_specs=[row_spec(GDN_W), row_spec(LRU_W), compact(ATT_W), row_spec(d),
                  pl.BlockSpec((1, 1, d), mod_idx(2)), pl.BlockSpec((1, 1, d), mod_idx(3)),
                  pl.BlockSpec((1, 1, d), mod_idx(4)),
                  lay((1, 1, d)), lay((1, d, d)), lay((1, 3 * d, LANES)), lay((1, 1, LANES))],
        out_specs=[row_spec(d), compact(d), compact(LANES),
                   pl.BlockSpec((SUBLANES, LANES), lambda i: (0, 0))],
        out_shape=[jax.ShapeDtypeStruct((n, d), F32), jax.ShapeDtypeStruct((n_moe, d), F32),
                   jax.ShapeDtypeStruct((n_moe, LANES), F32), jax.ShapeDtypeStruct((SUBLANES, LANES), F32)],
        scratch_shapes=[pltpu.VMEM((SUBLANES, LANES), F32)],
        input_output_aliases={3: 0},
        compiler_params=_cparams(("arbitrary",)),
        name="outproj",
    )(a, bm, cm, x, mod, mod, mod, norm2_g, w_out, w_route, b_route)


def _dest_kernel(route_ref, cnt_ref, dest_ref):
    route = route_ref[...]
    lane = lax.broadcasted_iota(jnp.int32, route.shape, 1)
    counts = cnt_ref[...]
    padded = jnp.floor((counts + (MOE_BLOCK - 1)) * (1.0 / MOE_BLOCK)) * MOE_BLOCK
    li = lax.broadcasted_iota(jnp.int32, (LANES, LANES), 0)
    lj = lax.broadcasted_iota(jnp.int32, (LANES, LANES), 1)
    start = _mm_exact(padded, jnp.where((li < lj) & (li < N_EXPERTS), 1.0, 0.0))[0:1, :]

    def col(j):
        return jnp.sum(jnp.where(lane == j, route, 0.0), axis=-1, keepdims=True)

    def slot(e, rank):
        return jnp.sum(jnp.where(lane == e.astype(jnp.int32), start, 0.0), axis=-1, keepdims=True) + rank

    d1 = slot(col(R_E1), col(R_RANK1))
    d2 = slot(col(R_E2), col(R_RANK2))
    dest_ref[...] = jnp.where(lane == 0, d1, jnp.where(lane == 1, d2, 0.0)).astype(jnp.int32)


def _dest(route, counts, dims):
    tm = dims[4]
    n = route.shape[0]
    return pl.pallas_call(
        _dest_kernel,
        grid=(n // tm,),
        in_specs=[pl.BlockSpec((tm, LANES), lambda i: (i, 0)),
                  pl.BlockSpec((SUBLANES, LANES), lambda i: (0, 0))],
        out_specs=pl.BlockSpec((tm, LANES), lambda i: (i, 0)),
        out_shape=jax.ShapeDtypeStruct((n, LANES), jnp.int32),
        compiler_params=_cparams(("arbitrary",)),
        name="dest",
    )(route, counts)


def _invert_kernel(n, d1_ref, d2_ref, slot_ref):
    def clear(s, carry):
        slot_ref[s] = 2 * n + (s & (SPARE_ROWS - 1))
        return carry

    lax.fori_loop(0, slot_ref.shape[0], clear, 0, unroll=8)

    def place(t, carry):
        slot_ref[d1_ref[t]] = t
        slot_ref[d2_ref[t]] = n + t
        return carry

    lax.fori_loop(0, n, place, 0, unroll=8)


def _invert(d1, d2, n_slots):
    n = d1.shape[0]
    return pl.pallas_call(
        functools.partial(_invert_kernel, n),
        grid_spec=pltpu.PrefetchScalarGridSpec(
            num_scalar_prefetch=2,
            grid=(1,),
            in_specs=[],
            out_specs=pl.BlockSpec(memory_space=pltpu.SMEM)),
        out_shape=jax.ShapeDtypeStruct((n_slots,), jnp.int32),
        compiler_params=_cparams(("arbitrary",)),
        name="invert",
    )(d1, d2)


SPARE_ROWS = 2 * MOE_BLOCK


def _expert_kernel(be_ref, nact_ref, slot_ref, h_ref, w1_ref, w3_ref, w2_ref, y_ref,
                   w1_s, w3_s, w2_s, x_s, y_s, gsem, ssem):
    i = pl.program_id(0)
    nact = nact_ref[0]
    n = h_ref.shape[0]
    half = MOE_BLOCK // 2

    def gather_rows(blk, buf, rows):
        for r in rows:
            dst = slot_ref[blk * MOE_BLOCK + r]
            tok = jnp.minimum(jnp.where(dst >= n, dst - n, dst), n - 1)
            pltpu.make_async_copy(h_ref.at[pl.ds(tok, 1)], x_s.at[buf, pl.ds(r, 1)], gsem.at[buf]).start()

    def scatter_rows(blk, buf, rows):
        for r in rows:
            dst = slot_ref[blk * MOE_BLOCK + r]
            pltpu.make_async_copy(y_s.at[buf, pl.ds(r, 1)], y_ref.at[pl.ds(dst, 1)], ssem.at[buf]).start()

    def wait_block(bufs, sem, buf):
        pltpu.make_async_copy(bufs.at[buf], bufs.at[buf], sem.at[buf]).wait()

    def compute(buf, between):
        x = x_s[buf].astype(BF16)
        between[0]()
        h1 = jnp.dot(x, w1_s[...], preferred_element_type=F32)
        between[1]()
        h3 = jnp.dot(x, w3_s[...], preferred_element_type=F32)
        between[2]()
        act = (_silu(h1) * h3).astype(BF16)
        y = jnp.dot(act, w2_s[...], preferred_element_type=F32)
        between[3]()
        return y

    @pl.when(i == 0)
    def _():
        y_s[...] = jnp.zeros(y_s.shape, F32)
        for buf in range(2):
            spare = pltpu.make_async_copy(y_s.at[buf], y_ref.at[pl.ds(2 * n + buf * MOE_BLOCK, MOE_BLOCK)],
                                          ssem.at[buf])
            spare.start()
            spare.wait()

    @pl.when((i < nact) & ((i == 0) | (be_ref[i] != be_ref[jnp.maximum(i - 1, 0)])))
    def _():
        w1_s[...] = w1_ref[0, 0].astype(BF16)
        w3_s[...] = w3_ref[0, 0].astype(BF16)
        w2_s[...] = w2_ref[0, 0].astype(BF16)

    xbuf = i % 3
    steady = (i >= 2) & (i + 2 < nact)
    for par in range(2):
        @pl.when(steady & (i % 2 == par))
        def _(par=par):
            wait_block(x_s, gsem, xbuf)
            nxt = (i + 2) % 3
            issue = [lambda: gather_rows(i + 2, nxt, range(0, half)),
                     lambda: gather_rows(i + 2, nxt, range(half, MOE_BLOCK)),
                     lambda: scatter_rows(i - 1, 1 - par, range(0, half)),
                     lambda: scatter_rows(i - 1, 1 - par, range(half, MOE_BLOCK))]
            y = compute(xbuf, issue)
            wait_block(y_s, ssem, par)
            y_s[par] = y

    @pl.when(jnp.logical_not(steady))
    def _():
        buf = i % 2
        for first in range(2):
            pl.when((i == 0) & (first < nact))(lambda first=first: gather_rows(first, first, range(MOE_BLOCK)))
        pl.when(i + 2 < nact)(lambda: gather_rows(i + 2, (i + 2) % 3, range(MOE_BLOCK)))
        pl.when((i >= 1) & (i - 1 < nact))(lambda: scatter_rows(i - 1, 1 - buf, range(MOE_BLOCK)))
        pl.when((i >= 2) & (i - 2 < nact))(lambda: wait_block(y_s, ssem, buf))

        @pl.when(i < nact)
        def _():
            wait_block(x_s, gsem, xbuf)
            y_s[buf] = compute(xbuf, [lambda: None] * 4)


def _experts(layer, block_e, nact, slot_src, h, w1, w3, w2):
    n = h.shape[0]
    d = w1.shape[-2]
    hid = w1.shape[-1]
    n_steps = block_e.shape[0]
    n_slots = slot_src.shape[0]
    wspec = lambda shape: pl.BlockSpec(shape, lambda i, be, na, sl: (layer, be[i], 0, 0))
    return pl.pallas_call(
        _expert_kernel,
        grid_spec=pltpu.PrefetchScalarGridSpec(
            num_scalar_prefetch=3,
            grid=(n_steps,),
            in_specs=[pl.BlockSpec(memory_space=pl.ANY),
                      wspec((1, 1, d, hid)), wspec((1, 1, d, hid)), wspec((1, 1, hid, d))],
            out_specs=pl.BlockSpec(memory_space=pl.ANY),
            scratch_shapes=[pltpu.VMEM((d, hid), BF16), pltpu.VMEM((d, hid), BF16),
                            pltpu.VMEM((hid, d), BF16),
                            pltpu.VMEM((3, MOE_BLOCK, d), F32), pltpu.VMEM((2, MOE_BLOCK, d), F32),
                            pltpu.SemaphoreType.DMA((3,)), pltpu.SemaphoreType.DMA((2,))]),
        out_shape=jax.ShapeDtypeStruct((2 * n + SPARE_ROWS, d), F32),
        compiler_params=_cparams(("arbitrary",)),
        name="experts",
    )(block_e, nact, slot_src, h, w1, w3, w2)


def _combine_kernel(final, x_ref, route_ref, g2_ref, fg_ref, y1_ref, y2_ref, o_ref):
    route = route_ref[...]
    lane = lax.broadcasted_iota(jnp.int32, route.shape, 1)
    gate1 = jnp.sum(jnp.where(lane == R_G1, route, 0.0), axis=-1, keepdims=True)
    gate2 = jnp.sum(jnp.where(lane == R_G2, route, 0.0), axis=-1, keepdims=True)
    x = x_ref[...] + g2_ref[0] * (y1_ref[...] * gate1 + y2_ref[...] * gate2)
    if final:
        ms = jnp.mean(x * x, axis=-1, keepdims=True)
        x = x * lax.rsqrt(ms + EPS) * fg_ref[...]
    o_ref[...] = x


def _combine(layer, x, route, mod, final_g, y, with_ctx, final, n_layers, dims):
    b, c, s, d, tm, nt = dims
    n = x.shape[0]
    rows = mod.shape[0] // (6 * n_layers)
    tile = _tile_map(with_ctx, nt)
    n_tiles = b * (nt if with_ctx else nt - 1)

    def mod_idx(i):
        ti = tile(i)
        r = jnp.where(ti % nt == 0, b, ti // nt)
        return ((layer * 6 + 5) * rows + r, 0, 0)

    if final:
        out_spec = pl.BlockSpec((tm, d), lambda i: (i, 0))
        out_shape = jax.ShapeDtypeStruct((n_tiles * tm, d), F32)
        aliases = {}
    else:
        out_spec = pl.BlockSpec((tm, d), lambda i: (tile(i), 0))
        out_shape = jax.ShapeDtypeStruct((n, d), F32)
        aliases = {0: 0}
    return pl.pallas_call(
        functools.partial(_combine_kernel, final),
        grid=(n_tiles,),
        in_specs=[pl.BlockSpec((tm, d), lambda i: (tile(i), 0)),
                  pl.BlockSpec((tm, LANES), lambda i: (i, 0)),
                  pl.BlockSpec((1, 1, d), mod_idx),
                  pl.BlockSpec((1, d), lambda i: (0, 0)),
                  pl.BlockSpec((tm, d), lambda i: (i, 0)),
                  pl.BlockSpec((tm, d), lambda i: (n_tiles + i, 0))],
        out_specs=out_spec,
        out_shape=out_shape,
        input_output_aliases=aliases,
        compiler_params=_cparams(("arbitrary",)),
        name="combine",
    )(x, route, mod, final_g, y, y)


def _rope_tables(s, tm):
    rows = s // GRID_W
    row = jnp.repeat(jnp.arange(rows, dtype=F32), GRID_W)
    col = jnp.tile(jnp.arange(GRID_W, dtype=F32), rows)
    axis_dim = HEAD // 2
    inv_freq = ROPE_THETA ** (-jnp.arange(0, axis_dim, 2, dtype=F32) / axis_dim)
    ar = row[:, None] * inv_freq
    ac = col[:, None] * inv_freq
    cos = jnp.concatenate([jnp.cos(ar), jnp.cos(ar), jnp.cos(ac), jnp.cos(ac)], axis=1)
    sin = jnp.concatenate([-jnp.sin(ar), jnp.sin(ar), -jnp.sin(ac), jnp.sin(ac)], axis=1)
    cos = jnp.concatenate([jnp.ones((tm, HEAD), F32), cos], axis=0)
    sin = jnp.concatenate([jnp.zeros((tm, HEAD), F32), sin], axis=0)
    return jnp.tile(cos, (1, 2)), jnp.tile(sin, (1, 2))


def _block_diag(w):
    n_layers = w.shape[0]
    eye = jnp.eye(LRU_BLOCKS, dtype=w.dtype)
    full = jnp.einsum('ldnij,nm->ldnimj', w, eye)
    return full.reshape(n_layers, 2, LRU_W, LRU_W)


def _pad_lanes(a, width=LANES):
    return jnp.pad(a, [(0, 0)] * (a.ndim - 1) + [(0, width - a.shape[-1])])


def kernel(x, c, ctx, c_ctx, w_ada, b_ada, norm1_g, norm2_g, w_in, w_out, gdn_conv_w, gdn_a_log, gdn_dt_bias, gdn_norm_g, lru_conv_w, lru_conv_b, lru_w_r, lru_b_r, lru_w_i, lru_b_i, lru_lambda, attn_q_norm_g, attn_k_norm_g, moe_w_group, moe_b_group, moe_w_expert, moe_b_expert, moe_w1, moe_w3, moe_w2, final_norm_g):
    bsz, s, d = x.shape
    cl = ctx.shape[1]
    n_layers = w_in.shape[0]
    tm = cl
    assert s % tm == 0 and tm % CHUNK == 0 and s % GRID_W == 0 and d == SUBLANES * LANES
    nt = (cl + s) // tm
    dims = (bsz, cl, s, d, tm, nt)

    o1 = 4 * GDN_W + 4 * GDN_H
    o2 = o1 + 2 * LRU_W
    w_ab = w_in[:, :, 4 * GDN_W:o1]
    w_pack = jnp.concatenate([w_in[:, :, :4 * GDN_W], _pad_lanes(w_ab), w_in[:, :, o1:o2], w_in[:, :, o2:]],
                             axis=-1).astype(BF16)
    wabt = jnp.swapaxes(w_ab, 1, 2).astype(BF16)
    gq = jnp.tile(attn_q_norm_g, (1, 2))[:, None, :]
    gk = jnp.tile(attn_k_norm_g, (1, 2))[:, None, :]
    cos_t, sin_t = _rope_tables(s, tm)
    alog = gdn_a_log.reshape(n_layers, 2 * GDN_H)
    dtb = gdn_dt_bias.reshape(n_layers, 2 * GDN_H)
    alog_r = _pad_lanes(alog)[:, None, :]
    dt_r = _pad_lanes(dtb)[:, None, :]
    alog_c = jnp.broadcast_to(_pad_lanes(alog, 16)[:, :, None], (n_layers, 16, LANES))
    dt_c = jnp.broadcast_to(_pad_lanes(dtb, 16)[:, :, None], (n_layers, 16, LANES))
    gdn_ng = jnp.tile(gdn_norm_g, (1, GDN_H))[:, None, :]
    wr_bd = _block_diag(lru_w_r).astype(BF16)
    wi_bd = _block_diag(lru_w_i).astype(BF16)
    w_out_b = w_out.astype(BF16)
    w_route = _pad_lanes(jnp.concatenate([moe_w_expert, moe_w_group], axis=-1))
    w_route_hi = w_route.astype(BF16)
    w_route_lo = (w_route - w_route_hi.astype(F32)).astype(BF16)
    w_route = jnp.concatenate([w_route_hi, w_route_hi, w_route_lo], axis=1)
    b_route = _pad_lanes(jnp.concatenate([moe_b_expert, moe_b_group], axis=-1))[:, None, :]

    cv = jnp.concatenate([c, c_ctx[None, :]], axis=0)
    rows = -(-cv.shape[0] // SUBLANES) * SUBLANES
    cv = jnp.pad(cv, ((0, rows - cv.shape[0]), (0, 0)))
    mod = _adaln(cv, w_ada, b_ada).reshape(n_layers * 6 * rows, 1, d)

    xf = jnp.concatenate([ctx, x], axis=1).reshape(bsz * (cl + s), d)
    out = None
    for layer in range(n_layers):
        with_ctx = layer < n_layers - 1
        qkv, gate, ab, abt, plru, q, k, v = _inproj(
            layer, xf, mod, norm1_g[:, None, :], w_pack, wabt, gq, gk, cos_t, sin_t, dims)
        mix_a = _gdn(layer, qkv, gate, ab, abt, gdn_conv_w, alog_r, dt_r, alog_c, dt_c, gdn_ng, dims)
        mix_b = _lru(layer, plru, lru_conv_w, lru_conv_b[:, None, :], wr_bd, lru_b_r, wi_bd, lru_b_i,
                     lru_lambda, dims)
        mix_c = _attn(q, k, v, with_ctx, dims)
        xf, h, route, counts = _outproj(layer, mix_a, mix_b, mix_c, xf, mod, norm2_g[:, None, :], w_out_b,
                                        w_route, b_route, with_ctx, dims)
        dest = _dest(route, counts, dims)
        d1 = dest[:, 0]
        d2 = dest[:, 1]
        n_tok = bsz * ((cl + s) if with_ctx else s)
        n_blocks = -(-(2 * n_tok) // MOE_BLOCK) + N_EXPERTS
        cnt = counts[0, :N_EXPERTS].astype(jnp.int32)
        padded = (cnt + MOE_BLOCK - 1) // MOE_BLOCK * MOE_BLOCK
        pad_end = jnp.cumsum(padded)
        block_row = jnp.arange(n_blocks + 2, dtype=jnp.int32) * MOE_BLOCK
        block_e = jnp.minimum(jnp.sum((pad_end[None, :] <= block_row[:, None]).astype(jnp.int32), axis=1),
                              N_EXPERTS - 1)
        nact = (pad_end[-1:] // MOE_BLOCK).astype(jnp.int32)
        slot_src = _invert(d1, d2, n_blocks * MOE_BLOCK)
        y = _experts(layer, block_e, nact, slot_src, h, moe_w1, moe_w3, moe_w2)
        res = _combine(layer, xf, route, mod, final_norm_g[None, :], y, with_ctx, not with_ctx, n_layers, dims)
        if with_ctx:
            xf = res
        else:
            out = res
    return out.reshape(bsz, s, d)
```

```python
import functools

import jax
import jax.numpy as jnp
from jax import lax
from jax.experimental import pallas as pl
from jax.experimental.pallas import tpu as pltpu

F32 = jnp.float32
BF16 = jnp.bfloat16
HIGHEST = lax.Precision.HIGHEST

HEAD = 64
GDN_W = 256
GDN_H = GDN_W // HEAD
CHUNK = 64
LRU_W = 256
LRU_BLOCKS = 4
LRU_C = 8.0
ATT_W = 512
ATT_KV_W = 128
ATT_GROUP = 4
CONV_K = 4
N_GROUPS = 8
N_EXPERTS = 64
EXPERTS_PER_GROUP = 8
MOE_HIDDEN = 512
MOE_BLOCK = 128
GRID_W = 64
ROPE_THETA = 10000.0
EPS = 1e-6
LANES = 128
SUBLANES = 8
PAD_ROWS = 8
VMEM_LIMIT = 56 * 1024 * 1024

P_QKV = 3 * GDN_W
P_GATE = GDN_W
P_AB = LANES
P_LRU = 2 * LRU_W
P_ATT = ATT_W + 2 * ATT_KV_W
P_ALL = P_QKV + P_GATE + P_AB + P_LRU + P_ATT


def _cparams(sem):
    return pltpu.CompilerParams(dimension_semantics=sem, vmem_limit_bytes=VMEM_LIMIT)


def _sigmoid(x):
    return 1.0 / (1.0 + jnp.exp(-x))


def _silu(x):
    return x * _sigmoid(x)


def _softplus(x):
    return jnp.maximum(x, 0.0) + jnp.log(1.0 + jnp.exp(-jnp.abs(x)))


def _mm(a, b):
    return jnp.dot(a.astype(BF16), b.astype(BF16), preferred_element_type=F32)


def _mm_nt(a, b):
    return lax.dot_general(a.astype(BF16), b.astype(BF16), (((1,), (1,)), ((), ())),
                           preferred_element_type=F32)


def _mm_tn(a, b):
    return lax.dot_general(a.astype(BF16), b.astype(BF16), (((0,), (0,)), ((), ())),
                           preferred_element_type=F32)


def _mm_exact(a, b):
    return jnp.dot(a, b, precision=HIGHEST, preferred_element_type=F32)


def _group_mean_square(x):
    lane = lax.broadcasted_iota(jnp.int32, x.shape, 1)
    lo = lane < HEAD
    x2 = x * x
    s_lo = jnp.sum(jnp.where(lo, x2, 0.0), axis=-1, keepdims=True)
    s_hi = jnp.sum(jnp.where(lo, 0.0, x2), axis=-1, keepdims=True)
    return jnp.where(lo, s_lo, s_hi) * (1.0 / HEAD)


def _per_head(fn, x):
    n = x.shape[1] // LANES
    return jnp.concatenate([fn(x[:, i * LANES:(i + 1) * LANES], i) for i in range(n)], axis=1)


def _adaln_kernel(cv_ref, w_ref, b_ref, o_ref):
    o_ref[0] = _mm_exact(_silu(cv_ref[...]), w_ref[0]) + b_ref[0]


def _adaln(cv, w_ada, b_ada):
    n_layers, d, _ = w_ada.shape
    rows = cv.shape[0]
    return pl.pallas_call(
        _adaln_kernel,
        grid=(n_layers, 6),
        in_specs=[pl.BlockSpec((rows, d), lambda l, j: (0, 0)),
                  pl.BlockSpec((1, d, d), lambda l, j: (l, 0, j)),
                  pl.BlockSpec((1, 1, d), lambda l, j: (l * 6 + j, 0, 0))],
        out_specs=pl.BlockSpec((1, rows, d), lambda l, j: (l * 6 + j, 0, 0)),
        out_shape=jax.ShapeDtypeStruct((n_layers * 6, rows, d), F32),
        compiler_params=_cparams(("arbitrary", "arbitrary")),
        name="adaln",
    )(cv, w_ada, b_ada.reshape(n_layers * 6, 1, d))


def _inproj_kernel(x_ref, sh_ref, sc_ref, g_ref, w_ref, wabt_ref, gq_ref, gk_ref, cos_ref, sin_ref,
                   qkv_ref, gate_ref, ab_ref, abt_ref, lru_ref, q_ref, k_ref, v_ref):
    x = x_ref[...]
    ms = jnp.mean(x * x, axis=-1, keepdims=True)
    h = (x * lax.rsqrt(ms + EPS) * g_ref[0]) * (1.0 + sc_ref[0]) + sh_ref[0]
    hb = h.astype(BF16)
    p = jnp.dot(hb, w_ref[0], preferred_element_type=F32)
    o = 0
    qkv_ref[...] = p[:, o:o + P_QKV]
    o += P_QKV
    gate_ref[...] = p[:, o:o + P_GATE]
    o += P_GATE
    ab_ref[...] = p[:, o:o + P_AB]
    o += P_AB
    lru_ref[...] = p[:, o:o + P_LRU]
    o += P_LRU
    att = p[:, o:o + P_ATT]
    abt_ref[...] = lax.dot_general(wabt_ref[0], hb, (((1,), (1,)), ((), ())), preferred_element_type=F32)

    cos = cos_ref[...]
    sin = sin_ref[...]
    lane = lax.broadcasted_iota(jnp.int32, cos.shape, 1)
    first_half = (lane & 16) == 0

    def norm_rope(gain):
        def fn(s, _):
            y = s * lax.rsqrt(_group_mean_square(s) + EPS) * gain
            swapped = jnp.where(first_half, pltpu.roll(y, LANES - 16, 1), pltpu.roll(y, 16, 1))
            return y * cos + swapped * sin
        return fn

    q = _per_head(norm_rope(gq_ref[0]), att[:, :ATT_W])
    k = _per_head(norm_rope(gk_ref[0]), att[:, ATT_W:ATT_W + ATT_KV_W])
    q_ref[...] = (q * (HEAD ** -0.5)).astype(BF16)
    k_ref[...] = k.astype(BF16)
    v_ref[...] = att[:, ATT_W + ATT_KV_W:].astype(BF16)


def _inproj(layer, x, mod, norm1_g, w_pack, wabt, gq, gk, cos_t, sin_t, dims):
    b, c, s, d, tm, nt = dims
    n = x.shape[0]
    rows = mod.shape[0] // (6 * norm1_g.shape[0])

    def mod_idx(chunk):
        def f(i):
            r = jnp.where(i % nt == 0, b, i // nt)
            return ((layer * 6 + chunk) * rows + r, 0, 0)
        return f

    row_spec = lambda w: pl.BlockSpec((tm, w), lambda i: (i, 0))
    return pl.pallas_call(
        _inproj_kernel,
        grid=(n // tm,),
        in_specs=[row_spec(d),
                  pl.BlockSpec((1, 1, d), mod_idx(0)),
                  pl.BlockSpec((1, 1, d), mod_idx(1)),
                  pl.BlockSpec((1, 1, d), lambda i: (layer, 0, 0)),
                  pl.BlockSpec((1, d, P_ALL), lambda i: (layer, 0, 0)),
                  pl.BlockSpec((1, 16, d), lambda i: (layer, 0, 0)),
                  pl.BlockSpec((1, 1, LANES), lambda i: (layer, 0, 0)),
                  pl.BlockSpec((1, 1, LANES), lambda i: (layer, 0, 0)),
                  pl.BlockSpec((tm, LANES), lambda i: (i % nt, 0)),
                  pl.BlockSpec((tm, LANES), lambda i: (i % nt, 0))],
        out_specs=[row_spec(P_QKV), row_spec(P_GATE), row_spec(P_AB),
                   pl.BlockSpec((16, tm), lambda i: (0, i)),
                   row_spec(P_LRU), row_spec(ATT_W), row_spec(ATT_KV_W), row_spec(ATT_KV_W)],
        out_shape=[jax.ShapeDtypeStruct((n, P_QKV), F32), jax.ShapeDtypeStruct((n, P_GATE), F32),
                   jax.ShapeDtypeStruct((n, P_AB), F32), jax.ShapeDtypeStruct((16, n), F32),
                   jax.ShapeDtypeStruct((n, P_LRU), F32), jax.ShapeDtypeStruct((n, ATT_W), BF16),
                   jax.ShapeDtypeStruct((n, ATT_KV_W), BF16), jax.ShapeDtypeStruct((n, ATT_KV_W), BF16)],
        compiler_params=_cparams(("arbitrary",)),
        name="inproj",
    )(x, mod, mod, norm1_g, w_pack, wabt, gq, gk, cos_t, sin_t)


def _conv_rows(xp_ref, w, r0, rt, c):
    base = r0 + PAD_ROWS
    xm1 = xp_ref[base - 1:base - 1 + rt, :]
    x0 = xp_ref[base:base + rt, :]
    xp1 = xp_ref[base + 1:base + 1 + rt, :]
    xp2 = xp_ref[base + 2:base + 2 + rt, :]
    row = r0 + lax.broadcasted_iota(jnp.int32, (rt, 1), 0)
    xm1 = jnp.where(row == c, 0.0, xm1)
    xp1 = jnp.where(row == c - 1, 0.0, xp1)
    xp2 = jnp.where((row == c - 1) | (row == c - 2), 0.0, xp2)
    return w[0:1, :] * xm1 + w[1:2, :] * x0 + w[2:3, :] * xp1 + w[3:4, :] * xp2


def _fill_padded(xp_ref, x, t):
    zeros = jnp.zeros((PAD_ROWS, xp_ref.shape[1]), F32)
    xp_ref[0:PAD_ROWS, :] = zeros
    xp_ref[PAD_ROWS + t:2 * PAD_ROWS + t, :] = zeros
    xp_ref[PAD_ROWS:PAD_ROWS + t, :] = x


def _gdn_kernel(c, t, rt, qkv_ref, gate_ref, ab_ref, abt_ref, cw_ref, alog_r_ref, dt_r_ref, alog_c_ref,
                dt_c_ref, ng_ref, out_ref,
                xp_ref, q_s, k_s, v_s, g_s, gcf_s, gcb_s, grp_s, u_s, w_s, qk_s, qg_s, kt_s, eg_s, st_s, o_s):
    nch = t // CHUNK
    nc = c // CHUNK
    nl = nch - nc

    _fill_padded(xp_ref, qkv_ref[...], t)
    cw = cw_ref[0]

    def l2n(s, _):
        return s * lax.rsqrt(_group_mean_square(s) * HEAD + EPS)

    for r0 in range(0, t, rt):
        y = _silu(_conv_rows(xp_ref, cw, r0, rt, c))
        q_s[r0:r0 + rt, :] = _per_head(l2n, y[:, :GDN_W]) * (HEAD ** -0.5)
        k_s[r0:r0 + rt, :] = _per_head(l2n, y[:, GDN_W:2 * GDN_W])
        v_s[r0:r0 + rt, :] = y[:, 2 * GDN_W:]

    ab = ab_ref[...]
    lane = lax.broadcasted_iota(jnp.int32, ab.shape, 1)
    gval = -jnp.exp(alog_r_ref[0]) * _softplus(ab + dt_r_ref[0])
    g_s[...] = jnp.where(lane < 2 * GDN_H, gval, _sigmoid(ab))
    abt = abt_ref[...]
    g_t = -jnp.exp(alog_c_ref[0][:, 0:1]) * _softplus(abt + dt_c_ref[0][:, 0:1])

    ri = lax.broadcasted_iota(jnp.int32, (CHUNK, CHUNK), 0)
    ci = lax.broadcasted_iota(jnp.int32, (CHUNK, CHUNK), 1)
    lower_f = (ri >= ci).astype(F32)
    upper_f = (ri <= ci).astype(F32)
    row8 = lax.broadcasted_iota(jnp.int32, (2 * GDN_H, CHUNK), 0)
    for ch in range(nch):
        r0 = ch * CHUNK
        gch = g_s[r0:r0 + CHUNK, :]
        gcf_s[r0:r0 + CHUNK, :] = _mm_exact(lower_f, gch)
        gcb_s[r0:r0 + CHUNK, :] = _mm_exact(upper_f, gch)
        gtc = g_t[0:2 * GDN_H, r0:r0 + CHUNK]
        gr = jnp.where(row8 < GDN_H, _mm_exact(gtc, upper_f), _mm_exact(gtc, lower_f))
        for direction in range(2):
            rows = [gr[direction * GDN_H + hd:direction * GDN_H + hd + 1, :] for hd in range(GDN_H)]
            edge = CHUNK - 1 if direction == 0 else 0
            tots = [jnp.broadcast_to(r[:, edge:edge + 1], (1, CHUNK)) for r in rows]
            grp_s[ch, direction:direction + 1, :] = jnp.concatenate(rows, axis=1)
            grp_s[ch, 2 + direction:3 + direction, :] = jnp.concatenate(tots, axis=1)

    st_s[...] = jnp.zeros(st_s.shape, F32)
    o_s[...] = jnp.zeros(o_s.shape, F32)

    ri4 = lax.broadcasted_iota(jnp.int32, (CHUNK, GDN_W), 0)
    li4 = lax.broadcasted_iota(jnp.int32, (CHUNK, GDN_W), 1)
    pos4 = li4 & (CHUNK - 1)
    head_masks = [jnp.where((li4 >= hd * HEAD) & (li4 < (hd + 1) * HEAD), 1.0, 0.0).astype(BF16)
                  for hd in range(GDN_H)]
    lane_lo = lax.broadcasted_iota(jnp.int32, (CHUNK, LANES), 1) < HEAD
    bi = jnp.right_shift(lax.broadcasted_iota(jnp.int32, (GDN_W, GDN_W), 0), 6)
    bj = jnp.right_shift(lax.broadcasted_iota(jnp.int32, (GDN_W, GDN_W), 1), 6)
    same_head = bi == bj

    def block_diag(xb):
        return jnp.concatenate([xb * m for m in head_masks], axis=0)

    def expand4(x, base):
        cols = [jnp.broadcast_to(x[:, base + hd:base + hd + 1], (CHUNK, LANES)) for hd in range(GDN_H)]
        return jnp.concatenate([jnp.where(lane_lo, cols[0], cols[1]), jnp.where(lane_lo, cols[2], cols[3])],
                               axis=1)

    def solve_units(units):
        work = []
        for direction, ch in units:
            r0 = pl.multiple_of(ch * CHUNK, CHUNK)
            unit = direction * nch + ch
            q = q_s[pl.ds(r0, CHUNK), :]
            k = k_s[pl.ds(r0, CHUNK), :]
            v = v_s[pl.ds(r0, CHUNK), :]
            gcx = expand4((gcf_s if direction == 0 else gcb_s)[pl.ds(r0, CHUNK), :], direction * GDN_H)
            betax = expand4(g_s[pl.ds(r0, CHUNK), :], 2 * GDN_H + direction * GDN_H)
            grow = grp_s[ch, direction:direction + 1, :]
            gtot = grp_s[ch, 2 + direction:3 + direction, :]
            incl = (ri4 >= pos4) if direction == 0 else (ri4 <= pos4)
            strict = (ri4 > pos4) if direction == 0 else (ri4 < pos4)
            decay = jnp.exp(jnp.minimum(gcx - grow, 0.0))
            eg = jnp.exp(gcx)
            kb = k * betax
            prod = lax.dot_general(jnp.concatenate([q, kb], axis=0).astype(BF16), block_diag(k.astype(BF16)),
                                   (((1,), (1,)), ((), ())), preferred_element_type=F32)
            qk_s[unit] = jnp.where(incl, prod[:CHUNK] * decay, 0.0).astype(BF16)
            qg_s[unit] = (q * eg).astype(BF16)
            kt_s[unit] = (k * jnp.exp(gtot - gcx)).astype(BF16)
            eg_s[unit] = jnp.broadcast_to(jnp.exp(gtot), (SUBLANES, GDN_W))
            a_mat = jnp.where(strict, prod[CHUNK:] * decay, 0.0)
            work.append([unit, a_mat, v * betax, kb * eg])
        for stage in range(6):
            for item in work:
                _, p, ru, rw = item
                pb = p.astype(BF16)
                parts = [block_diag(ru.astype(BF16)), block_diag(rw.astype(BF16))]
                if stage < 5:
                    parts = [block_diag(pb)] + parts
                res = jnp.dot(pb, jnp.concatenate(parts, axis=1), preferred_element_type=F32)
                off = GDN_W if stage < 5 else 0
                du = res[:, off:off + GDN_W]
                dw = res[:, off + GDN_W:]
                item[1] = res[:, :GDN_W] if stage < 5 else None
                item[2] = ru - du if stage == 0 else ru + du
                item[3] = rw - dw if stage == 0 else rw + dw
        for unit, _, ru, rw in work:
            u_s[unit] = ru
            w_s[unit] = rw.astype(BF16)

    per_step = 4 if nch % 4 == 0 else 2

    def solve_step(i, carry):
        solve_units([(direction, per_step * i + j) for j in range(per_step) for direction in range(2)])
        return carry

    lax.fori_loop(0, nch // per_step, solve_step, 0)

    def advance(chunks):
        first = []
        for direction, ch in enumerate(chunks):
            unit = direction * nch + ch
            state = st_s[direction]
            res = jnp.dot(jnp.concatenate([w_s[unit], qg_s[unit]], axis=0), state.astype(BF16),
                          preferred_element_type=F32)
            v_new = (u_s[unit] - res[:CHUNK]).astype(BF16)
            first.append((unit, state, res[CHUNK:], v_new))
        for direction, (unit, state, o_state, v_new) in enumerate(first):
            r0 = pl.multiple_of(chunks[direction] * CHUNK, CHUNK)
            o = o_state + jnp.dot(qk_s[unit], block_diag(v_new), preferred_element_type=F32)
            upd = lax.dot_general(kt_s[unit], v_new, (((0,), (0,)), ((), ())), preferred_element_type=F32)
            st_s[direction] = state * eg_s[unit, 0:1, :] + jnp.where(same_head, upd, 0.0)
            o_s[pl.ds(r0, CHUNK), :] = o_s[pl.ds(r0, CHUNK), :] + o

    def ctx_step(s, carry):
        advance((s, nc - 1 - s))
        return carry

    def lat_step(s, carry):
        advance((nc + s, nch - 1 - s))
        return carry

    lax.fori_loop(0, nc, ctx_step, 0)
    lax.fori_loop(0, nl, lat_step, 0)

    ng = ng_ref[0]

    def rms(s, i):
        return s * lax.rsqrt(_group_mean_square(s) + EPS) * ng[:, i * LANES:(i + 1) * LANES]

    for r0 in range(0, t, rt):
        y = _per_head(rms, o_s[r0:r0 + rt, :]) * _silu(gate_ref[r0:r0 + rt, :])
        out_ref[r0:r0 + rt, :] = y.astype(BF16)


def _gdn(layer, qkv, gate, ab, abt, conv_w, alog_r, dt_r, alog_c, dt_c, ng, dims):
    b, c, s, d, tm, nt = dims
    t = c + s
    n = qkv.shape[0]
    lay = lambda shape: pl.BlockSpec(shape, lambda i: (layer,) + (0,) * (len(shape) - 1))
    nch = t // CHUNK
    assert nch % 2 == 0
    once = pl.Buffered(1)
    return pl.pallas_call(
        functools.partial(_gdn_kernel, c, t, tm),
        grid=(b,),
        in_specs=[pl.BlockSpec((t, P_QKV), lambda i: (i, 0), pipeline_mode=once),
                  pl.BlockSpec((t, P_GATE), lambda i: (i, 0), pipeline_mode=once),
                  pl.BlockSpec((t, P_AB), lambda i: (i, 0)),
                  pl.BlockSpec((16, t), lambda i: (0, i)),
                  lay((1, CONV_K, P_QKV)), lay((1, 1, LANES)), lay((1, 1, LANES)),
                  lay((1, 16, LANES)), lay((1, 16, LANES)), lay((1, 1, GDN_W))],
        out_specs=pl.BlockSpec((t, GDN_W), lambda i: (i, 0)),
        out_shape=jax.ShapeDtypeStruct((n, GDN_W), BF16),
        scratch_shapes=[pltpu.VMEM((t + 2 * PAD_ROWS, P_QKV), F32),
                        pltpu.VMEM((t, GDN_W), F32), pltpu.VMEM((t, GDN_W), F32), pltpu.VMEM((t, GDN_W), F32),
                        pltpu.VMEM((t, LANES), F32), pltpu.VMEM((t, LANES), F32), pltpu.VMEM((t, LANES), F32),
                        pltpu.VMEM((nch, SUBLANES, GDN_W), F32),
                        pltpu.VMEM((2 * nch, CHUNK, GDN_W), F32),
                        pltpu.VMEM((2 * nch, CHUNK, GDN_W), BF16), pltpu.VMEM((2 * nch, CHUNK, GDN_W), BF16),
                        pltpu.VMEM((2 * nch, CHUNK, GDN_W), BF16), pltpu.VMEM((2 * nch, CHUNK, GDN_W), BF16),
                        pltpu.VMEM((2 * nch, SUBLANES, GDN_W), F32),
                        pltpu.VMEM((2, GDN_W, GDN_W), F32),
                        pltpu.VMEM((t, GDN_W), F32)],
        compiler_params=_cparams(("arbitrary",)),
        name="gdn",
    )(qkv, gate, ab, abt, conv_w, alog_r, dt_r, alog_c, dt_c, ng)


def _lru_kernel(c, t, rt, p_ref, cw_ref, cb_ref, wr_ref, br_ref, wi_ref, bi_ref, lam_ref, out_ref,
                xp_ref, a_s, b_s, h_s):
    _fill_padded(xp_ref, p_ref[:, :LRU_W], t)
    cw = cw_ref[0]
    cb = cb_ref[0]
    for r0 in range(0, t, rt):
        xr = _conv_rows(xp_ref, cw, r0, rt, c) + cb
        for dirn in range(2):
            r = _sigmoid(_mm(xr, wr_ref[0, dirn]) + br_ref[0, dirn:dirn + 1, :])
            i = _sigmoid(_mm(xr, wi_ref[0, dirn]) + bi_ref[0, dirn:dirn + 1, :])
            log_a = (-LRU_C) * r * _softplus(-lam_ref[0, dirn:dirn + 1, :])
            a = jnp.exp(log_a)
            a_s[dirn, r0:r0 + rt, :] = a
            b_s[dirn, r0:r0 + rt, :] = jnp.sqrt(1.0 - a * a) * (i * xr)

    row = lax.broadcasted_iota(jnp.int32, (SUBLANES, LRU_W), 0)

    def tile_scan(dirn, i, h_prev):
        r0 = pl.multiple_of(i * SUBLANES, SUBLANES)
        a = a_s[dirn, pl.ds(r0, SUBLANES), :]
        bx = b_s[dirn, pl.ds(r0, SUBLANES), :]
        for sh in (1, 2, 4):
            if dirn == 0:
                a_sh, b_sh, m = pltpu.roll(a, sh, 0), pltpu.roll(bx, sh, 0), row >= sh
            else:
                a_sh, b_sh, m = (pltpu.roll(a, SUBLANES - sh, 0), pltpu.roll(bx, SUBLANES - sh, 0),
                                 row < SUBLANES - sh)
            bx = jnp.where(m, a * b_sh + bx, bx)
            a = jnp.where(m, a * a_sh, a)
        h = a * h_prev + bx
        h_s[dirn, pl.ds(r0, SUBLANES), :] = h
        return h[SUBLANES - 1:SUBLANES, :] if dirn == 0 else h[0:1, :]

    n_t = t // SUBLANES
    n_c = c // SUBLANES
    zero = jnp.zeros((1, LRU_W), F32)

    def fwd_step(i, carry):
        hf, hb = carry
        hf = tile_scan(0, i, hf)
        hb = tile_scan(1, jnp.where(i < n_c, n_c - 1 - i, n_t - 1 - (i - n_c)), hb)
        return hf, hb

    lax.fori_loop(0, n_t, fwd_step, (zero, zero))

    for r0 in range(0, t, rt):
        yb = p_ref[r0:r0 + rt, LRU_W:]
        gelu = 0.5 * yb * (1.0 + jnp.tanh(0.7978845608028654 * (yb + 0.044715 * (yb * yb * yb))))
        out_ref[r0:r0 + rt, :] = ((h_s[0, r0:r0 + rt, :] + h_s[1, r0:r0 + rt, :]) * gelu).astype(BF16)


def _lru(layer, p, conv_w, conv_b, wr, br, wi, bi, lam, dims):
    b, c, s, d, tm, nt = dims
    t = c + s
    n = p.shape[0]
    lay = lambda shape: pl.BlockSpec(shape, lambda i: (layer,) + (0,) * (len(shape) - 1))
    return pl.pallas_call(
        functools.partial(_lru_kernel, c, t, tm),
        grid=(b,),
        in_specs=[pl.BlockSpec((t, P_LRU), lambda i: (i, 0)),
                  lay((1, CONV_K, LRU_W)), lay((1, 1, LRU_W)),
                  lay((1, 2, LRU_W, LRU_W)), lay((1, 2, LRU_W)),
                  lay((1, 2, LRU_W, LRU_W)), lay((1, 2, LRU_W)), lay((1, 2, LRU_W))],
        out_specs=pl.BlockSpec((t, LRU_W), lambda i: (i, 0)),
        out_shape=jax.ShapeDtypeStruct((n, LRU_W), BF16),
        scratch_shapes=[pltpu.VMEM((t + 2 * PAD_ROWS, LRU_W), F32),
                        pltpu.VMEM((2, t, LRU_W), F32), pltpu.VMEM((2, t, LRU_W), F32),
                        pltpu.VMEM((2, t, LRU_W), F32)],
        compiler_params=_cparams(("arbitrary",)),
        name="lru",
    )(p, conv_w, conv_b, wr, br, wi, bi, lam)


def _attn_rows(q, k, v):
    outs = []
    for hq in range(ATT_W // HEAD):
        kv = hq // ATT_GROUP
        qh = q[:, hq * HEAD:(hq + 1) * HEAD]
        kh = k[:, kv * HEAD:(kv + 1) * HEAD]
        vh = v[:, kv * HEAD:(kv + 1) * HEAD]
        s = lax.dot_general(qh, kh, (((1,), (1,)), ((), ())), preferred_element_type=F32)
        m = jnp.max(s, axis=-1, keepdims=True)
        p = jnp.exp(s - m)
        l = jnp.sum(p, axis=-1, keepdims=True)
        o = jnp.dot(p.astype(BF16), vh, preferred_element_type=F32)
        outs.append(o * (1.0 / l))
    return jnp.concatenate(outs, axis=1)


def _attn_kernel(c, with_ctx, q_ref, k_ref, v_ref, o_ref):
    def latent():
        o_ref[...] = _attn_rows(q_ref[...], k_ref[...], v_ref[...]).astype(BF16)

    if not with_ctx:
        latent()
        return
    j = pl.program_id(1)

    @pl.when(j == 0)
    def _():
        o_ref[...] = _attn_rows(q_ref[...], k_ref[0:c, :], v_ref[0:c, :]).astype(BF16)

    pl.when(j > 0)(latent)


def _attn(q, k, v, with_ctx, dims):
    b, c, s, d, tm, nt = dims
    t = c + s
    n = q.shape[0]
    off = 0 if with_ctx else 1
    return pl.pallas_call(
        functools.partial(_attn_kernel, c, with_ctx),
        grid=(b, nt - off),
        in_specs=[pl.BlockSpec((tm, ATT_W), lambda i, j: (i * nt + j + off, 0)),
                  pl.BlockSpec((t, ATT_KV_W), lambda i, j: (i, 0)),
                  pl.BlockSpec((t, ATT_KV_W), lambda i, j: (i, 0))],
        out_specs=pl.BlockSpec((tm, ATT_W), lambda i, j: (i * (nt - off) + j, 0)),
        out_shape=jax.ShapeDtypeStruct((b * (nt - off) * tm, ATT_W), BF16),
        compiler_params=_cparams(("arbitrary", "arbitrary")),
        name="attn",
    )(q, k, v)


R_E1, R_E2, R_G1, R_G2, R_RANK1, R_RANK2 = range(6)


def _outproj_kernel(tm, a_ref, b_ref, c_ref, x_ref, g1_ref, sh_ref, sc_ref, ng_ref, wo_ref, wr_ref, br_ref,
                    xo_ref, h_ref, route_ref, cnt_ref, carry_s):
    i = pl.program_id(0)

    @pl.when(i == 0)
    def _():
        carry_s[...] = jnp.zeros(carry_s.shape, F32)

    mix = (jnp.dot(a_ref[...], wo_ref[0, 0:GDN_W, :], preferred_element_type=F32)
           + jnp.dot(b_ref[...], wo_ref[0, GDN_W:GDN_W + LRU_W, :], preferred_element_type=F32)
           + jnp.dot(c_ref[...], wo_ref[0, GDN_W + LRU_W:, :], preferred_element_type=F32))
    x = x_ref[...] + g1_ref[0] * mix
    xo_ref[...] = x
    ms = jnp.mean(x * x, axis=-1, keepdims=True)
    h = (x * lax.rsqrt(ms + EPS) * ng_ref[0]) * (1.0 + sc_ref[0]) + sh_ref[0]
    h_ref[...] = h

    h_hi = h.astype(BF16)
    h_lo = (h - h_hi.astype(F32)).astype(BF16)
    logits = jnp.dot(jnp.concatenate([h_hi, h_lo, h_hi], axis=1), wr_ref[0],
                     preferred_element_type=F32) + br_ref[0]
    lane_i = lax.broadcasted_iota(jnp.int32, logits.shape, 1)
    lane = lane_i.astype(F32)
    lane_group = jnp.right_shift(lane_i, 3).astype(F32)
    big = jnp.float32(4 * LANES)
    neg = jnp.float32(-jnp.inf)
    is_group = (lane_i >= N_EXPERTS) & (lane_i < N_EXPERTS + N_GROUPS)
    gl = jnp.where(is_group, logits, neg)
    gm = jnp.max(gl, axis=-1, keepdims=True)
    pg_top = 1.0 / jnp.sum(jnp.where(is_group, jnp.exp(gl - gm), 0.0), axis=-1, keepdims=True)
    g_idx = jnp.min(jnp.where(gl == gm, lane, big), axis=-1, keepdims=True) - N_EXPERTS
    in_group = (lane_i < N_EXPERTS) & (lane_group == g_idx)
    le = jnp.where(in_group, logits, neg)
    m1 = jnp.max(le, axis=-1, keepdims=True)
    e1 = jnp.min(jnp.where(le == m1, lane, big), axis=-1, keepdims=True)
    le2 = jnp.where(lane == e1, neg, le)
    m2 = jnp.max(le2, axis=-1, keepdims=True)
    e2 = jnp.min(jnp.where(le2 == m2, lane, big), axis=-1, keepdims=True)
    z = jnp.sum(jnp.where(in_group, jnp.exp(le - m1), 0.0), axis=-1, keepdims=True)
    pe1 = 1.0 / z
    pe2 = jnp.exp(m2 - m1) / z
    gate1 = pg_top * pe1 / (pe1 + pe2)
    gate2 = pg_top * pe2 / (pe1 + pe2)

    sel1 = lane == e1
    sel2 = lane == e2
    onehot = jnp.where(sel1 | sel2, 1.0, 0.0)
    ri = lax.broadcasted_iota(jnp.int32, (tm, tm), 0)
    ci = lax.broadcasted_iota(jnp.int32, (tm, tm), 1)
    before = _mm(jnp.where(ri > ci, 1.0, 0.0), onehot) + carry_s[0:1, :]
    rank1 = jnp.sum(jnp.where(sel1, before, 0.0), axis=-1, keepdims=True)
    rank2 = jnp.sum(jnp.where(sel2, before, 0.0), axis=-1, keepdims=True)
    carry = carry_s[0:1, :] + jnp.sum(onehot, axis=0, keepdims=True)
    carry_s[...] = jnp.broadcast_to(carry, carry_s.shape)
    cnt_ref[...] = jnp.broadcast_to(carry, cnt_ref.shape)

    route = jnp.zeros(logits.shape, F32)
    for col, val in ((R_E1, e1), (R_E2, e2), (R_G1, gate1), (R_G2, gate2),
                     (R_RANK1, rank1), (R_RANK2, rank2)):
        route = jnp.where(lane_i == col, val, route)
    route_ref[...] = route


def _tile_map(with_ctx, nt):
    if with_ctx:
        return lambda i: i
    return lambda i: (i // (nt - 1)) * nt + 1 + i % (nt - 1)


def _outproj(layer, a, bm, cm, x, mod, norm2_g, w_out, w_route, b_route, with_ctx, dims):
    b, c, s, d, tm, nt = dims
    n = x.shape[0]
    n_layers = norm2_g.shape[0]
    rows = mod.shape[0] // (6 * n_layers)
    tile = _tile_map(with_ctx, nt)
    n_tiles = b * (nt if with_ctx else nt - 1)

    def mod_idx(chunk):
        def f(i):
            ti = tile(i)
            r = jnp.where(ti % nt == 0, b, ti // nt)
            return ((layer * 6 + chunk) * rows + r, 0, 0)
        return f

    row_spec = lambda w: pl.BlockSpec((tm, w), lambda i: (tile(i), 0))
    compact = lambda w: pl.BlockSpec((tm, w), lambda i: (i, 0))
    lay = lambda shape: pl.BlockSpec(shape, lambda i: (layer,) + (0,) * (len(shape) - 1))
    n_moe = n_tiles * tm
    return pl.pallas_call(
        functools.partial(_outproj_kernel, tm),
        grid=(n_tiles,),
        in_specs=[row_spec(GDN_W), row_spec(LRU_W), compact(ATT_W), row_spec(d),
                  pl.BlockSpec((1, 1, d), mod_idx(2)), pl.BlockSpec((1, 1, d), mod_idx(3)),
                  pl.BlockSpec((1, 1, d), mod_idx(4)),
                  lay((1, 1, d)), lay((1, d, d)), lay((1, 3 * d, LANES)), lay((1, 1, LANES))],
        out_specs=[row_spec(d), compact(d), compact(LANES),
                   pl.BlockSpec((SUBLANES, LANES), lambda i: (0, 0))],
        out_shape=[jax.ShapeDtypeStruct((n, d), F32), jax.ShapeDtypeStruct((n_moe, d), F32),
                   jax.ShapeDtypeStruct((n_moe, LANES), F32), jax.ShapeDtypeStruct((SUBLANES, LANES), F32)],
        scratch_shapes=[pltpu.VMEM((SUBLANES, LANES), F32)],
        input_output_aliases={3: 0},
        compiler_params=_cparams(("arbitrary",)),
        name="outproj",
    )(a, bm, cm, x, mod, mod, mod, norm2_g, w_out, w_route, b_route)


def _dest_kernel(route_ref, cnt_ref, dest_ref):
    route = route_ref[...]
    lane = lax.broadcasted_iota(jnp.int32, route.shape, 1)
    counts = cnt_ref[...]
    padded = jnp.floor((counts + (MOE_BLOCK - 1)) * (1.0 / MOE_BLOCK)) * MOE_BLOCK
    li = lax.broadcasted_iota(jnp.int32, (LANES, LANES), 0)
    lj = lax.broadcasted_iota(jnp.int32, (LANES, LANES), 1)
    start = _mm_exact(padded, jnp.where((li < lj) & (li < N_EXPERTS), 1.0, 0.0))[0:1, :]

    def col(j):
        return jnp.sum(jnp.where(lane == j, route, 0.0), axis=-1, keepdims=True)

    def slot(e, rank):
        return jnp.sum(jnp.where(lane == e.astype(jnp.int32), start, 0.0), axis=-1, keepdims=True) + rank

    d1 = slot(col(R_E1), col(R_RANK1))
    d2 = slot(col(R_E2), col(R_RANK2))
    dest_ref[...] = jnp.where(lane == 0, d1, jnp.where(lane == 1, d2, 0.0)).astype(jnp.int32)


def _dest(route, counts, dims):
    tm = dims[4]
    n = route.shape[0]
    return pl.pallas_call(
        _dest_kernel,
        grid=(n // tm,),
        in_specs=[pl.BlockSpec((tm, LANES), lambda i: (i, 0)),
                  pl.BlockSpec((SUBLANES, LANES), lambda i: (0, 0))],
        out_specs=pl.BlockSpec((tm, LANES), lambda i: (i, 0)),
        out_shape=jax.ShapeDtypeStruct((n, LANES), jnp.int32),
        compiler_params=_cparams(("arbitrary",)),
        name="dest",
    )(route, counts)


def _invert_kernel(n, d1_ref, d2_ref, slot_ref):
    def clear(s, carry):
        slot_ref[s] = 2 * n + (s & (SPARE_ROWS - 1))
        return carry

    lax.fori_loop(0, slot_ref.shape[0], clear, 0, unroll=8)

    def place(t, carry):
        slot_ref[d1_ref[t]] = t
        slot_ref[d2_ref[t]] = n + t
        return carry

    lax.fori_loop(0, n, place, 0, unroll=8)


def _invert(d1, d2, n_slots):
    n = d1.shape[0]
    return pl.pallas_call(
        functools.partial(_invert_kernel, n),
        grid_spec=pltpu.PrefetchScalarGridSpec(
            num_scalar_prefetch=2,
            grid=(1,),
            in_specs=[],
            out_specs=pl.BlockSpec(memory_space=pltpu.SMEM)),
        out_shape=jax.ShapeDtypeStruct((n_slots,), jnp.int32),
        compiler_params=_cparams(("arbitrary",)),
        name="invert",
    )(d1, d2)


SPARE_ROWS = 2 * MOE_BLOCK


def _expert_kernel(be_ref, nact_ref, slot_ref, h_ref, w1_ref, w3_ref, w2_ref, y_ref,
                   w1_s, w3_s, w2_s, x_s, y_s, gsem, ssem):
    i = pl.program_id(0)
    nact = nact_ref[0]
    n = h_ref.shape[0]
    half = MOE_BLOCK // 2

    def gather_rows(blk, buf, rows):
        for r in rows:
            dst = slot_ref[blk * MOE_BLOCK + r]
            tok = jnp.minimum(jnp.where(dst >= n, dst - n, dst), n - 1)
            pltpu.make_async_copy(h_ref.at[pl.ds(tok, 1)], x_s.at[buf, pl.ds(r, 1)],
                                  gsem.at[buf]).start(priority=1)

    def scatter_rows(blk, buf, rows):
        for r in rows:
            dst = slot_ref[blk * MOE_BLOCK + r]
            pltpu.make_async_copy(y_s.at[buf, pl.ds(r, 1)], y_ref.at[pl.ds(dst, 1)],
                                  ssem.at[buf]).start(priority=r % 2)

    def wait_block(bufs, sem, buf):
        pltpu.make_async_copy(bufs.at[buf], bufs.at[buf], sem.at[buf]).wait()

    def compute(buf, between):
        x = x_s[buf].astype(BF16)
        between[0]()
        h1 = jnp.dot(x, w1_s[...], preferred_element_type=F32)
        between[1]()
        h3 = jnp.dot(x, w3_s[...], preferred_element_type=F32)
        between[2]()
        act = (_silu(h1) * h3).astype(BF16)
        y = jnp.dot(act, w2_s[...], preferred_element_type=F32)
        between[3]()
        return y

    @pl.when(i == 0)
    def _():
        y_s[...] = jnp.zeros(y_s.shape, F32)
        for buf in range(2):
            spare = pltpu.make_async_copy(y_s.at[buf], y_ref.at[pl.ds(2 * n + buf * MOE_BLOCK, MOE_BLOCK)],
                                          ssem.at[buf])
            spare.start()
            spare.wait()

    @pl.when((i < nact) & ((i == 0) | (be_ref[i] != be_ref[jnp.maximum(i - 1, 0)])))
    def _():
        w1_s[...] = w1_ref[0, 0].astype(BF16)
        w3_s[...] = w3_ref[0, 0].astype(BF16)
        w2_s[...] = w2_ref[0, 0].astype(BF16)

    xbuf = i % 3
    steady = (i >= 2) & (i + 2 < nact)
    for par in range(2):
        @pl.when(steady & (i % 2 == par))
        def _(par=par):
            wait_block(x_s, gsem, xbuf)
            nxt = (i + 2) % 3
            issue = [lambda: gather_rows(i + 2, nxt, range(0, half)),
                     lambda: gather_rows(i + 2, nxt, range(half, MOE_BLOCK)),
                     lambda: scatter_rows(i - 1, 1 - par, range(0, half)),
                     lambda: scatter_rows(i - 1, 1 - par, range(half, MOE_BLOCK))]
            y = compute(xbuf, issue)
            wait_block(y_s, ssem, par)
            y_s[par] = y

    @pl.when(jnp.logical_not(steady))
    def _():
        buf = i % 2
        for first in range(2):
            pl.when((i == 0) & (first < nact))(lambda first=first: gather_rows(first, first, range(MOE_BLOCK)))
        pl.when(i + 2 < nact)(lambda: gather_rows(i + 2, (i + 2) % 3, range(MOE_BLOCK)))
        pl.when((i >= 1) & (i - 1 < nact))(lambda: scatter_rows(i - 1, 1 - buf, range(MOE_BLOCK)))
        pl.when((i >= 2) & (i - 2 < nact))(lambda: wait_block(y_s, ssem, buf))

        @pl.when(i < nact)
        def _():
            wait_block(x_s, gsem, xbuf)
            y_s[buf] = compute(xbuf, [lambda: None] * 4)


def _experts(layer, block_e, nact, slot_src, h, w1, w3, w2):
    n = h.shape[0]
    d = w1.shape[-2]
    hid = w1.shape[-1]
    n_steps = block_e.shape[0]
    n_slots = slot_src.shape[0]
    wspec = lambda shape: pl.BlockSpec(shape, lambda i, be, na, sl: (layer, be[i], 0, 0))
    return pl.pallas_call(
        _expert_kernel,
        grid_spec=pltpu.PrefetchScalarGridSpec(
            num_scalar_prefetch=3,
            grid=(n_steps,),
            in_specs=[pl.BlockSpec(memory_space=pl.ANY),
                      wspec((1, 1, d, hid)), wspec((1, 1, d, hid)), wspec((1, 1, hid, d))],
            out_specs=pl.BlockSpec(memory_space=pl.ANY),
            scratch_shapes=[pltpu.VMEM((d, hid), BF16), pltpu.VMEM((d, hid), BF16),
                            pltpu.VMEM((hid, d), BF16),
                            pltpu.VMEM((3, MOE_BLOCK, d), F32), pltpu.VMEM((2, MOE_BLOCK, d), F32),
                            pltpu.SemaphoreType.DMA((3,)), pltpu.SemaphoreType.DMA((2,))]),
        out_shape=jax.ShapeDtypeStruct((2 * n + SPARE_ROWS, d), F32),
        compiler_params=_cparams(("arbitrary",)),
        name="experts",
    )(block_e, nact, slot_src, h, w1, w3, w2)


def _combine_kernel(final, x_ref, route_ref, g2_ref, fg_ref, y1_ref, y2_ref, o_ref):
    route = route_ref[...]
    lane = lax.broadcasted_iota(jnp.int32, route.shape, 1)
    gate1 = jnp.sum(jnp.where(lane == R_G1, route, 0.0), axis=-1, keepdims=True)
    gate2 = jnp.sum(jnp.where(lane == R_G2, route, 0.0), axis=-1, keepdims=True)
    x = x_ref[...] + g2_ref[0] * (y1_ref[...] * gate1 + y2_ref[...] * gate2)
    if final:
        ms = jnp.mean(x * x, axis=-1, keepdims=True)
        x = x * lax.rsqrt(ms + EPS) * fg_ref[...]
    o_ref[...] = x


def _combine(layer, x, route, mod, final_g, y, with_ctx, final, n_layers, dims):
    b, c, s, d, tm, nt = dims
    n = x.shape[0]
    rows = mod.shape[0] // (6 * n_layers)
    tile = _tile_map(with_ctx, nt)
    n_tiles = b * (nt if with_ctx else nt - 1)

    def mod_idx(i):
        ti = tile(i)
        r = jnp.where(ti % nt == 0, b, ti // nt)
        return ((layer * 6 + 5) * rows + r, 0, 0)

    if final:
        out_spec = pl.BlockSpec((tm, d), lambda i: (i, 0))
        out_shape = jax.ShapeDtypeStruct((n_tiles * tm, d), F32)
        aliases = {}
    else:
        out_spec = pl.BlockSpec((tm, d), lambda i: (tile(i), 0))
        out_shape = jax.ShapeDtypeStruct((n, d), F32)
        aliases = {0: 0}
    return pl.pallas_call(
        functools.partial(_combine_kernel, final),
        grid=(n_tiles,),
        in_specs=[pl.BlockSpec((tm, d), lambda i: (tile(i), 0)),
                  pl.BlockSpec((tm, LANES), lambda i: (i, 0)),
                  pl.BlockSpec((1, 1, d), mod_idx),
                  pl.BlockSpec((1, d), lambda i: (0, 0)),
                  pl.BlockSpec((tm, d), lambda i: (i, 0)),
                  pl.BlockSpec((tm, d), lambda i: (n_tiles + i, 0))],
        out_specs=out_spec,
        out_shape=out_shape,
        input_output_aliases=aliases,
        compiler_params=_cparams(("arbitrary",)),
        name="combine",
    )(x, route, mod, final_g, y, y)


def _rope_tables(s, tm):
    rows = s // GRID_W
    row = jnp.repeat(jnp.arange(rows, dtype=F32), GRID_W)
    col = jnp.tile(jnp.arange(GRID_W, dtype=F32), rows)
    axis_dim = HEAD // 2
    inv_freq = ROPE_THETA ** (-jnp.arange(0, axis_dim, 2, dtype=F32) / axis_dim)
    ar = row[:, None] * inv_freq
    ac = col[:, None] * inv_freq
    cos = jnp.concatenate([jnp.cos(ar), jnp.cos(ar), jnp.cos(ac), jnp.cos(ac)], axis=1)
    sin = jnp.concatenate([-jnp.sin(ar), jnp.sin(ar), -jnp.sin(ac), jnp.sin(ac)], axis=1)
    cos = jnp.concatenate([jnp.ones((tm, HEAD), F32), cos], axis=0)
    sin = jnp.concatenate([jnp.zeros((tm, HEAD), F32), sin], axis=0)
    return jnp.tile(cos, (1, 2)), jnp.tile(sin, (1, 2))


def _block_diag(w):
    n_layers = w.shape[0]
    eye = jnp.eye(LRU_BLOCKS, dtype=w.dtype)
    full = jnp.einsum('ldnij,nm->ldnimj', w, eye)
    return full.reshape(n_layers, 2, LRU_W, LRU_W)


def _pad_lanes(a, width=LANES):
    return jnp.pad(a, [(0, 0)] * (a.ndim - 1) + [(0, width - a.shape[-1])])


def kernel(x, c, ctx, c_ctx, w_ada, b_ada, norm1_g, norm2_g, w_in, w_out, gdn_conv_w, gdn_a_log, gdn_dt_bias, gdn_norm_g, lru_conv_w, lru_conv_b, lru_w_r, lru_b_r, lru_w_i, lru_b_i, lru_lambda, attn_q_norm_g, attn_k_norm_g, moe_w_group, moe_b_group, moe_w_expert, moe_b_expert, moe_w1, moe_w3, moe_w2, final_norm_g):
    bsz, s, d = x.shape
    cl = ctx.shape[1]
    n_layers = w_in.shape[0]
    tm = cl
    assert s % tm == 0 and tm % CHUNK == 0 and s % GRID_W == 0 and d == SUBLANES * LANES
    nt = (cl + s) // tm
    dims = (bsz, cl, s, d, tm, nt)

    o1 = 4 * GDN_W + 4 * GDN_H
    o2 = o1 + 2 * LRU_W
    w_ab = w_in[:, :, 4 * GDN_W:o1]
    w_pack = jnp.concatenate([w_in[:, :, :4 * GDN_W], _pad_lanes(w_ab), w_in[:, :, o1:o2], w_in[:, :, o2:]],
                             axis=-1).astype(BF16)
    wabt = jnp.swapaxes(w_ab, 1, 2).astype(BF16)
    gq = jnp.tile(attn_q_norm_g, (1, 2))[:, None, :]
    gk = jnp.tile(attn_k_norm_g, (1, 2))[:, None, :]
    cos_t, sin_t = _rope_tables(s, tm)
    alog = gdn_a_log.reshape(n_layers, 2 * GDN_H)
    dtb = gdn_dt_bias.reshape(n_layers, 2 * GDN_H)
    alog_r = _pad_lanes(alog)[:, None, :]
    dt_r = _pad_lanes(dtb)[:, None, :]
    alog_c = jnp.broadcast_to(_pad_lanes(alog, 16)[:, :, None], (n_layers, 16, LANES))
    dt_c = jnp.broadcast_to(_pad_lanes(dtb, 16)[:, :, None], (n_layers, 16, LANES))
    gdn_ng = jnp.tile(gdn_norm_g, (1, GDN_H))[:, None, :]
    wr_bd = _block_diag(lru_w_r).astype(BF16)
    wi_bd = _block_diag(lru_w_i).astype(BF16)
    w_out_b = w_out.astype(BF16)
    w_route = _pad_lanes(jnp.concatenate([moe_w_expert, moe_w_group], axis=-1))
    w_route_hi = w_route.astype(BF16)
    w_route_lo = (w_route - w_route_hi.astype(F32)).astype(BF16)
    w_route = jnp.concatenate([w_route_hi, w_route_hi, w_route_lo], axis=1)
    b_route = _pad_lanes(jnp.concatenate([moe_b_expert, moe_b_group], axis=-1))[:, None, :]

    cv = jnp.concatenate([c, c_ctx[None, :]], axis=0)
    rows = -(-cv.shape[0] // SUBLANES) * SUBLANES
    cv = jnp.pad(cv, ((0, rows - cv.shape[0]), (0, 0)))
    mod = _adaln(cv, w_ada, b_ada).reshape(n_layers * 6 * rows, 1, d)

    xf = jnp.concatenate([ctx, x], axis=1).reshape(bsz * (cl + s), d)
    out = None
    for layer in range(n_layers):
        with_ctx = layer < n_layers - 1
        qkv, gate, ab, abt, plru, q, k, v = _inproj(
            layer, xf, mod, norm1_g[:, None, :], w_pack, wabt, gq, gk, cos_t, sin_t, dims)
        mix_a = _gdn(layer, qkv, gate, ab, abt, gdn_conv_w, alog_r, dt_r, alog_c, dt_c, gdn_ng, dims)
        mix_b = _lru(layer, plru, lru_conv_w, lru_conv_b[:, None, :], wr_bd, lru_b_r, wi_bd, lru_b_i,
                     lru_lambda, dims)
        mix_c = _attn(q, k, v, with_ctx, dims)
        xf, h, route, counts = _outproj(layer, mix_a, mix_b, mix_c, xf, mod, norm2_g[:, None, :], w_out_b,
                                        w_route, b_route, with_ctx, dims)
        dest = _dest(route, counts, dims)
        d1 = dest[:, 0]
        d2 = dest[:, 1]
        n_tok = bsz * ((cl + s) if with_ctx else s)
        n_blocks = -(-(2 * n_tok) // MOE_BLOCK) + N_EXPERTS
        cnt = counts[0, :N_EXPERTS].astype(jnp.int32)
        padded = (cnt + MOE_BLOCK - 1) // MOE_BLOCK * MOE_BLOCK
        pad_end = jnp.cumsum(padded)
        block_row = jnp.arange(n_blocks + 2, dtype=jnp.int32) * MOE_BLOCK
        block_e = jnp.minimum(jnp.sum((pad_end[None, :] <= block_row[:, None]).astype(jnp.int32), axis=1),
                              N_EXPERTS - 1)
        nact = (pad_end[-1:] // MOE_BLOCK).astype(jnp.int32)
        slot_src = _invert(d1, d2, n_blocks * MOE_BLOCK)
        y = _experts(layer, block_e, nact, slot_src, h, moe_w1, moe_w3, moe_w2)
        res = _combine(layer, xf, route, mod, final_norm_g[None, :], y, with_ctx, not with_ctx, n_layers, dims)
        if with_ctx:
            xf = res
        else:
            out = res
    return out.reshape(bsz, s, d)
```

```python
import functools

import jax
import jax.numpy as jnp
from jax import lax
from jax.experimental import pallas as pl
from jax.experimental.pallas import tpu as pltpu

F32 = jnp.float32
BF16 = jnp.bfloat16
HIGHEST = lax.Precision.HIGHEST

HEAD = 64
GDN_W = 256
GDN_H = GDN_W // HEAD
CHUNK = 64
LRU_W = 256
LRU_BLOCKS = 4
LRU_C = 8.0
ATT_W = 512
ATT_KV_W = 128
ATT_GROUP = 4
CONV_K = 4
N_GROUPS = 8
N_EXPERTS = 64
EXPERTS_PER_GROUP = 8
MOE_HIDDEN = 512
MOE_BLOCK = 128
GRID_W = 64
ROPE_THETA = 10000.0
EPS = 1e-6
LANES = 128
SUBLANES = 8
PAD_ROWS = 8
VMEM_LIMIT = 56 * 1024 * 1024

P_QKV = 3 * GDN_W
P_GATE = GDN_W
P_AB = LANES
P_LRU = 2 * LRU_W
P_ATT = ATT_W + 2 * ATT_KV_W
P_ALL = P_QKV + P_GATE + P_AB + P_LRU + P_ATT


def _cparams(sem):
    return pltpu.CompilerParams(dimension_semantics=sem, vmem_limit_bytes=VMEM_LIMIT)


def _sigmoid(x):
    return 1.0 / (1.0 + jnp.exp(-x))


def _silu(x):
    return x * _sigmoid(x)


def _softplus(x):
    return jnp.maximum(x, 0.0) + jnp.log(1.0 + jnp.exp(-jnp.abs(x)))


def _mm(a, b):
    return jnp.dot(a.astype(BF16), b.astype(BF16), preferred_element_type=F32)


def _mm_nt(a, b):
    return lax.dot_general(a.astype(BF16), b.astype(BF16), (((1,), (1,)), ((), ())),
                           preferred_element_type=F32)


def _mm_tn(a, b):
    return lax.dot_general(a.astype(BF16), b.astype(BF16), (((0,), (0,)), ((), ())),
                           preferred_element_type=F32)


def _mm_exact(a, b):
    return jnp.dot(a, b, precision=HIGHEST, preferred_element_type=F32)


def _group_mean_square(x):
    lane = lax.broadcasted_iota(jnp.int32, x.shape, 1)
    lo = lane < HEAD
    x2 = x * x
    s_lo = jnp.sum(jnp.where(lo, x2, 0.0), axis=-1, keepdims=True)
    s_hi = jnp.sum(jnp.where(lo, 0.0, x2), axis=-1, keepdims=True)
    return jnp.where(lo, s_lo, s_hi) * (1.0 / HEAD)


def _per_head(fn, x):
    n = x.shape[1] // LANES
    return jnp.concatenate([fn(x[:, i * LANES:(i + 1) * LANES], i) for i in range(n)], axis=1)


def _adaln_kernel(cv_ref, w_ref, b_ref, o_ref):
    o_ref[0] = _mm_exact(_silu(cv_ref[...]), w_ref[0]) + b_ref[0]


def _adaln(cv, w_ada, b_ada):
    n_layers, d, _ = w_ada.shape
    rows = cv.shape[0]
    return pl.pallas_call(
        _adaln_kernel,
        grid=(n_layers, 6),
        in_specs=[pl.BlockSpec((rows, d), lambda l, j: (0, 0)),
                  pl.BlockSpec((1, d, d), lambda l, j: (l, 0, j)),
                  pl.BlockSpec((1, 1, d), lambda l, j: (l * 6 + j, 0, 0))],
        out_specs=pl.BlockSpec((1, rows, d), lambda l, j: (l * 6 + j, 0, 0)),
        out_shape=jax.ShapeDtypeStruct((n_layers * 6, rows, d), F32),
        compiler_params=_cparams(("arbitrary", "arbitrary")),
        name="adaln",
    )(cv, w_ada, b_ada.reshape(n_layers * 6, 1, d))


def _inproj_kernel(x_ref, sh_ref, sc_ref, g_ref, w_ref, wabt_ref, gq_ref, gk_ref, cos_ref, sin_ref,
                   qkv_ref, gate_ref, ab_ref, abt_ref, lru_ref, q_ref, k_ref, v_ref):
    x = x_ref[...]
    ms = jnp.mean(x * x, axis=-1, keepdims=True)
    h = (x * lax.rsqrt(ms + EPS) * g_ref[0]) * (1.0 + sc_ref[0]) + sh_ref[0]
    hb = h.astype(BF16)
    p = jnp.dot(hb, w_ref[0], preferred_element_type=F32)
    o = 0
    qkv_ref[...] = p[:, o:o + P_QKV]
    o += P_QKV
    gate_ref[...] = p[:, o:o + P_GATE]
    o += P_GATE
    ab_ref[...] = p[:, o:o + P_AB]
    o += P_AB
    lru_ref[...] = p[:, o:o + P_LRU]
    o += P_LRU
    att = p[:, o:o + P_ATT]
    abt_ref[...] = lax.dot_general(wabt_ref[0], hb, (((1,), (1,)), ((), ())), preferred_element_type=F32)

    cos = cos_ref[...]
    sin = sin_ref[...]
    lane = lax.broadcasted_iota(jnp.int32, cos.shape, 1)
    first_half = (lane & 16) == 0

    def norm_rope(gain):
        def fn(s, _):
            y = s * lax.rsqrt(_group_mean_square(s) + EPS) * gain
            swapped = jnp.where(first_half, pltpu.roll(y, LANES - 16, 1), pltpu.roll(y, 16, 1))
            return y * cos + swapped * sin
        return fn

    q = _per_head(norm_rope(gq_ref[0]), att[:, :ATT_W])
    k = _per_head(norm_rope(gk_ref[0]), att[:, ATT_W:ATT_W + ATT_KV_W])
    q_ref[...] = (q * (HEAD ** -0.5)).astype(BF16)
    k_ref[...] = k.astype(BF16)
    v_ref[...] = att[:, ATT_W + ATT_KV_W:].astype(BF16)


def _inproj(layer, x, mod, norm1_g, w_pack, wabt, gq, gk, cos_t, sin_t, dims):
    b, c, s, d, tm, nt = dims
    n = x.shape[0]
    rows = mod.shape[0] // (6 * norm1_g.shape[0])

    def mod_idx(chunk):
        def f(i):
            r = jnp.where(i % nt == 0, b, i // nt)
            return ((layer * 6 + chunk) * rows + r, 0, 0)
        return f

    row_spec = lambda w: pl.BlockSpec((tm, w), lambda i: (i, 0))
    return pl.pallas_call(
        _inproj_kernel,
        grid=(n // tm,),
        in_specs=[row_spec(d),
                  pl.BlockSpec((1, 1, d), mod_idx(0)),
                  pl.BlockSpec((1, 1, d), mod_idx(1)),
                  pl.BlockSpec((1, 1, d), lambda i: (layer, 0, 0)),
                  pl.BlockSpec((1, d, P_ALL), lambda i: (layer, 0, 0)),
                  pl.BlockSpec((1, 16, d), lambda i: (layer, 0, 0)),
                  pl.BlockSpec((1, 1, LANES), lambda i: (layer, 0, 0)),
                  pl.BlockSpec((1, 1, LANES), lambda i: (layer, 0, 0)),
                  pl.BlockSpec((tm, LANES), lambda i: (i % nt, 0)),
                  pl.BlockSpec((tm, LANES), lambda i: (i % nt, 0))],
        out_specs=[row_spec(P_QKV), row_spec(P_GATE), row_spec(P_AB),
                   pl.BlockSpec((16, tm), lambda i: (0, i)),
                   row_spec(P_LRU), row_spec(ATT_W), row_spec(ATT_KV_W), row_spec(ATT_KV_W)],
        out_shape=[jax.ShapeDtypeStruct((n, P_QKV), F32), jax.ShapeDtypeStruct((n, P_GATE), F32),
                   jax.ShapeDtypeStruct((n, P_AB), F32), jax.ShapeDtypeStruct((16, n), F32),
                   jax.ShapeDtypeStruct((n, P_LRU), F32), jax.ShapeDtypeStruct((n, ATT_W), BF16),
                   jax.ShapeDtypeStruct((n, ATT_KV_W), BF16), jax.ShapeDtypeStruct((n, ATT_KV_W), BF16)],
        compiler_params=_cparams(("arbitrary",)),
        name="inproj",
    )(x, mod, mod, norm1_g, w_pack, wabt, gq, gk, cos_t, sin_t)


def _conv_rows(xp_ref, w, r0, rt, c):
    base = r0 + PAD_ROWS
    xm1 = xp_ref[base - 1:base - 1 + rt, :]
    x0 = xp_ref[base:base + rt, :]
    xp1 = xp_ref[base + 1:base + 1 + rt, :]
    xp2 = xp_ref[base + 2:base + 2 + rt, :]
    row = r0 + lax.broadcasted_iota(jnp.int32, (rt, 1), 0)
    xm1 = jnp.where(row == c, 0.0, xm1)
    xp1 = jnp.where(row == c - 1, 0.0, xp1)
    xp2 = jnp.where((row == c - 1) | (row == c - 2), 0.0, xp2)
    return w[0:1, :] * xm1 + w[1:2, :] * x0 + w[2:3, :] * xp1 + w[3:4, :] * xp2


def _fill_padded(xp_ref, x, t):
    zeros = jnp.zeros((PAD_ROWS, xp_ref.shape[1]), F32)
    xp_ref[0:PAD_ROWS, :] = zeros
    xp_ref[PAD_ROWS + t:2 * PAD_ROWS + t, :] = zeros
    xp_ref[PAD_ROWS:PAD_ROWS + t, :] = x


def _gdn_kernel(c, t, rt, qkv_ref, gate_ref, ab_ref, abt_ref, cw_ref, alog_r_ref, dt_r_ref, alog_c_ref,
                dt_c_ref, ng_ref, out_ref,
                xp_ref, q_s, k_s, v_s, g_s, gcf_s, gcb_s, grp_s, u_s, w_s, qk_s, qg_s, kt_s, eg_s, st_s, o_s):
    nch = t // CHUNK
    nc = c // CHUNK
    nl = nch - nc

    _fill_padded(xp_ref, qkv_ref[...], t)
    cw = cw_ref[0]

    def l2n(s, _):
        return s * lax.rsqrt(_group_mean_square(s) * HEAD + EPS)

    for r0 in range(0, t, rt):
        y = _silu(_conv_rows(xp_ref, cw, r0, rt, c))
        q_s[r0:r0 + rt, :] = _per_head(l2n, y[:, :GDN_W]) * (HEAD ** -0.5)
        k_s[r0:r0 + rt, :] = _per_head(l2n, y[:, GDN_W:2 * GDN_W])
        v_s[r0:r0 + rt, :] = y[:, 2 * GDN_W:]

    ab = ab_ref[...]
    lane = lax.broadcasted_iota(jnp.int32, ab.shape, 1)
    gval = -jnp.exp(alog_r_ref[0]) * _softplus(ab + dt_r_ref[0])
    g_s[...] = jnp.where(lane < 2 * GDN_H, gval, _sigmoid(ab))
    abt = abt_ref[...]
    g_t = -jnp.exp(alog_c_ref[0][:, 0:1]) * _softplus(abt + dt_c_ref[0][:, 0:1])

    ri = lax.broadcasted_iota(jnp.int32, (CHUNK, CHUNK), 0)
    ci = lax.broadcasted_iota(jnp.int32, (CHUNK, CHUNK), 1)
    lower_f = (ri >= ci).astype(F32)
    upper_f = (ri <= ci).astype(F32)
    row8 = lax.broadcasted_iota(jnp.int32, (2 * GDN_H, CHUNK), 0)
    for ch in range(nch):
        r0 = ch * CHUNK
        gch = g_s[r0:r0 + CHUNK, :]
        gcf_s[r0:r0 + CHUNK, :] = _mm_exact(lower_f, gch)
        gcb_s[r0:r0 + CHUNK, :] = _mm_exact(upper_f, gch)
        gtc = g_t[0:2 * GDN_H, r0:r0 + CHUNK]
        gr = jnp.where(row8 < GDN_H, _mm_exact(gtc, upper_f), _mm_exact(gtc, lower_f))
        for direction in range(2):
            rows = [gr[direction * GDN_H + hd:direction * GDN_H + hd + 1, :] for hd in range(GDN_H)]
            edge = CHUNK - 1 if direction == 0 else 0
            tots = [jnp.broadcast_to(r[:, edge:edge + 1], (1, CHUNK)) for r in rows]
            grp_s[ch, direction:direction + 1, :] = jnp.concatenate(rows, axis=1)
            grp_s[ch, 2 + direction:3 + direction, :] = jnp.concatenate(tots, axis=1)

    st_s[...] = jnp.zeros(st_s.shape, F32)
    o_s[...] = jnp.zeros(o_s.shape, F32)

    ri4 = lax.broadcasted_iota(jnp.int32, (CHUNK, GDN_W), 0)
    li4 = lax.broadcasted_iota(jnp.int32, (CHUNK, GDN_W), 1)
    pos4 = li4 & (CHUNK - 1)
    head_masks = [jnp.where((li4 >= hd * HEAD) & (li4 < (hd + 1) * HEAD), 1.0, 0.0).astype(BF16)
                  for hd in range(GDN_H)]
    lane_lo = lax.broadcasted_iota(jnp.int32, (CHUNK, LANES), 1) < HEAD
    bi = jnp.right_shift(lax.broadcasted_iota(jnp.int32, (GDN_W, GDN_W), 0), 6)
    bj = jnp.right_shift(lax.broadcasted_iota(jnp.int32, (GDN_W, GDN_W), 1), 6)
    same_head = bi == bj

    def block_diag(xb):
        return jnp.concatenate([xb * m for m in head_masks], axis=0)

    def expand4(x, base):
        cols = [jnp.broadcast_to(x[:, base + hd:base + hd + 1], (CHUNK, LANES)) for hd in range(GDN_H)]
        return jnp.concatenate([jnp.where(lane_lo, cols[0], cols[1]), jnp.where(lane_lo, cols[2], cols[3])],
                               axis=1)

    def solve_units(units):
        work = []
        for direction, ch in units:
            r0 = pl.multiple_of(ch * CHUNK, CHUNK)
            unit = direction * nch + ch
            q = q_s[pl.ds(r0, CHUNK), :]
            k = k_s[pl.ds(r0, CHUNK), :]
            v = v_s[pl.ds(r0, CHUNK), :]
            gcx = expand4((gcf_s if direction == 0 else gcb_s)[pl.ds(r0, CHUNK), :], direction * GDN_H)
            betax = expand4(g_s[pl.ds(r0, CHUNK), :], 2 * GDN_H + direction * GDN_H)
            grow = grp_s[ch, direction:direction + 1, :]
            gtot = grp_s[ch, 2 + direction:3 + direction, :]
            incl = (ri4 >= pos4) if direction == 0 else (ri4 <= pos4)
            strict = (ri4 > pos4) if direction == 0 else (ri4 < pos4)
            decay = jnp.exp(jnp.minimum(gcx - grow, 0.0))
            eg = jnp.exp(gcx)
            kb = k * betax
            prod = lax.dot_general(jnp.concatenate([q, kb], axis=0).astype(BF16), block_diag(k.astype(BF16)),
                                   (((1,), (1,)), ((), ())), preferred_element_type=F32)
            qk_s[unit] = jnp.where(incl, prod[:CHUNK] * decay, 0.0).astype(BF16)
            qg_s[unit] = (q * eg).astype(BF16)
            kt_s[unit] = (k * jnp.exp(gtot - gcx)).astype(BF16)
            eg_s[unit] = jnp.broadcast_to(jnp.exp(gtot), (SUBLANES, GDN_W))
            a_mat = jnp.where(strict, prod[CHUNK:] * decay, 0.0)
            work.append([unit, a_mat, v * betax, kb * eg])
        for stage in range(6):
            for item in work:
                _, p, ru, rw = item
                pb = p.astype(BF16)
                parts = [block_diag(ru.astype(BF16)), block_diag(rw.astype(BF16))]
                if stage < 5:
                    parts = [block_diag(pb)] + parts
                res = jnp.dot(pb, jnp.concatenate(parts, axis=1), preferred_element_type=F32)
                off = GDN_W if stage < 5 else 0
                du = res[:, off:off + GDN_W]
                dw = res[:, off + GDN_W:]
                item[1] = res[:, :GDN_W] if stage < 5 else None
                item[2] = ru - du if stage == 0 else ru + du
                item[3] = rw - dw if stage == 0 else rw + dw
        for unit, _, ru, rw in work:
            u_s[unit] = ru
            w_s[unit] = rw.astype(BF16)

    per_step = 4 if nch % 4 == 0 else 2

    def solve_step(i, carry):
        solve_units([(direction, per_step * i + j) for j in range(per_step) for direction in range(2)])
        return carry

    lax.fori_loop(0, nch // per_step, solve_step, 0)

    def advance(chunks):
        first = []
        for direction, ch in enumerate(chunks):
            unit = direction * nch + ch
            state = st_s[direction]
            res = jnp.dot(jnp.concatenate([w_s[unit], qg_s[unit]], axis=0), state.astype(BF16),
                          preferred_element_type=F32)
            v_new = (u_s[unit] - res[:CHUNK]).astype(BF16)
            first.append((unit, state, res[CHUNK:], v_new))
        for direction, (unit, state, o_state, v_new) in enumerate(first):
            r0 = pl.multiple_of(chunks[direction] * CHUNK, CHUNK)
            o = o_state + jnp.dot(qk_s[unit], block_diag(v_new), preferred_element_type=F32)
            upd = lax.dot_general(kt_s[unit], v_new, (((0,), (0,)), ((), ())), preferred_element_type=F32)
            st_s[direction] = state * eg_s[unit, 0:1, :] + jnp.where(same_head, upd, 0.0)
            o_s[pl.ds(r0, CHUNK), :] = o_s[pl.ds(r0, CHUNK), :] + o

    def ctx_step(s, carry):
        advance((s, nc - 1 - s))
        return carry

    def lat_step(s, carry):
        advance((nc + s, nch - 1 - s))
        return carry

    lax.fori_loop(0, nc, ctx_step, 0)
    lax.fori_loop(0, nl, lat_step, 0)

    ng = ng_ref[0]

    def rms(s, i):
        return s * lax.rsqrt(_group_mean_square(s) + EPS) * ng[:, i * LANES:(i + 1) * LANES]

    for r0 in range(0, t, rt):
        y = _per_head(rms, o_s[r0:r0 + rt, :]) * _silu(gate_ref[r0:r0 + rt, :])
        out_ref[r0:r0 + rt, :] = y.astype(BF16)


def _gdn(layer, qkv, gate, ab, abt, conv_w, alog_r, dt_r, alog_c, dt_c, ng, dims):
    b, c, s, d, tm, nt = dims
    t = c + s
    n = qkv.shape[0]
    lay = lambda shape: pl.BlockSpec(shape, lambda i: (layer,) + (0,) * (len(shape) - 1))
    nch = t // CHUNK
    assert nch % 2 == 0
    once = pl.Buffered(1)
    return pl.pallas_call(
        functools.partial(_gdn_kernel, c, t, tm),
        grid=(b,),
        in_specs=[pl.BlockSpec((t, P_QKV), lambda i: (i, 0), pipeline_mode=once),
                  pl.BlockSpec((t, P_GATE), lambda i: (i, 0), pipeline_mode=once),
                  pl.BlockSpec((t, P_AB), lambda i: (i, 0)),
                  pl.BlockSpec((16, t), lambda i: (0, i)),
                  lay((1, CONV_K, P_QKV)), lay((1, 1, LANES)), lay((1, 1, LANES)),
                  lay((1, 16, LANES)), lay((1, 16, LANES)), lay((1, 1, GDN_W))],
        out_specs=pl.BlockSpec((t, GDN_W), lambda i: (i, 0)),
        out_shape=jax.ShapeDtypeStruct((n, GDN_W), BF16),
        scratch_shapes=[pltpu.VMEM((t + 2 * PAD_ROWS, P_QKV), F32),
                        pltpu.VMEM((t, GDN_W), F32), pltpu.VMEM((t, GDN_W), F32), pltpu.VMEM((t, GDN_W), F32),
                        pltpu.VMEM((t, LANES), F32), pltpu.VMEM((t, LANES), F32), pltpu.VMEM((t, LANES), F32),
                        pltpu.VMEM((nch, SUBLANES, GDN_W), F32),
                        pltpu.VMEM((2 * nch, CHUNK, GDN_W), F32),
                        pltpu.VMEM((2 * nch, CHUNK, GDN_W), BF16), pltpu.VMEM((2 * nch, CHUNK, GDN_W), BF16),
                        pltpu.VMEM((2 * nch, CHUNK, GDN_W), BF16), pltpu.VMEM((2 * nch, CHUNK, GDN_W), BF16),
                        pltpu.VMEM((2 * nch, SUBLANES, GDN_W), F32),
                        pltpu.VMEM((2, GDN_W, GDN_W), F32),
                        pltpu.VMEM((t, GDN_W), F32)],
        compiler_params=_cparams(("arbitrary",)),
        name="gdn",
    )(qkv, gate, ab, abt, conv_w, alog_r, dt_r, alog_c, dt_c, ng)


def _lru_kernel(c, t, rt, p_ref, cw_ref, cb_ref, wr_ref, br_ref, wi_ref, bi_ref, lam_ref, out_ref,
                xp_ref, a_s, b_s, h_s):
    _fill_padded(xp_ref, p_ref[:, :LRU_W], t)
    cw = cw_ref[0]
    cb = cb_ref[0]
    for r0 in range(0, t, rt):
        xr = _conv_rows(xp_ref, cw, r0, rt, c) + cb
        for dirn in range(2):
            r = _sigmoid(_mm(xr, wr_ref[0, dirn]) + br_ref[0, dirn:dirn + 1, :])
            i = _sigmoid(_mm(xr, wi_ref[0, dirn]) + bi_ref[0, dirn:dirn + 1, :])
            log_a = (-LRU_C) * r * _softplus(-lam_ref[0, dirn:dirn + 1, :])
            a = jnp.exp(log_a)
            a_s[dirn, r0:r0 + rt, :] = a
            b_s[dirn, r0:r0 + rt, :] = jnp.sqrt(1.0 - a * a) * (i * xr)

    row = lax.broadcasted_iota(jnp.int32, (SUBLANES, LRU_W), 0)

    def tile_scan(dirn, i, h_prev):
        r0 = pl.multiple_of(i * SUBLANES, SUBLANES)
        a = a_s[dirn, pl.ds(r0, SUBLANES), :]
        bx = b_s[dirn, pl.ds(r0, SUBLANES), :]
        for sh in (1, 2, 4):
            if dirn == 0:
                a_sh, b_sh, m = pltpu.roll(a, sh, 0), pltpu.roll(bx, sh, 0), row >= sh
            else:
                a_sh, b_sh, m = (pltpu.roll(a, SUBLANES - sh, 0), pltpu.roll(bx, SUBLANES - sh, 0),
                                 row < SUBLANES - sh)
            bx = jnp.where(m, a * b_sh + bx, bx)
            a = jnp.where(m, a * a_sh, a)
        h = a * h_prev + bx
        h_s[dirn, pl.ds(r0, SUBLANES), :] = h
        return h[SUBLANES - 1:SUBLANES, :] if dirn == 0 else h[0:1, :]

    n_t = t // SUBLANES
    n_c = c // SUBLANES
    zero = jnp.zeros((1, LRU_W), F32)

    def fwd_step(i, carry):
        hf, hb = carry
        hf = tile_scan(0, i, hf)
        hb = tile_scan(1, jnp.where(i < n_c, n_c - 1 - i, n_t - 1 - (i - n_c)), hb)
        return hf, hb

    lax.fori_loop(0, n_t, fwd_step, (zero, zero))

    for r0 in range(0, t, rt):
        yb = p_ref[r0:r0 + rt, LRU_W:]
        gelu = 0.5 * yb * (1.0 + jnp.tanh(0.7978845608028654 * (yb + 0.044715 * (yb * yb * yb))))
        out_ref[r0:r0 + rt, :] = ((h_s[0, r0:r0 + rt, :] + h_s[1, r0:r0 + rt, :]) * gelu).astype(BF16)


def _lru(layer, p, conv_w, conv_b, wr, br, wi, bi, lam, dims):
    b, c, s, d, tm, nt = dims
    t = c + s
    n = p.shape[0]
    lay = lambda shape: pl.BlockSpec(shape, lambda i: (layer,) + (0,) * (len(shape) - 1))
    return pl.pallas_call(
        functools.partial(_lru_kernel, c, t, tm),
        grid=(b,),
        in_specs=[pl.BlockSpec((t, P_LRU), lambda i: (i, 0)),
                  lay((1, CONV_K, LRU_W)), lay((1, 1, LRU_W)),
                  lay((1, 2, LRU_W, LRU_W)), lay((1, 2, LRU_W)),
                  lay((1, 2, LRU_W, LRU_W)), lay((1, 2, LRU_W)), lay((1, 2, LRU_W))],
        out_specs=pl.BlockSpec((t, LRU_W), lambda i: (i, 0)),
        out_shape=jax.ShapeDtypeStruct((n, LRU_W), BF16),
        scratch_shapes=[pltpu.VMEM((t + 2 * PAD_ROWS, LRU_W), F32),
                        pltpu.VMEM((2, t, LRU_W), F32), pltpu.VMEM((2, t, LRU_W), F32),
                        pltpu.VMEM((2, t, LRU_W), F32)],
        compiler_params=_cparams(("arbitrary",)),
        name="lru",
    )(p, conv_w, conv_b, wr, br, wi, bi, lam)


def _attn_rows(q, k, v):
    outs = []
    for hq in range(ATT_W // HEAD):
        kv = hq // ATT_GROUP
        qh = q[:, hq * HEAD:(hq + 1) * HEAD]
        kh = k[:, kv * HEAD:(kv + 1) * HEAD]
        vh = v[:, kv * HEAD:(kv + 1) * HEAD]
        s = lax.dot_general(qh, kh, (((1,), (1,)), ((), ())), preferred_element_type=F32)
        m = jnp.max(s, axis=-1, keepdims=True)
        p = jnp.exp(s - m)
        l = jnp.sum(p, axis=-1, keepdims=True)
        o = jnp.dot(p.astype(BF16), vh, preferred_element_type=F32)
        outs.append(o * (1.0 / l))
    return jnp.concatenate(outs, axis=1)


def _attn_kernel(c, with_ctx, q_ref, k_ref, v_ref, o_ref):
    def latent():
        o_ref[...] = _attn_rows(q_ref[...], k_ref[...], v_ref[...]).astype(BF16)

    if not with_ctx:
        latent()
        return
    j = pl.program_id(1)

    @pl.when(j == 0)
    def _():
        o_ref[...] = _attn_rows(q_ref[...], k_ref[0:c, :], v_ref[0:c, :]).astype(BF16)

    pl.when(j > 0)(latent)


def _attn(q, k, v, with_ctx, dims):
    b, c, s, d, tm, nt = dims
    t = c + s
    n = q.shape[0]
    off = 0 if with_ctx else 1
    return pl.pallas_call(
        functools.partial(_attn_kernel, c, with_ctx),
        grid=(b, nt - off),
        in_specs=[pl.BlockSpec((tm, ATT_W), lambda i, j: (i * nt + j + off, 0)),
                  pl.BlockSpec((t, ATT_KV_W), lambda i, j: (i, 0)),
                  pl.BlockSpec((t, ATT_KV_W), lambda i, j: (i, 0))],
        out_specs=pl.BlockSpec((tm, ATT_W), lambda i, j: (i * (nt - off) + j, 0)),
        out_shape=jax.ShapeDtypeStruct((b * (nt - off) * tm, ATT_W), BF16),
        compiler_params=_cparams(("arbitrary", "arbitrary")),
        name="attn",
    )(q, k, v)


R_E1, R_E2, R_G1, R_G2, R_RANK1, R_RANK2 = range(6)


def _outproj_kernel(tm, a_ref, b_ref, c_ref, x_ref, g1_ref, sh_ref, sc_ref, ng_ref, wo_ref, wr_ref, br_ref,
                    xo_ref, h_ref, route_ref, cnt_ref, carry_s):
    i = pl.program_id(0)

    @pl.when(i == 0)
    def _():
        carry_s[...] = jnp.zeros(carry_s.shape, F32)

    mix = (jnp.dot(a_ref[...], wo_ref[0, 0:GDN_W, :], preferred_element_type=F32)
           + jnp.dot(b_ref[...], wo_ref[0, GDN_W:GDN_W + LRU_W, :], preferred_element_type=F32)
           + jnp.dot(c_ref[...], wo_ref[0, GDN_W + LRU_W:, :], preferred_element_type=F32))
    x = x_ref[...] + g1_ref[0] * mix
    xo_ref[...] = x
    ms = jnp.mean(x * x, axis=-1, keepdims=True)
    h = (x * lax.rsqrt(ms + EPS) * ng_ref[0]) * (1.0 + sc_ref[0]) + sh_ref[0]
    h_ref[...] = h

    h_hi = h.astype(BF16)
    h_lo = (h - h_hi.astype(F32)).astype(BF16)
    logits = jnp.dot(jnp.concatenate([h_hi, h_lo, h_hi], axis=1), wr_ref[0],
                     preferred_element_type=F32) + br_ref[0]
    lane_i = lax.broadcasted_iota(jnp.int32, logits.shape, 1)
    lane = lane_i.astype(F32)
    lane_group = jnp.right_shift(lane_i, 3).astype(F32)
    big = jnp.float32(4 * LANES)
    neg = jnp.float32(-jnp.inf)
    is_group = (lane_i >= N_EXPERTS) & (lane_i < N_EXPERTS + N_GROUPS)
    gl = jnp.where(is_group, logits, neg)
    gm = jnp.max(gl, axis=-1, keepdims=True)
    pg_top = 1.0 / jnp.sum(jnp.where(is_group, jnp.exp(gl - gm), 0.0), axis=-1, keepdims=True)
    g_idx = jnp.min(jnp.where(gl == gm, lane, big), axis=-1, keepdims=True) - N_EXPERTS
    in_group = (lane_i < N_EXPERTS) & (lane_group == g_idx)
    le = jnp.where(in_group, logits, neg)
    m1 = jnp.max(le, axis=-1, keepdims=True)
    e1 = jnp.min(jnp.where(le == m1, lane, big), axis=-1, keepdims=True)
    le2 = jnp.where(lane == e1, neg, le)
    m2 = jnp.max(le2, axis=-1, keepdims=True)
    e2 = jnp.min(jnp.where(le2 == m2, lane, big), axis=-1, keepdims=True)
    z = jnp.sum(jnp.where(in_group, jnp.exp(le - m1), 0.0), axis=-1, keepdims=True)
    pe1 = 1.0 / z
    pe2 = jnp.exp(m2 - m1) / z
    gate1 = pg_top * pe1 / (pe1 + pe2)
    gate2 = pg_top * pe2 / (pe1 + pe2)

    sel1 = lane == e1
    sel2 = lane == e2
    onehot = jnp.where(sel1 | sel2, 1.0, 0.0)
    ri = lax.broadcasted_iota(jnp.int32, (tm, tm), 0)
    ci = lax.broadcasted_iota(jnp.int32, (tm, tm), 1)
    before = _mm(jnp.where(ri > ci, 1.0, 0.0), onehot) + carry_s[0:1, :]
    rank1 = jnp.sum(jnp.where(sel1, before, 0.0), axis=-1, keepdims=True)
    rank2 = jnp.sum(jnp.where(sel2, before, 0.0), axis=-1, keepdims=True)
    carry = carry_s[0:1, :] + jnp.sum(onehot, axis=0, keepdims=True)
    carry_s[...] = jnp.broadcast_to(carry, carry_s.shape)
    cnt_ref[...] = jnp.broadcast_to(carry, cnt_ref.shape)

    route = jnp.zeros(logits.shape, F32)
    for col, val in ((R_E1, e1), (R_E2, e2), (R_G1, gate1), (R_G2, gate2),
                     (R_RANK1, rank1), (R_RANK2, rank2)):
        route = jnp.where(lane_i == col, val, route)
    route_ref[...] = route


def _tile_map(with_ctx, nt):
    if with_ctx:
        return lambda i: i
    return lambda i: (i // (nt - 1)) * nt + 1 + i % (nt - 1)


def _outproj(layer, a, bm, cm, x, mod, norm2_g, w_out, w_route, b_route, with_ctx, dims):
    b, c, s, d, tm, nt = dims
    n = x.shape[0]
    n_layers = norm2_g.shape[0]
    rows = mod.shape[0] // (6 * n_layers)
    tile = _tile_map(with_ctx, nt)
    n_tiles = b * (nt if with_ctx else nt - 1)

    def mod_idx(chunk):
        def f(i):
            ti = tile(i)
            r = jnp.where(ti % nt == 0, b, ti // nt)
            return ((layer * 6 + chunk) * rows + r, 0, 0)
        return f

    row_spec = lambda w: pl.BlockSpec((tm, w), lambda i: (tile(i), 0))
    compact = lambda w: pl.BlockSpec((tm, w), lambda i: (i, 0))
    lay = lambda shape: pl.BlockSpec(shape, lambda i: (layer,) + (0,) * (len(shape) - 1))
    n_moe = n_tiles * tm
    return pl.pallas_call(
        functools.partial(_outproj_kernel, tm),
        grid=(n_tiles,),
        in_specs=[row_spec(GDN_W), row_spec(LRU_W), compact(ATT_W), row_spec(d),
                  pl.BlockSpec((1, 1, d), mod_idx(2)), pl.BlockSpec((1, 1, d), mod_idx(3)),
                  pl.BlockSpec((1, 1, d), mod_idx(4)),
                  lay((1, 1, d)), lay((1, d, d)), lay((1, 3 * d, LANES)), lay((1, 1, LANES))],
        out_specs=[row_spec(d), compact(d), compact(LANES),
                   pl.BlockSpec((SUBLANES, LANES), lambda i: (0, 0))],
        out_shape=[jax.ShapeDtypeStruct((n, d), F32), jax.ShapeDtypeStruct((n_moe, d), F32),
                   jax.ShapeDtypeStruct((n_moe, LANES), F32), jax.ShapeDtypeStruct((SUBLANES, LANES), F32)],
        scratch_shapes=[pltpu.VMEM((SUBLANES, LANES), F32)],
        input_output_aliases={3: 0},
        compiler_params=_cparams(("arbitrary",)),
        name="outproj",
    )(a, bm, cm, x, mod, mod, mod, norm2_g, w_out, w_route, b_route)


def _dest_kernel(route_ref, cnt_ref, dest_ref):
    route = route_ref[...]
    lane = lax.broadcasted_iota(jnp.int32, route.shape, 1)
    counts = cnt_ref[...]
    padded = jnp.floor((counts + (MOE_BLOCK - 1)) * (1.0 / MOE_BLOCK)) * MOE_BLOCK
    li = lax.broadcasted_iota(jnp.int32, (LANES, LANES), 0)
    lj = lax.broadcasted_iota(jnp.int32, (LANES, LANES), 1)
    start = _mm_exact(padded, jnp.where((li < lj) & (li < N_EXPERTS), 1.0, 0.0))[0:1, :]

    def col(j):
        return jnp.sum(jnp.where(lane == j, route, 0.0), axis=-1, keepdims=True)

    def slot(e, rank):
        return jnp.sum(jnp.where(lane == e.astype(jnp.int32), start, 0.0), axis=-1, keepdims=True) + rank

    d1 = slot(col(R_E1), col(R_RANK1))
    d2 = slot(col(R_E2), col(R_RANK2))
    dest_ref[...] = jnp.where(lane == 0, d1, jnp.where(lane == 1, d2, 0.0)).astype(jnp.int32)


def _dest(route, counts, dims):
    tm = dims[4]
    n = route.shape[0]
    return pl.pallas_call(
        _dest_kernel,
        grid=(n // tm,),
        in_specs=[pl.BlockSpec((tm, LANES), lambda i: (i, 0)),
                  pl.BlockSpec((SUBLANES, LANES), lambda i: (0, 0))],
        out_specs=pl.BlockSpec((tm, LANES), lambda i: (i, 0)),
        out_shape=jax.ShapeDtypeStruct((n, LANES), jnp.int32),
        compiler_params=_cparams(("arbitrary",)),
        name="dest",
    )(route, counts)


def _invert_kernel(n, d1_ref, d2_ref, spare_ref, slot_ref, sem):
    fill = pltpu.make_async_copy(spare_ref, slot_ref, sem)
    fill.start()
    fill.wait()

    def place(t, carry):
        slot_ref[d1_ref[t]] = t
        slot_ref[d2_ref[t]] = n + t
        return carry

    lax.fori_loop(0, n, place, 0, unroll=8)


def _invert(d1, d2, n_slots):
    n = d1.shape[0]
    spare = 2 * n + (jnp.arange(n_slots, dtype=jnp.int32) & (SPARE_ROWS - 1))
    return pl.pallas_call(
        functools.partial(_invert_kernel, n),
        grid_spec=pltpu.PrefetchScalarGridSpec(
            num_scalar_prefetch=2,
            grid=(1,),
            in_specs=[pl.BlockSpec(memory_space=pl.ANY)],
            out_specs=pl.BlockSpec(memory_space=pltpu.SMEM),
            scratch_shapes=[pltpu.SemaphoreType.DMA(())]),
        out_shape=jax.ShapeDtypeStruct((n_slots,), jnp.int32),
        compiler_params=_cparams(("arbitrary",)),
        name="invert",
    )(d1, d2, spare)


SPARE_ROWS = 2 * MOE_BLOCK


def _expert_kernel(be_ref, nact_ref, slot_ref, h_ref, w1_ref, w3_ref, w2_ref, y_ref,
                   w1_s, w3_s, w2_s, x_s, y_s, gsem, ssem):
    i = pl.program_id(0)
    nact = nact_ref[0]
    n = h_ref.shape[0]
    half = MOE_BLOCK // 2

    def gather_rows(blk, buf, rows):
        for r in rows:
            dst = slot_ref[blk * MOE_BLOCK + r]
            tok = jnp.minimum(jnp.where(dst >= n, dst - n, dst), n - 1)
            pltpu.make_async_copy(h_ref.at[pl.ds(tok, 1)], x_s.at[buf, pl.ds(r, 1)],
                                  gsem.at[buf]).start(priority=1)

    def scatter_rows(blk, buf, rows):
        for r in rows:
            dst = slot_ref[blk * MOE_BLOCK + r]
            pltpu.make_async_copy(y_s.at[buf, pl.ds(r, 1)], y_ref.at[pl.ds(dst, 1)],
                                  ssem.at[buf]).start(priority=r % 2)

    def wait_block(bufs, sem, buf):
        pltpu.make_async_copy(bufs.at[buf], bufs.at[buf], sem.at[buf]).wait()

    def compute(buf, between):
        x = x_s[buf].astype(BF16)
        between[0]()
        h1 = jnp.dot(x, w1_s[...], preferred_element_type=F32)
        between[1]()
        h3 = jnp.dot(x, w3_s[...], preferred_element_type=F32)
        between[2]()
        act = (_silu(h1) * h3).astype(BF16)
        y = jnp.dot(act, w2_s[...], preferred_element_type=F32)
        between[3]()
        return y

    @pl.when(i == 0)
    def _():
        y_s[...] = jnp.zeros(y_s.shape, F32)
        for buf in range(2):
            spare = pltpu.make_async_copy(y_s.at[buf], y_ref.at[pl.ds(2 * n + buf * MOE_BLOCK, MOE_BLOCK)],
                                          ssem.at[buf])
            spare.start()
            spare.wait()

    @pl.when((i < nact) & ((i == 0) | (be_ref[i] != be_ref[jnp.maximum(i - 1, 0)])))
    def _():
        w1_s[...] = w1_ref[0, 0].astype(BF16)
        w3_s[...] = w3_ref[0, 0].astype(BF16)
        w2_s[...] = w2_ref[0, 0].astype(BF16)

    xbuf = i % 3
    steady = (i >= 2) & (i + 2 < nact)
    for par in range(2):
        @pl.when(steady & (i % 2 == par))
        def _(par=par):
            wait_block(x_s, gsem, xbuf)
            nxt = (i + 2) % 3
            issue = [lambda: gather_rows(i + 2, nxt, range(0, half)),
                     lambda: gather_rows(i + 2, nxt, range(half, MOE_BLOCK)),
                     lambda: scatter_rows(i - 1, 1 - par, range(0, half)),
                     lambda: scatter_rows(i - 1, 1 - par, range(half, MOE_BLOCK))]
            y = compute(xbuf, issue)
            wait_block(y_s, ssem, par)
            y_s[par] = y

    @pl.when(jnp.logical_not(steady))
    def _():
        buf = i % 2
        for first in range(2):
            pl.when((i == 0) & (first < nact))(lambda first=first: gather_rows(first, first, range(MOE_BLOCK)))
        pl.when(i + 2 < nact)(lambda: gather_rows(i + 2, (i + 2) % 3, range(MOE_BLOCK)))
        pl.when((i >= 1) & (i - 1 < nact))(lambda: scatter_rows(i - 1, 1 - buf, range(MOE_BLOCK)))
        pl.when((i >= 2) & (i - 2 < nact))(lambda: wait_block(y_s, ssem, buf))

        @pl.when(i < nact)
        def _():
            wait_block(x_s, gsem, xbuf)
            y_s[buf] = compute(xbuf, [lambda: None] * 4)


def _experts(layer, block_e, nact, slot_src, h, w1, w3, w2):
    n = h.shape[0]
    d = w1.shape[-2]
    hid = w1.shape[-1]
    n_steps = block_e.shape[0]
    n_slots = slot_src.shape[0]
    wspec = lambda shape: pl.BlockSpec(shape, lambda i, be, na, sl: (layer, be[i], 0, 0))
    return pl.pallas_call(
        _expert_kernel,
        grid_spec=pltpu.PrefetchScalarGridSpec(
            num_scalar_prefetch=3,
            grid=(n_steps,),
            in_specs=[pl.BlockSpec(memory_space=pl.ANY),
                      wspec((1, 1, d, hid)), wspec((1, 1, d, hid)), wspec((1, 1, hid, d))],
            out_specs=pl.BlockSpec(memory_space=pl.ANY),
            scratch_shapes=[pltpu.VMEM((d, hid), BF16), pltpu.VMEM((d, hid), BF16),
                            pltpu.VMEM((hid, d), BF16),
                            pltpu.VMEM((3, MOE_BLOCK, d), F32), pltpu.VMEM((2, MOE_BLOCK, d), F32),
                            pltpu.SemaphoreType.DMA((3,)), pltpu.SemaphoreType.DMA((2,))]),
        out_shape=jax.ShapeDtypeStruct((2 * n + SPARE_ROWS, d), F32),
        compiler_params=_cparams(("arbitrary",)),
        name="experts",
    )(block_e, nact, slot_src, h, w1, w3, w2)


def _combine_kernel(final, x_ref, route_ref, g2_ref, fg_ref, y1_ref, y2_ref, o_ref):
    route = route_ref[...]
    lane = lax.broadcasted_iota(jnp.int32, route.shape, 1)
    gate1 = jnp.sum(jnp.where(lane == R_G1, route, 0.0), axis=-1, keepdims=True)
    gate2 = jnp.sum(jnp.where(lane == R_G2, route, 0.0), axis=-1, keepdims=True)
    x = x_ref[...] + g2_ref[0] * (y1_ref[...] * gate1 + y2_ref[...] * gate2)
    if final:
        ms = jnp.mean(x * x, axis=-1, keepdims=True)
        x = x * lax.rsqrt(ms + EPS) * fg_ref[...]
    o_ref[...] = x


def _combine(layer, x, route, mod, final_g, y, with_ctx, final, n_layers, dims):
    b, c, s, d, tm, nt = dims
    n = x.shape[0]
    rows = mod.shape[0] // (6 * n_layers)
    tile = _tile_map(with_ctx, nt)
    n_tiles = b * (nt if with_ctx else nt - 1)

    def mod_idx(i):
        ti = tile(i)
        r = jnp.where(ti % nt == 0, b, ti // nt)
        return ((layer * 6 + 5) * rows + r, 0, 0)

    if final:
        out_spec = pl.BlockSpec((tm, d), lambda i: (i, 0))
        out_shape = jax.ShapeDtypeStruct((n_tiles * tm, d), F32)
        aliases = {}
    else:
        out_spec = pl.BlockSpec((tm, d), lambda i: (tile(i), 0))
        out_shape = jax.ShapeDtypeStruct((n, d), F32)
        aliases = {0: 0}
    return pl.pallas_call(
        functools.partial(_combine_kernel, final),
        grid=(n_tiles,),
        in_specs=[pl.BlockSpec((tm, d), lambda i: (tile(i), 0)),
                  pl.BlockSpec((tm, LANES), lambda i: (i, 0)),
                  pl.BlockSpec((1, 1, d), mod_idx),
                  pl.BlockSpec((1, d), lambda i: (0, 0)),
                  pl.BlockSpec((tm, d), lambda i: (i, 0)),
                  pl.BlockSpec((tm, d), lambda i: (n_tiles + i, 0))],
        out_specs=out_spec,
        out_shape=out_shape,
        input_output_aliases=aliases,
        compiler_params=_cparams(("arbitrary",)),
        name="combine",
    )(x, route, mod, final_g, y, y)


def _rope_tables(s, tm):
    rows = s // GRID_W
    row = jnp.repeat(jnp.arange(rows, dtype=F32), GRID_W)
    col = jnp.tile(jnp.arange(GRID_W, dtype=F32), rows)
    axis_dim = HEAD // 2
    inv_freq = ROPE_THETA ** (-jnp.arange(0, axis_dim, 2, dtype=F32) / axis_dim)
    ar = row[:, None] * inv_freq
    ac = col[:, None] * inv_freq
    cos = jnp.concatenate([jnp.cos(ar), jnp.cos(ar), jnp.cos(ac), jnp.cos(ac)], axis=1)
    sin = jnp.concatenate([-jnp.sin(ar), jnp.sin(ar), -jnp.sin(ac), jnp.sin(ac)], axis=1)
    cos = jnp.concatenate([jnp.ones((tm, HEAD), F32), cos], axis=0)
    sin = jnp.concatenate([jnp.zeros((tm, HEAD), F32), sin], axis=0)
    return jnp.tile(cos, (1, 2)), jnp.tile(sin, (1, 2))


def _block_diag(w):
    n_layers = w.shape[0]
    eye = jnp.eye(LRU_BLOCKS, dtype=w.dtype)
    full = jnp.einsum('ldnij,nm->ldnimj', w, eye)
    return full.reshape(n_layers, 2, LRU_W, LRU_W)


def _pad_lanes(a, width=LANES):
    return jnp.pad(a, [(0, 0)] * (a.ndim - 1) + [(0, width - a.shape[-1])])


def kernel(x, c, ctx, c_ctx, w_ada, b_ada, norm1_g, norm2_g, w_in, w_out, gdn_conv_w, gdn_a_log, gdn_dt_bias, gdn_norm_g, lru_conv_w, lru_conv_b, lru_w_r, lru_b_r, lru_w_i, lru_b_i, lru_lambda, attn_q_norm_g, attn_k_norm_g, moe_w_group, moe_b_group, moe_w_expert, moe_b_expert, moe_w1, moe_w3, moe_w2, final_norm_g):
    bsz, s, d = x.shape
    cl = ctx.shape[1]
    n_layers = w_in.shape[0]
    tm = cl
    assert s % tm == 0 and tm % CHUNK == 0 and s % GRID_W == 0 and d == SUBLANES * LANES
    nt = (cl + s) // tm
    dims = (bsz, cl, s, d, tm, nt)

    o1 = 4 * GDN_W + 4 * GDN_H
    o2 = o1 + 2 * LRU_W
    w_ab = w_in[:, :, 4 * GDN_W:o1]
    w_pack = jnp.concatenate([w_in[:, :, :4 * GDN_W], _pad_lanes(w_ab), w_in[:, :, o1:o2], w_in[:, :, o2:]],
                             axis=-1).astype(BF16)
    wabt = jnp.swapaxes(w_ab, 1, 2).astype(BF16)
    gq = jnp.tile(attn_q_norm_g, (1, 2))[:, None, :]
    gk = jnp.tile(attn_k_norm_g, (1, 2))[:, None, :]
    cos_t, sin_t = _rope_tables(s, tm)
    alog = gdn_a_log.reshape(n_layers, 2 * GDN_H)
    dtb = gdn_dt_bias.reshape(n_layers, 2 * GDN_H)
    alog_r = _pad_lanes(alog)[:, None, :]
    dt_r = _pad_lanes(dtb)[:, None, :]
    alog_c = jnp.broadcast_to(_pad_lanes(alog, 16)[:, :, None], (n_layers, 16, LANES))
    dt_c = jnp.broadcast_to(_pad_lanes(dtb, 16)[:, :, None], (n_layers, 16, LANES))
    gdn_ng = jnp.tile(gdn_norm_g, (1, GDN_H))[:, None, :]
    wr_bd = _block_diag(lru_w_r).astype(BF16)
    wi_bd = _block_diag(lru_w_i).astype(BF16)
    w_out_b = w_out.astype(BF16)
    w_route = _pad_lanes(jnp.concatenate([moe_w_expert, moe_w_group], axis=-1))
    w_route_hi = w_route.astype(BF16)
    w_route_lo = (w_route - w_route_hi.astype(F32)).astype(BF16)
    w_route = jnp.concatenate([w_route_hi, w_route_hi, w_route_lo], axis=1)
    b_route = _pad_lanes(jnp.concatenate([moe_b_expert, moe_b_group], axis=-1))[:, None, :]

    cv = jnp.concatenate([c, c_ctx[None, :]], axis=0)
    rows = -(-cv.shape[0] // SUBLANES) * SUBLANES
    cv = jnp.pad(cv, ((0, rows - cv.shape[0]), (0, 0)))
    mod = _adaln(cv, w_ada, b_ada).reshape(n_layers * 6 * rows, 1, d)

    xf = jnp.concatenate([ctx, x], axis=1).reshape(bsz * (cl + s), d)
    out = None
    for layer in range(n_layers):
        with_ctx = layer < n_layers - 1
        qkv, gate, ab, abt, plru, q, k, v = _inproj(
            layer, xf, mod, norm1_g[:, None, :], w_pack, wabt, gq, gk, cos_t, sin_t, dims)
        mix_a = _gdn(layer, qkv, gate, ab, abt, gdn_conv_w, alog_r, dt_r, alog_c, dt_c, gdn_ng, dims)
        mix_b = _lru(layer, plru, lru_conv_w, lru_conv_b[:, None, :], wr_bd, lru_b_r, wi_bd, lru_b_i,
                     lru_lambda, dims)
        mix_c = _attn(q, k, v, with_ctx, dims)
        xf, h, route, counts = _outproj(layer, mix_a, mix_b, mix_c, xf, mod, norm2_g[:, None, :], w_out_b,
                                        w_route, b_route, with_ctx, dims)
        dest = _dest(route, counts, dims)
        d1 = dest[:, 0]
        d2 = dest[:, 1]
        n_tok = bsz * ((cl + s) if with_ctx else s)
        n_blocks = -(-(2 * n_tok) // MOE_BLOCK) + N_EXPERTS
        cnt = counts[0, :N_EXPERTS].astype(jnp.int32)
        padded = (cnt + MOE_BLOCK - 1) // MOE_BLOCK * MOE_BLOCK
        pad_end = jnp.cumsum(padded)
        block_row = jnp.arange(n_blocks + 2, dtype=jnp.int32) * MOE_BLOCK
        block_e = jnp.minimum(jnp.sum((pad_end[None, :] <= block_row[:, None]).astype(jnp.int32), axis=1),
                              N_EXPERTS - 1)
        nact = (pad_end[-1:] // MOE_BLOCK).astype(jnp.int32)
        slot_src = _invert(d1, d2, n_blocks * MOE_BLOCK)
        y = _experts(layer, block_e, nact, slot_src, h, moe_w1, moe_w3, moe_w2)
        res = _combine(layer, xf, route, mod, final_norm_g[None, :], y, with_ctx, not with_ctx, n_layers, dims)
        if with_ctx:
            xf = res
        else:
            out = res
    return out.reshape(bsz, s, d)
```
